```python
import jax, jax.numpy as jnp
from jax import lax
import numpy as np

D_MODEL = 1024
BATCH = 8
SEQ = 4096
DEPTH = 1
DEC_BATCH = 32
DEC_SEQ = 4
PAST_LEN = 16384
PAGE_SIZE = 128

HEAD_DIM = 64
RWKV_HEADS = D_MODEL // (2 * HEAD_DIM)
C_A = RWKV_HEADS * HEAD_DIM
R_DECAY = 32
R_ICLR = 32
R_GATE = 96
RW_IN = 3 * C_A + R_DECAY + R_ICLR + R_GATE
NSA_HEADS = D_MODEL // (2 * HEAD_DIM)
NSA_KV_HEADS = 2
NSA_GROUP = NSA_HEADS // NSA_KV_HEADS
C_B = NSA_HEADS * HEAD_DIM
KV_W = 2 * NSA_KV_HEADS * HEAD_DIM
CMP_LEN = 32
CMP_STRIDE = 16
CMP_RATIO = CMP_LEN // CMP_STRIDE
CMP_HIDDEN = HEAD_DIM
SEL_BLOCK = 64
N_SEL = 16
WINDOW = 512
Q_BLOCK = 64
OFF_Q = RW_IN
OFF_KVC = OFF_Q + C_B
OFF_KVS = OFF_KVC + KV_W
OFF_KVW = OFF_KVS + KV_W
OFF_NG = OFF_KVW + KV_W
OFF_MG = OFF_NG + 3 * NSA_HEADS
N_IN = OFF_MG + 2 * D_MODEL
N_GROUPS = 4
EXPERTS_PER_GROUP = 8
N_EXPERTS = N_GROUPS * EXPERTS_PER_GROUP
TOP_K = 2
D_EXPERT = D_MODEL // 2
MOE_BLOCK = 128
RMS_EPS = 1e-6
GN_EPS = 64e-5
NEG_INF = -1e30
FORCE_SCORE = 1e6

kernel_name = 'hybrid_rwkv7_nsa_hmoe_step'

F32 = jnp.float32


def rms_norm(x, g):
    xf = x.astype(F32)
    y = xf * lax.rsqrt(jnp.mean(xf * xf, axis=-1, keepdims=True) + RMS_EPS)
    return (y * g.astype(F32)).astype(x.dtype)


def masked_softmax(s, mask):
    s = jnp.where(mask, s, NEG_INF)
    e = jnp.where(mask, jnp.exp(s - jnp.max(s, axis=-1, keepdims=True)), 0.0)
    return e / jnp.maximum(jnp.sum(e, axis=-1, keepdims=True), 1e-30)


def wkv_scan(r, w, k, v, a, b, s0):
    def step(s, inp):
        r_t, w_t, k_t, v_t, a_t, b_t = inp
        sa = jnp.einsum('bhij,bhj->bhi', s, a_t)
        s = s * w_t[:, :, None, :] + sa[..., None] * b_t[:, :, None, :] + v_t[..., None] * k_t[:, :, None, :]
        return s, jnp.einsum('bhij,bhj->bhi', s, r_t)
    xs = tuple(jnp.moveaxis(t, 1, 0) for t in (r, w, k, v, a, b))
    s_fin, ys = lax.scan(step, s0, xs)
    return jnp.moveaxis(ys, 0, 1), s_fin


def rwkv_branch(z_a, shift_prev, wkv_prev, mu_shift, w0, w_decay_up, a0, w_iclr_up, w_gate_up,
                k_k, k_a, r_k, ln_x_w, ln_x_b):
    B, T, _ = z_a.shape
    zf = z_a.astype(F32)
    z_prev = jnp.concatenate([shift_prev.astype(F32)[:, None], zf[:, :-1]], axis=1)
    zm = zf + (z_prev - zf) * mu_shift.astype(F32)
    r, k, v, w_lo, a_lo, g_lo = jnp.split(
        zm, [C_A, 2 * C_A, 3 * C_A, 3 * C_A + R_DECAY, 3 * C_A + R_DECAY + R_ICLR], axis=-1)
    w_log = -jax.nn.softplus(-(w0.astype(F32) + jnp.tanh(w_lo) @ w_decay_up.astype(F32))) - 0.5
    decay = jnp.exp(-jnp.exp(w_log))
    a = jax.nn.sigmoid(a0.astype(F32) + a_lo @ w_iclr_up.astype(F32))
    g = jax.nn.sigmoid(g_lo) @ w_gate_up.astype(F32)
    heads = lambda t: t.reshape(B, T, RWKV_HEADS, HEAD_DIM)
    kk = heads(k * k_k.astype(F32))
    kk = kk / jnp.maximum(jnp.sqrt(jnp.sum(kk * kk, axis=-1, keepdims=True)), 1e-12)
    k = k * (1.0 + (a - 1.0) * k_a.astype(F32))
    r_h, k_h, v_h, a_h = heads(r), heads(k), heads(v), heads(a)
    y, wkv_new = wkv_scan(r_h, heads(decay), k_h, v_h, -kk, kk * a_h, wkv_prev.astype(F32))
    mean = jnp.mean(y, axis=-1, keepdims=True)
    var = jnp.mean(jnp.square(y - mean), axis=-1, keepdims=True)
    y = ((y - mean) * lax.rsqrt(var + GN_EPS)).reshape(B, T, C_A) * ln_x_w.astype(F32) + ln_x_b.astype(F32)
    bonus = jnp.sum(r_h * k_h * r_k.astype(F32), axis=-1, keepdims=True) * v_h
    y = (y + bonus.reshape(B, T, C_A)) * g
    return y, wkv_new.astype(wkv_prev.dtype), z_a[:, -1]


def nsa_inputs(z, q_norm, ks_norm, kw_norm):
    B, T, _ = z.shape
    kv_shape = (B, T, 2, NSA_KV_HEADS, HEAD_DIM)
    q = rms_norm(z[..., OFF_Q:OFF_KVC].reshape(B, T, NSA_HEADS, HEAD_DIM), q_norm)
    kv_c = z[..., OFF_KVC:OFF_KVS].reshape(kv_shape)
    kv_s = z[..., OFF_KVS:OFF_KVW].reshape(kv_shape)
    kv_w = z[..., OFF_KVW:OFF_NG].reshape(kv_shape)
    kv_s = jnp.stack([rms_norm(kv_s[:, :, 0], ks_norm), kv_s[:, :, 1]], axis=2)
    kv_w = jnp.stack([rms_norm(kv_w[:, :, 0], kw_norm), kv_w[:, :, 1]], axis=2)
    gates = jax.nn.sigmoid(z[..., OFF_NG:OFF_MG].astype(F32)).reshape(B, T, 3, NSA_HEADS)
    return q, kv_c, kv_s, kv_w, gates


def compress(rows, pe, w1, b1, w2):
    B, T = rows.shape[:2]
    nc = (T - CMP_LEN) // CMP_STRIDE + 1
    ch = rows[:, :(nc + CMP_RATIO - 1) * CMP_STRIDE].reshape(
        B, nc + CMP_RATIO - 1, CMP_STRIDE, NSA_KV_HEADS, HEAD_DIM)
    blocks = jnp.concatenate([ch[:, r:r + nc] for r in range(CMP_RATIO)], axis=2)
    hid = jax.nn.silu(jnp.einsum('bnlgd,ldf->bngf', blocks + pe[:, None, :], w1) + b1)
    return jnp.einsum('bngf,fd->bngd', hid, w2)


def compress_kv(kv_rows, kc_norm, pe_k, w1_k, b1_k, w2_k, pe_v, w1_v, b1_v, w2_v):
    kc = rms_norm(compress(kv_rows[:, :, 0], pe_k, w1_k, b1_k, w2_k), kc_norm)
    vc = compress(kv_rows[:, :, 1], pe_v, w1_v, b1_v, w2_v)
    return kc, vc


def nsa_core(q, q_pos, kc, vc, n_blocks, gather_sel, kw, vw, kw_pos, gates):
    B, Tq = q.shape[:2]
    nc = kc.shape[1]
    qg = q.astype(F32).reshape(B, Tq, NSA_KV_HEADS, NSA_GROUP, HEAD_DIM) * (HEAD_DIM ** -0.5)
    t = q_pos[:, None]
    c_idx = jnp.arange(nc)
    s_c = jnp.einsum('btgmd,bngd->btgmn', qg, kc.astype(F32))
    p_c = masked_softmax(s_c, (c_idx * CMP_STRIDE + CMP_LEN - 1 <= t)[None, :, None, None, :])
    o_c = jnp.einsum('btgmn,bngd->btgmd', p_c, vc.astype(F32))
    j = jnp.arange(n_blocks)
    c0 = c_idx[:, None] * CMP_STRIDE
    overlap = ((c0 < (j[None, :] + 1) * SEL_BLOCK) & (c0 + CMP_LEN > j[None, :] * SEL_BLOCK)).astype(F32)
    importance = jnp.einsum('btgmn,nj->btgj', p_c, overlap)
    cur = t // SEL_BLOCK
    valid = (j[None, :] * SEL_BLOCK <= t)[None, :, None, :]
    forced = ((j[None, :] == 0) | (j[None, :] == cur) | (j[None, :] == cur - 1))[None, :, None, :]
    score = jnp.where(valid, jnp.where(forced, FORCE_SCORE, importance), NEG_INF)
    _, idx = lax.top_k(score, min(N_SEL, n_blocks))
    k_g, v_g, pos_g = gather_sel(idx)
    s_s = jnp.einsum('btgmd,btgkd->btgmk', qg, k_g.astype(F32))
    p_s = masked_softmax(s_s, (pos_g <= q_pos[None, :, None, None])[:, :, :, None, :])
    o_s = jnp.einsum('btgmk,btgkd->btgmd', p_s, v_g.astype(F32))
    s_w = jnp.einsum('btgmd,bkgd->btgmk', qg, kw.astype(F32))
    kp = kw_pos[None, :]
    mask_w = (kp <= t) & (kp > t - WINDOW) & (kp >= 0)
    p_w = masked_softmax(s_w, mask_w[None, :, None, None, :])
    o_w = jnp.einsum('btgmk,bkgd->btgmd', p_w, vw.astype(F32))
    g = gates.reshape(B, Tq, 3, NSA_KV_HEADS, NSA_GROUP)[..., None]
    o = g[:, :, 0] * o_c + g[:, :, 1] * o_s + g[:, :, 2] * o_w
    return o.reshape(B, Tq, C_B)


def make_prompt_gather(kv_s):
    B = kv_s.shape[0]
    def gather(idx):
        pos = idx[..., None] * SEL_BLOCK + jnp.arange(SEL_BLOCK)
        bi = jnp.arange(B)[:, None, None, None, None]
        gi = jnp.arange(NSA_KV_HEADS)[None, None, :, None, None]
        k = kv_s[bi, pos, 0, gi]
        v = kv_s[bi, pos, 1, gi]
        flat = pos.shape[:3] + (-1,)
        return k.reshape(flat + (HEAD_DIM,)), v.reshape(flat + (HEAD_DIM,)), pos.reshape(flat)
    return gather


def make_paged_gather(pool, page_table, new_kv):
    DB, DS = new_kv.shape[:2]
    def gather(idx):
        pos = idx[..., None] * SEL_BLOCK + jnp.arange(SEL_BLOCK)
        bi = jnp.arange(DB)[:, None, None, None, None]
        gi = jnp.arange(NSA_KV_HEADS)[None, None, :, None, None]
        pc = jnp.minimum(pos, PAST_LEN - 1)
        page = page_table[bi, pc // PAGE_SIZE]
        off = pc % PAGE_SIZE
        npos = jnp.clip(pos - PAST_LEN, 0, DS - 1)
        past = (pos < PAST_LEN)[..., None]
        k = jnp.where(past, pool[page, off, 0, gi], new_kv[bi, npos, 0, gi])
        v = jnp.where(past, pool[page, off, 1, gi], new_kv[bi, npos, 1, gi])
        flat = pos.shape[:3] + (-1,)
        return k.reshape(flat + (HEAD_DIM,)), v.reshape(flat + (HEAD_DIM,)), pos.reshape(flat)
    return gather


def nsa_prompt(q, kc, vc, kv_s, kv_w, gates):
    B, T = q.shape[:2]
    n_blocks = T // SEL_BLOCK
    gather = make_prompt_gather(kv_s)
    kw_pad = jnp.pad(kv_w, ((0, 0), (WINDOW, 0), (0, 0), (0, 0), (0, 0)))
    def block(i):
        s0 = i * Q_BLOCK
        qb = lax.dynamic_slice_in_dim(q, s0, Q_BLOCK, axis=1)
        gb = lax.dynamic_slice_in_dim(gates, s0, Q_BLOCK, axis=1)
        wb = lax.dynamic_slice_in_dim(kw_pad, s0, WINDOW + Q_BLOCK, axis=1)
        q_pos = s0 + jnp.arange(Q_BLOCK)
        kw_pos = s0 - WINDOW + jnp.arange(WINDOW + Q_BLOCK)
        return nsa_core(qb, q_pos, kc, vc, n_blocks, gather, wb[:, :, 0], wb[:, :, 1], kw_pos, gb)
    o = lax.map(block, jnp.arange(T // Q_BLOCK))
    return o.transpose(1, 0, 2, 3).reshape(B, T, C_B)


def hier_moe(h, w_route_group, b_route_group, w_route_expert, b_route_expert, w_exp_gate, w_exp_up, w_exp_down):
    T = h.shape[0]
    g_logit = (h @ w_route_group).astype(F32) + b_route_group.astype(F32)
    g_sel = jnp.argmax(g_logit, axis=-1)
    g_w = jnp.take_along_axis(jax.nn.softmax(g_logit, axis=-1), g_sel[:, None], axis=1)
    e_logit = ((h @ w_route_expert).astype(F32) + b_route_expert.astype(F32)).reshape(T, N_GROUPS, EXPERTS_PER_GROUP)
    e_logit = jnp.take_along_axis(e_logit, g_sel[:, None, None], axis=1)[:, 0]
    top_v, top_i = lax.top_k(e_logit, TOP_K)
    wts = (jax.nn.softmax(top_v, axis=-1) * g_w).reshape(-1)
    expert = (g_sel[:, None] * EXPERTS_PER_GROUP + top_i).reshape(-1)
    tok = jnp.repeat(jnp.arange(T), TOP_K)
    n_slots = T * TOP_K
    order = jnp.argsort(expert)
    e_s, tok_s, w_s = expert[order], tok[order], wts[order]
    counts = jnp.bincount(expert, length=N_EXPERTS)
    padded = (counts + MOE_BLOCK - 1) // MOE_BLOCK * MOE_BLOCK
    p_end = jnp.cumsum(padded)
    dest = (p_end - padded)[e_s] + jnp.arange(n_slots) - (jnp.cumsum(counts) - counts)[e_s]
    n_blk = -(-n_slots // MOE_BLOCK) + N_EXPERTS
    xbuf = jnp.zeros((n_blk * MOE_BLOCK, D_MODEL), h.dtype).at[dest].set(h[tok_s])
    blk_e = jnp.minimum(jnp.searchsorted(p_end, jnp.arange(n_blk) * MOE_BLOCK, side='right'), N_EXPERTS - 1)
    def expert_block(args):
        xb, e = args
        return (jax.nn.silu(xb @ w_exp_gate[e]) * (xb @ w_exp_up[e])) @ w_exp_down[e]
    ybuf = lax.map(expert_block, (xbuf.reshape(n_blk, MOE_BLOCK, D_MODEL), blk_e)).reshape(n_blk * MOE_BLOCK, D_MODEL)
    out = jnp.zeros((T, D_MODEL), F32).at[tok_s].add(ybuf[dest].astype(F32) * w_s[:, None])
    return out.astype(h.dtype)


def merge_and_channel_mix(x, z, o_a, o_b, w_branch_a, w_branch_b, w_out, norm2, w_route_group, b_route_group,
                          w_route_expert, b_route_expert, w_exp_gate, w_exp_up, w_exp_down):
    dt = x.dtype
    B, T, _ = x.shape
    gate_a = jax.nn.sigmoid(z[..., OFF_MG:OFF_MG + D_MODEL].astype(F32))
    gate_b = jax.nn.sigmoid(z[..., OFF_MG + D_MODEL:OFF_MG + 2 * D_MODEL].astype(F32))
    merged = gate_a * (o_a.astype(dt) @ w_branch_a).astype(F32) + gate_b * (o_b.astype(dt) @ w_branch_b).astype(F32)
    x1 = x + (merged.astype(dt) @ w_out).astype(dt)
    h = rms_norm(x1, norm2).reshape(B * T, D_MODEL)
    y = hier_moe(h, w_route_group, b_route_group, w_route_expert, b_route_expert, w_exp_gate, w_exp_up, w_exp_down)
    return x1 + y.reshape(B, T, D_MODEL)


def decoder_layer(x_prompt, x_sample, cache_cmp_kv, cache_slc_kv, cache_win_kv, state_wkv, state_shift, page_table, p):
    (norm1, w_in, mu_shift, w0, w_decay_up, a0, w_iclr_up, w_gate_up, k_k, k_a, r_k, ln_x_w, ln_x_b,
     q_norm, kc_norm, ks_norm, kw_norm, cmp_pe_k, cmp_w1_k, cmp_b1_k, cmp_w2_k, cmp_pe_v, cmp_w1_v, cmp_b1_v,
     cmp_w2_v, w_branch_a, w_branch_b, w_out, norm2, w_route_group, b_route_group, w_route_expert,
     b_route_expert, w_exp_gate, w_exp_up, w_exp_down) = p
    rwkv_p = (mu_shift, w0, w_decay_up, a0, w_iclr_up, w_gate_up, k_k, k_a, r_k, ln_x_w, ln_x_b)
    cmp_p = (kc_norm, cmp_pe_k, cmp_w1_k, cmp_b1_k, cmp_w2_k, cmp_pe_v, cmp_w1_v, cmp_b1_v, cmp_w2_v)
    merge_p = (w_branch_a, w_branch_b, w_out, norm2, w_route_group, b_route_group, w_route_expert,
               b_route_expert, w_exp_gate, w_exp_up, w_exp_down)

    Bp, Tp, _ = x_prompt.shape
    zp = rms_norm(x_prompt, norm1) @ w_in
    oa_p, wkv_p, shift_p = rwkv_branch(
        zp[..., :RW_IN], jnp.zeros((Bp, RW_IN), zp.dtype),
        jnp.zeros((Bp, RWKV_HEADS, HEAD_DIM, HEAD_DIM), state_wkv.dtype), *rwkv_p)
    q_p, kvc_p, kvs_p, kvw_p, gates_p = nsa_inputs(zp, q_norm, ks_norm, kw_norm)
    kc_p, vc_p = compress_kv(kvc_p, *cmp_p)
    ob_p = nsa_prompt(q_p, kc_p, vc_p, kvs_p, kvw_p, gates_p)
    y_p = merge_and_channel_mix(x_prompt, zp, oa_p, ob_p, *merge_p)
    win_p = kvw_p[:, Tp - min(WINDOW, Tp):]

    Bs, Ts, _ = x_sample.shape
    zs = rms_norm(x_sample, norm1) @ w_in
    oa_s, wkv_s, shift_s = rwkv_branch(zs[..., :RW_IN], state_shift, state_wkv, *rwkv_p)
    q_s, kvc_s, kvs_s, kvw_s, gates_s = nsa_inputs(zs, q_norm, ks_norm, kw_norm)
    past_c = cache_cmp_kv[page_table].reshape((Bs, PAST_LEN, 2, NSA_KV_HEADS, HEAD_DIM))
    kc_s, vc_s = compress_kv(jnp.concatenate([past_c.astype(kvc_s.dtype), kvc_s], axis=1), *cmp_p)
    n_buf = cache_win_kv.shape[1]
    win_all = jnp.concatenate([cache_win_kv.astype(kvw_s.dtype), kvw_s], axis=1)
    q_pos = PAST_LEN + jnp.arange(Ts)
    kw_pos = PAST_LEN - n_buf + jnp.arange(n_buf + Ts)
    ob_s = nsa_core(q_s, q_pos, kc_s, vc_s, -(-(PAST_LEN + Ts) // SEL_BLOCK),
                    make_paged_gather(cache_slc_kv, page_table, kvs_s),
                    win_all[:, :, 0], win_all[:, :, 1], kw_pos, gates_s)
    y_s = merge_and_channel_mix(x_sample, zs, oa_s, ob_s, *merge_p)
    keep = min(WINDOW, n_buf + Ts)
    win_s = win_all[:, n_buf + Ts - keep:]
    return y_p, y_s, (kvc_p, kvs_p, win_p, wkv_p, shift_p, kvc_s, kvs_s, win_s, wkv_s, shift_s)


def setup_inputs(seed: int = 0) -> dict:
    key = jax.random.key(seed)
    keys = iter(jax.random.split(key, 64))
    nrm = lambda shape, scale: jax.random.normal(next(keys), shape, F32) * scale
    uni = lambda shape, lo, hi: jax.random.uniform(next(keys), shape, F32, lo, hi)
    L = DEPTH
    n_pages = PAST_LEN // PAGE_SIZE
    n_used = DEC_BATCH * n_pages
    n_pool = (n_used * 5) // 4
    return {
        'x_prompt': nrm((BATCH, SEQ, D_MODEL), 1.0),
        'x_sample': nrm((DEC_BATCH, DEC_SEQ, D_MODEL), 1.0),
        'cache_cmp_kv': nrm((L, n_pool, PAGE_SIZE, 2, NSA_KV_HEADS, HEAD_DIM), 1.0),
        'cache_slc_kv': nrm((L, n_pool, PAGE_SIZE, 2, NSA_KV_HEADS, HEAD_DIM), 1.0),
        'cache_win_kv': nrm((L, DEC_BATCH, min(WINDOW, PAST_LEN), 2, NSA_KV_HEADS, HEAD_DIM), 1.0),
        'state_wkv': nrm((L, DEC_BATCH, RWKV_HEADS, HEAD_DIM, HEAD_DIM), 0.5),
        'state_shift': nrm((L, DEC_BATCH, RW_IN), 1.0),
        'page_table': jax.random.permutation(next(keys), n_pool)[:n_used].reshape(DEC_BATCH, n_pages).astype(jnp.int32),
        'norm1': 1.0 + nrm((L, D_MODEL), 0.02),
        'w_in': nrm((L, D_MODEL, N_IN), D_MODEL ** -0.5),
        'mu_shift': uni((L, RW_IN), 0.0, 1.0),
        'w0': uni((L, C_A), -6.0, -0.5),
        'w_decay_up': nrm((L, R_DECAY, C_A), 0.5 * R_DECAY ** -0.5),
        'a0': nrm((L, C_A), 0.1),
        'w_iclr_up': nrm((L, R_ICLR, C_A), R_ICLR ** -0.5),
        'w_gate_up': nrm((L, R_GATE, C_A), R_GATE ** -0.5),
        'k_k': 0.85 + nrm((L, C_A), 0.05),
        'k_a': 1.0 + nrm((L, C_A), 0.05),
        'r_k': nrm((L, RWKV_HEADS, HEAD_DIM), 0.1),
        'ln_x_w': 1.0 + nrm((L, C_A), 0.02),
        'ln_x_b': nrm((L, C_A), 0.02),
        'q_norm': 1.0 + nrm((L, HEAD_DIM), 0.02),
        'kc_norm': 1.0 + nrm((L, HEAD_DIM), 0.02),
        'ks_norm': 1.0 + nrm((L, HEAD_DIM), 0.02),
        'kw_norm': 1.0 + nrm((L, HEAD_DIM), 0.02),
        'cmp_pe_k': nrm((L, CMP_LEN, HEAD_DIM), 0.1),
        'cmp_w1_k': nrm((L, CMP_LEN, HEAD_DIM, CMP_HIDDEN), (CMP_LEN * HEAD_DIM) ** -0.5),
        'cmp_b1_k': nrm((L, CMP_HIDDEN), 0.02),
        'cmp_w2_k': nrm((L, CMP_HIDDEN, HEAD_DIM), CMP_HIDDEN ** -0.5),
        'cmp_pe_v': nrm((L, CMP_LEN, HEAD_DIM), 0.1),
        'cmp_w1_v': nrm((L, CMP_LEN, HEAD_DIM, CMP_HIDDEN), (CMP_LEN * HEAD_DIM) ** -0.5),
        'cmp_b1_v': nrm((L, CMP_HIDDEN), 0.02),
        'cmp_w2_v': nrm((L, CMP_HIDDEN, HEAD_DIM), CMP_HIDDEN ** -0.5),
        'w_branch_a': nrm((L, C_A, D_MODEL), C_A ** -0.5),
        'w_branch_b': nrm((L, C_B, D_MODEL), C_B ** -0.5),
        'w_out': nrm((L, D_MODEL, D_MODEL), D_MODEL ** -0.5),
        'norm2': 1.0 + nrm((L, D_MODEL), 0.02),
        'w_route_group': nrm((L, D_MODEL, N_GROUPS), D_MODEL ** -0.5),
        'b_route_group': nrm((L, N_GROUPS), 0.01),
        'w_route_expert': nrm((L, D_MODEL, N_EXPERTS), D_MODEL ** -0.5),
        'b_route_expert': nrm((L, N_EXPERTS), 0.01),
        'w_exp_gate': nrm((L, N_EXPERTS, D_MODEL, D_EXPERT), D_MODEL ** -0.5),
        'w_exp_up': nrm((L, N_EXPERTS, D_MODEL, D_EXPERT), D_MODEL ** -0.5),
        'w_exp_down': nrm((L, N_EXPERTS, D_EXPERT, D_MODEL), D_EXPERT ** -0.5),
    }


def reference(x_prompt, x_sample, cache_cmp_kv, cache_slc_kv, cache_win_kv, state_wkv, state_shift, page_table,
              norm1, w_in, mu_shift, w0, w_decay_up, a0, w_iclr_up, w_gate_up, k_k, k_a, r_k, ln_x_w, ln_x_b,
              q_norm, kc_norm, ks_norm, kw_norm, cmp_pe_k, cmp_w1_k, cmp_b1_k, cmp_w2_k, cmp_pe_v, cmp_w1_v,
              cmp_b1_v, cmp_w2_v, w_branch_a, w_branch_b, w_out, norm2, w_route_group, b_route_group,
              w_route_expert, b_route_expert, w_exp_gate, w_exp_up, w_exp_down):
    layer_weights = (norm1, w_in, mu_shift, w0, w_decay_up, a0, w_iclr_up, w_gate_up, k_k, k_a, r_k, ln_x_w,
                     ln_x_b, q_norm, kc_norm, ks_norm, kw_norm, cmp_pe_k, cmp_w1_k, cmp_b1_k, cmp_w2_k, cmp_pe_v,
                     cmp_w1_v, cmp_b1_v, cmp_w2_v, w_branch_a, w_branch_b, w_out, norm2, w_route_group,
                     b_route_group, w_route_expert, b_route_expert, w_exp_gate, w_exp_up, w_exp_down)
    xp, xs = x_prompt, x_sample
    per_layer = []
    for l in range(DEPTH):
        xp, xs, st = decoder_layer(xp, xs, cache_cmp_kv[l], cache_slc_kv[l], cache_win_kv[l], state_wkv[l],
                                   state_shift[l], page_table, tuple(w[l] for w in layer_weights))
        per_layer.append(st)
    s = [jnp.stack([st[i] for st in per_layer]) for i in range(10)]
    return (xp, xs, s[0], s[1], s[2], s[3], s[4], s[5], s[6], s[7], s[8], s[9])
```

```python
import functools
import math

import jax
import jax.numpy as jnp
from jax import lax
from jax.experimental import pallas as pl
from jax.experimental.pallas import tpu as pltpu

F32 = jnp.float32
BF16 = jnp.bfloat16

D_MODEL = 1024
HEAD_DIM = 64
N_HEADS = 8
C_MIX = N_HEADS * HEAD_DIM
R_DECAY, R_ICLR, R_GATE = 32, 32, 96
RW_IN = 3 * C_MIX + R_DECAY + R_ICLR + R_GATE
KV_HEADS = 2
KV_GROUP = N_HEADS // KV_HEADS
KV_W = 2 * KV_HEADS * HEAD_DIM
CMP_LEN, CMP_STRIDE = 32, 16
SEL_BLOCK = 64
N_SEL = 16
WINDOW = 512
PAGE_SIZE = 128
N_GROUPS, EXPERTS_PER_GROUP = 4, 8
N_EXPERTS = N_GROUPS * EXPERTS_PER_GROUP
D_EXPERT = D_MODEL // 2
RMS_EPS = 1e-6
GN_EPS = 64e-5
NEG_INF = -1e30
FORCE_SCORE = 1e6

LANE = 128
VMEM_LIMIT = 56 * 1024 * 1024

RW_PAD = 1792
OFF_Q = RW_PAD
OFF_KVC = OFF_Q + C_MIX
OFF_KVS = OFF_KVC + KV_W
OFF_KVW = OFF_KVS + KV_W
OFF_NG = OFF_KVW + KV_W
OFF_MG = OFF_NG + LANE
N_IN_PAD = OFF_MG + 2 * D_MODEL
RW_TAIL = 3 * C_MIX


def _params(*sem):
    return pltpu.CompilerParams(dimension_semantics=sem, vmem_limit_bytes=VMEM_LIMIT)


def _dot(a, b):
    return jnp.dot(a, b, preferred_element_type=F32)


def _dot_nt(a, b):
    return lax.dot_general(a, b, (((1,), (1,)), ((), ())), preferred_element_type=F32)


def _dot_tn(a, b):
    return lax.dot_general(a, b, (((0,), (0,)), ((), ())), preferred_element_type=F32)


def _split2(x):
    hi = x.astype(BF16)
    lo = (x - hi.astype(F32)).astype(BF16)
    return hi, lo


def _split3(x):
    hi = x.astype(BF16)
    r1 = x - hi.astype(F32)
    mid = r1.astype(BF16)
    lo = (r1 - mid.astype(F32)).astype(BF16)
    return hi, mid, lo


def _gsum(y, g):
    hi, lo = _split2(y)
    return _dot(hi, g) + _dot(lo, g)


def _block_ones(n, blk):
    i = jnp.arange(n) // blk
    return (i[:, None] == i[None, :]).astype(BF16)


def _sigmoid(x):
    return 1.0 / (1.0 + jnp.exp(-x))


def _inproj_body(x_ref, n1_ref, w_ref, g512_ref, g128_ref, qn_ref, ksn_ref, kwn_ref,
                 zrw_ref, q_ref, kvc_ref, kvs_ref, kvw_ref, ng_ref, mg_ref):
    x = x_ref[...]
    ms = jnp.mean(x * x, axis=-1, keepdims=True)
    xn = (x * lax.rsqrt(ms + RMS_EPS) * n1_ref[...]).astype(BF16)

    def proj(a, b):
        return _dot(xn, w_ref[:, a:b])

    zrw_ref[...] = proj(0, RW_PAD)
    q = proj(OFF_Q, OFF_KVC)
    q_ref[...] = q * lax.rsqrt(_gsum(q * q, g512_ref[...]) * (1.0 / HEAD_DIM) + RMS_EPS) * qn_ref[...]
    kvc_ref[...] = proj(OFF_KVC, OFF_KVS)
    for off, nref, oref in ((OFF_KVS, ksn_ref, kvs_ref), (OFF_KVW, kwn_ref, kvw_ref)):
        kv = proj(off, off + KV_W)
        k = kv[:, :LANE]
        oref[:, :LANE] = k * lax.rsqrt(_gsum(k * k, g128_ref[...]) * (1.0 / HEAD_DIM) + RMS_EPS) * nref[...]
        oref[:, LANE:] = kv[:, LANE:]
    ng_ref[...] = _sigmoid(proj(OFF_NG, OFF_MG))
    mg_ref[...] = _sigmoid(proj(OFF_MG, N_IN_PAD))


def _in_proj(x2d, norm1, w_pad, q_norm, ks_norm, kw_norm, tm):
    n = x2d.shape[0]
    const = lambda shape: pl.BlockSpec(shape, lambda i: (0,) * len(shape))
    row = lambda w: pl.BlockSpec((tm, w), lambda i: (i, 0))
    widths = (RW_PAD, C_MIX, KV_W, KV_W, KV_W, LANE, 2 * D_MODEL)
    return pl.pallas_call(
        _inproj_body,
        grid=(n // tm,),
        in_specs=[row(D_MODEL), const((1, D_MODEL)), const((D_MODEL, N_IN_PAD)), const((C_MIX, C_MIX)),
                  const((LANE, LANE)), const((1, C_MIX)), const((1, LANE)), const((1, LANE))],
        out_specs=[row(w) for w in widths],
        out_shape=[jax.ShapeDtypeStruct((n, w), F32) for w in widths],
        compiler_params=_params("parallel"),
        name="in_proj",
    )(x2d, norm1.reshape(1, D_MODEL), w_pad, _block_ones(C_MIX, HEAD_DIM), _block_ones(LANE, HEAD_DIM),
      jnp.tile(q_norm, N_HEADS).reshape(1, C_MIX), jnp.tile(ks_norm, KV_HEADS).reshape(1, LANE),
      jnp.tile(kw_norm, KV_HEADS).reshape(1, LANE))


def _pad_w_in(w_in):
    d = w_in.shape[0]
    z = lambda n: jnp.zeros((d, n), w_in.dtype)
    o_q = RW_IN
    o_ng = o_q + C_MIX + 3 * KV_W
    o_mg = o_ng + 3 * N_HEADS
    return jnp.concatenate([w_in[:, :RW_IN], z(RW_PAD - RW_IN), w_in[:, o_q:o_ng], w_in[:, o_ng:o_mg],
                            z(LANE - 3 * N_HEADS), w_in[:, o_mg:]], axis=1).astype(BF16)


RW_CHUNK = 64


def _rwkv_body(t_valid, z_ref, sp_ref, s0_ref, mu_ref, w0_ref, a0_ref, kk_ref, ka_ref, rk_ref, lnw_ref, lnb_ref,
               wd_ref, wi_ref, wg_ref, g512_ref, o_ref, s_ref, prev_scr):
    C = RW_CHUNK
    c = pl.program_id(1)

    @pl.when(c == 0)
    def _():
        s_ref[...] = s0_ref[...]
        prev_scr[...] = sp_ref[0]

    z = z_ref[0]
    row = lax.broadcasted_iota(jnp.int32, (C, 1), 0)
    z_prev = jnp.where(row == 0, prev_scr[...], pltpu.roll(z, 1, axis=0))
    prev_scr[...] = z[C - 1:C]
    zm = z + (z_prev - z) * mu_ref[...]
    r = zm[:, 0:C_MIX]
    k = zm[:, C_MIX:2 * C_MIX]
    v = zm[:, 2 * C_MIX:3 * C_MIX]
    tail = zm[:, RW_TAIL:RW_PAD]
    w_lora = _dot(jnp.tanh(tail).astype(BF16), wd_ref[...])
    a_lora = _dot(tail.astype(BF16), wi_ref[...])
    g = _dot(_sigmoid(tail).astype(BF16), wg_ref[...])
    u = -(w0_ref[...] + w_lora)
    softplus = jnp.maximum(u, 0.0) + jnp.log(1.0 + jnp.exp(-jnp.abs(u)))
    w_log = -softplus - 0.5
    valid = (c * C + row) < t_valid
    ld = jnp.where(valid, -jnp.exp(w_log), 0.0)
    a = _sigmoid(a0_ref[...] + a_lora)
    kk = k * kk_ref[...]
    kk = kk / jnp.maximum(jnp.sqrt(_gsum(kk * kk, g512_ref[...])), 1e-12)
    k2 = k * (1.0 + (a - 1.0) * ka_ref[...])

    ci = lax.broadcasted_iota(jnp.int32, (C, C), 0)
    cj = lax.broadcasted_iota(jnp.int32, (C, C), 1)
    tri = (ci >= cj).astype(BF16)
    h1, h2, h3 = _split3(ld)
    cl = _dot(tri, h1) + _dot(tri, h2) + _dot(tri, h3)
    p_in = jnp.exp(cl)
    p_inv = jnp.exp(-cl)
    r_t = (r * p_in).astype(BF16)
    a_t = (-kk * jnp.exp(cl - ld)).astype(BF16)
    b_t = jnp.where(valid, kk * a * p_inv, 0.0).astype(BF16)
    k_t = jnp.where(valid, k2 * p_inv, 0.0).astype(BF16)
    p_end = p_in[C - 1:C]
    rkk = r * k2 * rk_ref[...]
    lower = ci > cj
    lower_eq = ci >= cj
    eye = (ci == cj).astype(F32)
    n_lvl = int(math.log2(C))

    for h in range(N_HEADS):
        sl = slice(h * HEAD_DIM, (h + 1) * HEAD_DIM)
        s0 = s_ref[0, h]
        s0b = s0.astype(BF16)
        vh = v[:, sl]
        vb = vh.astype(BF16)
        ah, rh, bh, kh = a_t[:, sl], r_t[:, sl], b_t[:, sl], k_t[:, sl]
        L = jnp.where(lower, _dot_nt(ah, bh), 0.0)
        Lak = jnp.where(lower, _dot_nt(ah, kh), 0.0)
        Mrb = jnp.where(lower_eq, _dot_nt(rh, bh), 0.0)
        Mrk = jnp.where(lower_eq, _dot_nt(rh, kh), 0.0)
        rhs = _dot_nt(ah, s0b) + _dot(Lak.astype(BF16), vb)
        X = eye + L
        Lp = L
        for _ in range(n_lvl - 1):
            Lpb = Lp.astype(BF16)
            Lp = _dot(Lpb, Lpb)
            X = X + _dot(X.astype(BF16), Lp.astype(BF16))
        U = _dot(X.astype(BF16), rhs.astype(BF16))
        Ub = U.astype(BF16)
        y = _dot_nt(rh, s0b) + _dot(Mrb.astype(BF16), Ub) + _dot(Mrk.astype(BF16), vb)
        s_ref[0, h] = (s0 + _dot_tn(Ub, bh) + _dot_tn(vb, kh)) * p_end[:, sl]
        mean = jnp.mean(y, axis=-1, keepdims=True)
        yc = y - mean
        var = jnp.mean(yc * yc, axis=-1, keepdims=True)
        yn = yc * lax.rsqrt(var + GN_EPS) * lnw_ref[:, sl] + lnb_ref[:, sl]
        bonus = jnp.sum(rkk[:, sl], axis=-1, keepdims=True) * vh
        o_ref[0, :, sl] = (yn + bonus) * g[:, sl]


def _rwkv(z_rw, shift_prev, s0, t_valid, mu, w0, wd, a0, wi, wg, k_k, k_a, r_k, ln_w, ln_b):
    B, T, _ = z_rw.shape
    C = RW_CHUNK
    const = lambda shape: pl.BlockSpec(shape, lambda b, c: (0,) * len(shape))
    vec = lambda p: p.reshape(1, C_MIX)
    pad_rows = lambda w, off: jnp.zeros((RW_PAD - RW_TAIL, C_MIX), F32).at[off:off + w.shape[0]].set(w).astype(BF16)
    state_spec = pl.BlockSpec((1, N_HEADS, HEAD_DIM, HEAD_DIM), lambda b, c: (b, 0, 0, 0))
    return pl.pallas_call(
        functools.partial(_rwkv_body, t_valid),
        grid=(B, T // C),
        in_specs=[pl.BlockSpec((1, C, RW_PAD), lambda b, c: (b, c, 0)),
                  pl.BlockSpec((1, 1, RW_PAD), lambda b, c: (b, 0, 0)),
                  state_spec, const((1, RW_PAD))] + [const((1, C_MIX))] * 7
                 + [const((RW_PAD - RW_TAIL, C_MIX))] * 3 + [const((C_MIX, C_MIX))],
        out_specs=[pl.BlockSpec((1, C, C_MIX), lambda b, c: (b, c, 0)), state_spec],
        out_shape=[jax.ShapeDtypeStruct((B, T, C_MIX), F32),
                   jax.ShapeDtypeStruct((B, N_HEADS, HEAD_DIM, HEAD_DIM), F32)],
        scratch_shapes=[pltpu.VMEM((1, RW_PAD), F32)],
        compiler_params=_params("parallel", "arbitrary"),
        name="rwkv7",
    )(z_rw, shift_prev, s0, jnp.pad(mu, (0, RW_PAD - RW_IN)).reshape(1, RW_PAD), vec(w0), vec(a0), vec(k_k),
      vec(k_a), vec(r_k), vec(ln_w), vec(ln_b), pad_rows(wd, 0), pad_rows(wi, R_DECAY),
      pad_rows(wg, R_DECAY + R_ICLR), _block_ones(C_MIX, HEAD_DIM))


CH_W = CMP_STRIDE * KV_W
N_SLOT = 2 * KV_HEADS


def _cmp_weights(pe_k, w1_k, b1_k, w2_k, pe_v, w1_v, b1_v, w2_v):
    eye = jnp.eye(N_SLOT, dtype=F32)
    w1 = jnp.stack([w1_k, w1_k, w1_v, w1_v])
    pe = jnp.stack([pe_k, pe_k, pe_v, pe_v])
    halves = []
    pes = []
    for r in range(CMP_LEN // CMP_STRIDE):
        ls = slice(r * CMP_STRIDE, (r + 1) * CMP_STRIDE)
        halves.append(jnp.einsum('sldf,st->lsdtf', w1[:, ls], eye).reshape(CH_W, N_SLOT * HEAD_DIM))
        pes.append(jnp.transpose(pe[:, ls], (1, 0, 2)).reshape(1, CH_W))
    wc = jnp.concatenate(halves, axis=1).astype(BF16)
    w2 = jnp.stack([w2_k, w2_k, w2_v, w2_v])
    w2b = jnp.einsum('sfd,st->sftd', w2, eye).reshape(N_SLOT * HEAD_DIM, N_SLOT * HEAD_DIM).astype(BF16)
    b1 = jnp.concatenate([b1_k, b1_k, b1_v, b1_v]).reshape(1, N_SLOT * HEAD_DIM)
    return wc, pes[0], pes[1], b1, w2b


def _cmp_proj_body(ch_ref, pea_ref, peb_ref, wc_ref, ab_ref):
    ch = ch_ref[...]
    w = N_SLOT * HEAD_DIM
    ab_ref[:, :w] = _dot((ch + pea_ref[...]).astype(BF16), wc_ref[:, :w])
    ab_ref[:, w:] = _dot((ch + peb_ref[...]).astype(BF16), wc_ref[:, w:])


def _cmp_proj(chunks, pea, peb, wc, tr):
    n = chunks.shape[0]
    const = lambda shape: pl.BlockSpec(shape, lambda i: (0,) * len(shape))
    return pl.pallas_call(
        _cmp_proj_body,
        grid=(n // tr,),
        in_specs=[pl.BlockSpec((tr, CH_W), lambda i: (i, 0)), const((1, CH_W)), const((1, CH_W)),
                  const((CH_W, 2 * N_SLOT * HEAD_DIM))],
        out_specs=pl.BlockSpec((tr, 2 * N_SLOT * HEAD_DIM), lambda i: (i, 0)),
        out_shape=jax.ShapeDtypeStruct((n, 2 * N_SLOT * HEAD_DIM), F32),
        compiler_params=_params("parallel"),
        name="cmp_proj",
    )(chunks, pea, peb, wc)


CMP_PAGES = 16


def _cmp_proj_paged_body(pt_ref, cache_ref, pea_ref, peb_ref, wc_ref, ab_ref, buf, sem):
    b = pl.program_id(0)
    g = pl.program_id(1)
    rows = PAGE_SIZE // CMP_STRIDE

    def page_copy(i):
        page = pt_ref[b, g * CMP_PAGES + i]
        return pltpu.make_async_copy(cache_ref.at[page], buf.at[pl.ds(i * rows, rows)], sem)

    for i in range(CMP_PAGES):
        page_copy(i).start()
    for i in range(CMP_PAGES):
        page_copy(i).wait()
    _cmp_proj_body(buf, pea_ref, peb_ref, wc_ref, ab_ref.at[0])


def _cmp_proj_paged(cache, page_table, pea, peb, wc):
    B, n_pages = page_table.shape
    rows = PAGE_SIZE // CMP_STRIDE
    const = lambda shape: pl.BlockSpec(shape, lambda b, g, pt: (0,) * len(shape))
    return pl.pallas_call(
        _cmp_proj_paged_body,
        grid_spec=pltpu.PrefetchScalarGridSpec(
            num_scalar_prefetch=1,
            grid=(B, n_pages // CMP_PAGES),
            in_specs=[pl.BlockSpec(memory_space=pl.ANY), const((1, CH_W)), const((1, CH_W)),
                      const((CH_W, 2 * N_SLOT * HEAD_DIM))],
            out_specs=pl.BlockSpec((1, CMP_PAGES * rows, 2 * N_SLOT * HEAD_DIM), lambda b, g, pt: (b, g, 0)),
            scratch_shapes=[pltpu.VMEM((CMP_PAGES * rows, CH_W), F32), pltpu.SemaphoreType.DMA(())],
        ),
        out_shape=jax.ShapeDtypeStruct((B, n_pages * rows, 2 * N_SLOT * HEAD_DIM), F32),
        compiler_params=_params("arbitrary", "arbitrary"),
        name="cmp_proj_paged",
    )(page_table, cache, pea, peb, wc)


def _cmp_finish_body(ab_ref, b1_ref, w2_ref, kcn_ref, g128_ref, kc_ref, vc_ref):
    ab = ab_ref[0]
    n = ab.shape[0]
    w = N_SLOT * HEAD_DIM
    pre = ab[:, :w] + pltpu.roll(ab[:, w:], n - 1, axis=0) + b1_ref[...]
    hid = pre * _sigmoid(pre)
    out = _dot(hid.astype(BF16), w2_ref[...])
    k = out[:, :LANE]
    kc_ref[0] = k * lax.rsqrt(_gsum(k * k, g128_ref[...]) * (1.0 / HEAD_DIM) + RMS_EPS) * kcn_ref[...]
    vc_ref[0] = out[:, LANE:]


def _cmp_finish(ab, b1, w2b, kc_norm):
    B, n, _ = ab.shape
    const = lambda shape: pl.BlockSpec(shape, lambda b: (0,) * len(shape))
    w = N_SLOT * HEAD_DIM
    return pl.pallas_call(
        _cmp_finish_body,
        grid=(B,),
        in_specs=[pl.BlockSpec((1, n, 2 * w), lambda b: (b, 0, 0)), const((1, w)), const((w, w)), const((1, LANE)),
                  const((LANE, LANE))],
        out_specs=[pl.BlockSpec((1, n, LANE), lambda b: (b, 0, 0))] * 2,
        out_shape=[jax.ShapeDtypeStruct((B, n, LANE), F32)] * 2,
        compiler_params=_params("parallel"),
        name="cmp_finish",
    )(ab, b1, w2b, jnp.tile(kc_norm, KV_HEADS).reshape(1, LANE), _block_ones(LANE, HEAD_DIM))


def _cmp_attn_body(pos0, nc, nb, nbp, q_ref, kc_ref, vc_ref, ovl_ref, o_ref, sel_ref):
    tq = q_ref.shape[1]
    ncp = kc_ref.shape[1]
    q = q_ref[0] * (HEAD_DIM ** -0.5)
    t = pos0 + pl.program_id(1) * tq + lax.broadcasted_iota(jnp.int32, (tq, 1), 0)
    cidx = lax.broadcasted_iota(jnp.int32, (1, ncp), 1)
    cmask = (cidx * CMP_STRIDE + (CMP_LEN - 1) <= t) & (cidx < nc)
    j = lax.broadcasted_iota(jnp.int32, (1, nbp), 1)
    cur = t // SEL_BLOCK
    valid = (j * SEL_BLOCK <= t) & (j < nb)
    forced = (j == 0) | (j == cur) | (j == cur - 1)
    for g in range(KV_HEADS):
        kcg = kc_ref[0, :, g * HEAD_DIM:(g + 1) * HEAD_DIM].astype(BF16)
        vcg = vc_ref[0, :, g * HEAD_DIM:(g + 1) * HEAD_DIM].astype(BF16)
        psum = jnp.zeros((tq, ncp), F32)
        for m in range(KV_GROUP):
            sl = slice((g * KV_GROUP + m) * HEAD_DIM, (g * KV_GROUP + m + 1) * HEAD_DIM)
            s = jnp.where(cmask, _dot_nt(q[:, sl].astype(BF16), kcg), NEG_INF)
            e = jnp.where(cmask, jnp.exp(s - jnp.max(s, axis=-1, keepdims=True)), 0.0)
            p = e / jnp.maximum(jnp.sum(e, axis=-1, keepdims=True), 1e-30)
            o_ref[0, :, sl] = _dot(p.astype(BF16), vcg)
            psum = psum + p
        imp = _gsum(psum, ovl_ref[...])
        score = jnp.where(valid, jnp.where(forced, FORCE_SCORE, imp), NEG_INF)
        cnt = jnp.zeros((tq, nbp), jnp.int32)
        for jp in range(nb):
            cj = score[:, jp:jp + 1]
            cnt = cnt + jnp.where(j > jp, jnp.where(cj >= score, 1, 0), jnp.where(cj > score, 1, 0))
        sel_ref[0, :, g * nbp:(g + 1) * nbp] = (cnt < N_SEL).astype(F32)


def _cmp_attn(q, kc, vc, pos0, nc, nb, tq):
    B, Tq, _ = q.shape
    ncp = kc.shape[1]
    nbp = -(-nb // SEL_BLOCK) * SEL_BLOCK
    c0 = jnp.arange(ncp)[:, None] * CMP_STRIDE
    jj = jnp.arange(nbp)[None, :]
    ovl = ((c0 < (jj + 1) * SEL_BLOCK) & (c0 + CMP_LEN > jj * SEL_BLOCK) & (jnp.arange(ncp)[:, None] < nc)
           & (jj < nb)).astype(BF16)
    return pl.pallas_call(
        functools.partial(_cmp_attn_body, pos0, nc, nb, nbp),
        grid=(B, Tq // tq),
        in_specs=[pl.BlockSpec((1, tq, C_MIX), lambda b, i: (b, i, 0)),
                  pl.BlockSpec((1, ncp, LANE), lambda b, i: (b, 0, 0)),
                  pl.BlockSpec((1, ncp, LANE), lambda b, i: (b, 0, 0)),
                  pl.BlockSpec((ncp, nbp), lambda b, i: (0, 0))],
        out_specs=[pl.BlockSpec((1, tq, C_MIX), lambda b, i: (b, i, 0)),
                   pl.BlockSpec((1, tq, KV_HEADS * nbp), lambda b, i: (b, i, 0))],
        out_shape=[jax.ShapeDtypeStruct((B, Tq, C_MIX), F32), jax.ShapeDtypeStruct((B, Tq, KV_HEADS * nbp), F32)],
        compiler_params=_params("parallel", "parallel"),
        name="cmp_attn",
    )(q, kc, vc, ovl)


def _softmax_step(carry, s, mask, vb):
    m_, l_, acc = carry
    s = jnp.where(mask, s, NEG_INF)
    m_new = jnp.maximum(m_, jnp.max(s, axis=-1, keepdims=True))
    alpha = jnp.exp(m_ - m_new)
    p = jnp.where(mask, jnp.exp(s - m_new), 0.0)
    return m_new, alpha * l_ + jnp.sum(p, axis=-1, keepdims=True), alpha * acc + _dot(p.astype(BF16), vb)


def _softmax_init(rows):
    return jnp.full((rows, 1), NEG_INF, F32), jnp.zeros((rows, 1), F32), jnp.zeros((rows, HEAD_DIM), F32)


def _stack_heads(x, g):
    return jnp.concatenate([x[:, (g * KV_GROUP + m) * HEAD_DIM:(g * KV_GROUP + m + 1) * HEAD_DIM]
                            for m in range(KV_GROUP)], axis=0)


def _gate_mix(ng, o_c, o_s, o_w, g, tq, o_ref, os_stacked=True):
    for m in range(KV_GROUP):
        h = g * KV_GROUP + m
        sl = slice(h * HEAD_DIM, (h + 1) * HEAD_DIM)
        rs = slice(m * tq, (m + 1) * tq)
        o_ref[0, :, sl] = (ng[:, h:h + 1] * o_c[:, sl]
                           + ng[:, N_HEADS + h:N_HEADS + h + 1] * (o_s[rs] if os_stacked else o_s[:, sl])
                           + ng[:, 2 * N_HEADS + h:2 * N_HEADS + h + 1] * o_w[rs])


ATT_TK = 128


def _nsa_prompt_body(q_ref, kvs_ref, kvw_ref, sel_ref, oc_ref, ng_ref, o_ref):
    tq = q_ref.shape[1]
    tk = ATT_TK
    nbp = sel_ref.shape[2] // KV_HEADS
    q0 = pl.program_id(1) * tq
    q = q_ref[0] * (HEAD_DIM ** -0.5)
    t = q0 + lax.broadcasted_iota(jnp.int32, (tq, 1), 0)
    blk = lax.broadcasted_iota(jnp.int32, (nbp, 1), 0)
    kio = lax.broadcasted_iota(jnp.int32, (1, tk), 1)
    for g in range(KV_HEADS):
        qg = _stack_heads(q, g).astype(BF16)
        selg = sel_ref[0, :, g * nbp:(g + 1) * nbp].astype(BF16)
        ks = slice(g * HEAD_DIM, (g + 1) * HEAD_DIM)
        vs = slice(LANE + g * HEAD_DIM, LANE + (g + 1) * HEAD_DIM)

        def sel_step(kt, carry):
            k0 = pl.multiple_of(kt * tk, tk)
            kv = kvs_ref[0, pl.ds(k0, tk), :]
            kpos = k0 + kio
            expand = (kpos // SEL_BLOCK == blk).astype(BF16)
            mask = (_dot(selg, expand) > 0.5) & (kpos <= t)
            mask = jnp.concatenate([mask] * KV_GROUP, axis=0)
            return _softmax_step(carry, _dot_nt(qg, kv[:, ks].astype(BF16)), mask, kv[:, vs].astype(BF16))

        _, l_s, acc_s = lax.fori_loop(0, (q0 + tq + tk - 1) // tk, sel_step, _softmax_init(KV_GROUP * tq))
        o_s = acc_s / jnp.maximum(l_s, 1e-30)

        def win_step(kt, carry):
            k0 = pl.multiple_of(kt * tk, tk)
            kv = kvw_ref[0, pl.ds(k0, tk), :]
            kpos = k0 + kio
            mask = (kpos <= t) & (kpos > t - WINDOW)
            mask = jnp.concatenate([mask] * KV_GROUP, axis=0)
            return _softmax_step(carry, _dot_nt(qg, kv[:, ks].astype(BF16)), mask, kv[:, vs].astype(BF16))

        first = jnp.maximum(q0 - WINDOW + 1, 0) // tk
        _, l_w, acc_w = lax.fori_loop(first, (q0 + tq + tk - 1) // tk, win_step, _softmax_init(KV_GROUP * tq))
        o_w = acc_w / jnp.maximum(l_w, 1e-30)
        _gate_mix(ng_ref[0], oc_ref[0], o_s, o_w, g, tq, o_ref)


def _nsa_prompt(q, kvs, kvw, sel, o_c, ng, tq):
    B, T, _ = q.shape
    tile = lambda w: pl.BlockSpec((1, tq, w), lambda b, i: (b, i, 0))
    whole = pl.BlockSpec((1, T, KV_W), lambda b, i: (b, 0, 0))
    return pl.pallas_call(
        _nsa_prompt_body,
        grid=(B, T // tq),
        in_specs=[tile(C_MIX), whole, whole, tile(sel.shape[2]), tile(C_MIX), tile(LANE)],
        out_specs=tile(C_MIX),
        out_shape=jax.ShapeDtypeStruct((B, T, C_MIX), F32),
        compiler_params=_params("parallel", "arbitrary"),
        name="nsa_prompt",
    )(q, kvs, kvw, sel, o_c, ng)


DEC_ROWS = 8


def _nsa_decode_body(mode, pages, pos_q0, pos_k0, pt_ref, q_ref, pool_ref, new_ref, *rest):
    if mode == "sel":
        sel_ref, o_ref, buf, sem, m_scr, l_scr, acc_scr = rest
    else:
        oc_ref, os_ref, ng_ref, o_ref, buf, sem, m_scr, l_scr, acc_scr = rest
    b = pl.program_id(0)
    st = pl.program_id(1)
    tq = DEC_ROWS
    tk = pages * PAGE_SIZE

    def page_copy(i):
        return pltpu.make_async_copy(pool_ref.at[pt_ref[b, st * pages + i]], buf.at[pl.ds(i * PAGE_SIZE, PAGE_SIZE)], sem)

    for i in range(pages):
        page_copy(i).start()

    @pl.when(st == 0)
    def _():
        m_scr[...] = jnp.full(m_scr.shape, NEG_INF, F32)
        l_scr[...] = jnp.zeros(l_scr.shape, F32)
        acc_scr[...] = jnp.zeros(acc_scr.shape, F32)

    q = q_ref[0] * (HEAD_DIM ** -0.5)
    t = pos_q0 + lax.broadcasted_iota(jnp.int32, (tq, 1), 0)

    def make_mask(kpos, g):
        if mode == "sel":
            nbp = sel_ref.shape[2] // KV_HEADS
            blk = lax.broadcasted_iota(jnp.int32, (nbp, 1), 0)
            expand = (kpos // SEL_BLOCK == blk).astype(BF16)
            mask = (_dot(sel_ref[0, :, g * nbp:(g + 1) * nbp].astype(BF16), expand) > 0.5) & (kpos <= t)
        else:
            mask = (kpos <= t) & (kpos > t - WINDOW)
        return jnp.concatenate([mask] * KV_GROUP, axis=0)

    def update(kv, kpos):
        for g in range(KV_HEADS):
            qg = _stack_heads(q, g).astype(BF16)
            carry = (m_scr[g], l_scr[g], acc_scr[g])
            k = kv[:, g * HEAD_DIM:(g + 1) * HEAD_DIM].astype(BF16)
            v = kv[:, LANE + g * HEAD_DIM:LANE + (g + 1) * HEAD_DIM].astype(BF16)
            m_scr[g], l_scr[g], acc_scr[g] = _softmax_step(carry, _dot_nt(qg, k), make_mask(kpos, g), v)

    for i in range(pages):
        page_copy(i).wait()
    update(buf[...], pos_k0 + st * tk + lax.broadcasted_iota(jnp.int32, (1, tk), 1))

    @pl.when(st == pl.num_programs(1) - 1)
    def _():
        update(new_ref[0], pos_q0 + lax.broadcasted_iota(jnp.int32, (1, tq), 1))
        for g in range(KV_HEADS):
            o = acc_scr[g] / jnp.maximum(l_scr[g], 1e-30)
            if mode == "sel":
                for m in range(KV_GROUP):
                    h = g * KV_GROUP + m
                    o_ref[0, :, h * HEAD_DIM:(h + 1) * HEAD_DIM] = o[m * tq:(m + 1) * tq]
            else:
                _gate_mix(ng_ref[0], oc_ref[0], os_ref[0], o, g, tq, o_ref, os_stacked=False)


def _nsa_decode(mode, q, pool, page_table, kv_new, extras, pages, pos_q0, pos_k0):
    B, n_pages = page_table.shape
    rowblk = lambda a: pl.BlockSpec((1,) + a.shape[1:], lambda b, s, pt: (b,) + (0,) * (a.ndim - 1))
    rows = KV_GROUP * DEC_ROWS
    return pl.pallas_call(
        functools.partial(_nsa_decode_body, mode, pages, pos_q0, pos_k0),
        grid_spec=pltpu.PrefetchScalarGridSpec(
            num_scalar_prefetch=1,
            grid=(B, n_pages // pages),
            in_specs=[rowblk(q), pl.BlockSpec(memory_space=pl.ANY), rowblk(kv_new)] + [rowblk(e) for e in extras],
            out_specs=pl.BlockSpec((1, DEC_ROWS, C_MIX), lambda b, s, pt: (b, 0, 0)),
            scratch_shapes=[pltpu.VMEM((pages * PAGE_SIZE, KV_W), F32), pltpu.SemaphoreType.DMA(()),
                            pltpu.VMEM((KV_HEADS, rows, 1), F32), pltpu.VMEM((KV_HEADS, rows, 1), F32),
                            pltpu.VMEM((KV_HEADS, rows, HEAD_DIM), F32)],
        ),
        out_shape=jax.ShapeDtypeStruct((B, DEC_ROWS, C_MIX), F32),
        compiler_params=_params("arbitrary", "arbitrary"),
        name="nsa_decode_" + mode,
    )(page_table, q, pool, kv_new, *extras)


ROUTE_W = LANE


def _merge_body(x_ref, oa_ref, ob_ref, mg_ref, wa_ref, wb_ref, wo_ref, n2_ref, wrh_ref, wrl_ref, br_ref,
                x1_ref, h_ref, route_ref):
    mg = mg_ref[...]
    merged = (mg[:, :D_MODEL] * _dot(oa_ref[...].astype(BF16), wa_ref[...])
              + mg[:, D_MODEL:] * _dot(ob_ref[...].astype(BF16), wb_ref[...]))
    x1 = x_ref[...] + _dot(merged.astype(BF16), wo_ref[...])
    x1_ref[...] = x1
    h = x1 * lax.rsqrt(jnp.mean(x1 * x1, axis=-1, keepdims=True) + RMS_EPS) * n2_ref[...]
    h_ref[...] = h.astype(BF16)
    hh, hl = _split2(h)
    logits = _dot(hh, wrh_ref[...]) + _dot(hl, wrh_ref[...]) + _dot(hh, wrl_ref[...]) + br_ref[...]
    lane = lax.broadcasted_iota(jnp.int32, (1, ROUTE_W), 1)
    first = lambda hit: jnp.min(jnp.where(hit, lane, ROUTE_W), axis=-1, keepdims=True)
    is_g = lane < N_GROUPS
    gl = jnp.where(is_g, logits, NEG_INF)
    gmax = jnp.max(gl, axis=-1, keepdims=True)
    g_sel = first(gl == gmax)
    g_w = 1.0 / jnp.sum(jnp.where(is_g, jnp.exp(gl - gmax), 0.0), axis=-1, keepdims=True)
    in_grp = (lane >= N_GROUPS) & (lane < N_GROUPS + N_EXPERTS) & (((lane - N_GROUPS) >> 3) == g_sel)
    el = jnp.where(in_grp, logits, NEG_INF)
    v1 = jnp.max(el, axis=-1, keepdims=True)
    i1 = first(el == v1)
    el2 = jnp.where(lane == i1, NEG_INF, el)
    v2 = jnp.max(el2, axis=-1, keepdims=True)
    i2 = first(el2 == v2)
    d = jnp.exp(v2 - v1)
    w1 = g_w / (1.0 + d)
    w2 = g_w * d / (1.0 + d)
    route_ref[...] = jnp.where(lane == 0, (i1 - N_GROUPS).astype(F32),
                               jnp.where(lane == 1, (i2 - N_GROUPS).astype(F32),
                                         jnp.where(lane == 2, w1, jnp.where(lane == 3, w2, 0.0))))


def _merge(x2d, oa, ob, mg, wa, wb, wo, norm2, w_rg, b_rg, w_re, b_re, tm):
    n = x2d.shape[0]
    const = lambda shape: pl.BlockSpec(shape, lambda i: (0,) * len(shape))
    row = lambda w: pl.BlockSpec((tm, w), lambda i: (i, 0))
    wr = jnp.zeros((D_MODEL, ROUTE_W), F32).at[:, :N_GROUPS].set(w_rg).at[:, N_GROUPS:N_GROUPS + N_EXPERTS].set(w_re)
    br = jnp.zeros((1, ROUTE_W), F32).at[0, :N_GROUPS].set(b_rg).at[0, N_GROUPS:N_GROUPS + N_EXPERTS].set(b_re)
    wrh, wrl = _split2(wr)
    return pl.pallas_call(
        _merge_body,
        grid=(n // tm,),
        in_specs=[row(D_MODEL), row(C_MIX), row(C_MIX), row(2 * D_MODEL), const((C_MIX, D_MODEL)),
                  const((C_MIX, D_MODEL)), const((D_MODEL, D_MODEL)), const((1, D_MODEL)),
                  const((D_MODEL, ROUTE_W)), const((D_MODEL, ROUTE_W)), const((1, ROUTE_W))],
        out_specs=[row(D_MODEL), row(D_MODEL), row(ROUTE_W)],
        out_shape=[jax.ShapeDtypeStruct((n, D_MODEL), F32), jax.ShapeDtypeStruct((n, D_MODEL), BF16),
                   jax.ShapeDtypeStruct((n, ROUTE_W), F32)],
        compiler_params=_params("parallel"),
        name="merge_route",
    )(x2d, oa, ob, mg, wa.astype(BF16), wb.astype(BF16), wo.astype(BF16), norm2.reshape(1, D_MODEL), wrh, wrl, br)


MOE_TB = 256


def _moe_body(be_ref, nu_ref, x_ref, wg_ref, wu_ref, wd_ref, y_ref):
    @pl.when(pl.program_id(0) < nu_ref[0])
    def _():
        x = x_ref[...]
        gate = _dot(x, wg_ref[0])
        hid = gate * _sigmoid(gate) * _dot(x, wu_ref[0])
        y_ref[...] = _dot(hid.astype(BF16), wd_ref[0])

    @pl.when(pl.program_id(0) >= nu_ref[0])
    def _():
        y_ref[...] = jnp.zeros(y_ref.shape, F32)


def _moe_experts(xbuf, blk_e, n_used, wg, wu, wd):
    n_blk = blk_e.shape[0]
    wspec = lambda shape: pl.BlockSpec((1,) + shape, lambda i, be, nu: (be[i], 0, 0))
    return pl.pallas_call(
        _moe_body,
        grid_spec=pltpu.PrefetchScalarGridSpec(
            num_scalar_prefetch=2,
            grid=(n_blk,),
            in_specs=[pl.BlockSpec((MOE_TB, D_MODEL), lambda i, be, nu: (i, 0)), wspec((D_MODEL, D_EXPERT)),
                      wspec((D_MODEL, D_EXPERT)), wspec((D_EXPERT, D_MODEL))],
            out_specs=pl.BlockSpec((MOE_TB, D_MODEL), lambda i, be, nu: (i, 0)),
        ),
        out_shape=jax.ShapeDtypeStruct((n_blk * MOE_TB, D_MODEL), F32),
        compiler_params=_params("arbitrary"),
        name="moe_experts",
    )(blk_e, n_used, xbuf, wg, wu, wd)


def _moe(x1, h, route, wg, wu, wd):
    n = x1.shape[0]
    expert = route[:, :2].astype(jnp.int32).reshape(-1)
    wts = route[:, 2:4]
    n_slots = 2 * n
    n_blk = -(-n_slots // MOE_TB) + N_EXPERTS
    order = jnp.argsort(expert, stable=True)
    e_s = expert[order]
    counts = jnp.zeros((N_EXPERTS,), jnp.int32).at[expert].add(1)
    padded = (counts + MOE_TB - 1) // MOE_TB * MOE_TB
    p_end = jnp.cumsum(padded)
    dest_s = (p_end - padded)[e_s] + jnp.arange(n_slots, dtype=jnp.int32) - (jnp.cumsum(counts) - counts)[e_s]
    row_tok = jnp.full((n_blk * MOE_TB,), n, jnp.int32).at[dest_s].set(order // 2)
    dest = jnp.zeros((n_slots,), jnp.int32).at[order].set(dest_s).reshape(n, 2)
    blk_e = jnp.minimum(jnp.searchsorted(p_end, jnp.arange(n_blk, dtype=jnp.int32) * MOE_TB, side='right'),
                        N_EXPERTS - 1).astype(jnp.int32)
    n_used = (p_end[-1:] // MOE_TB).astype(jnp.int32)
    xbuf = jnp.concatenate([h, jnp.zeros((1, D_MODEL), h.dtype)])[row_tok]
    ybuf = _moe_experts(xbuf, blk_e, n_used, wg, wu, wd)
    return x1 + wts[:, 0:1] * ybuf[dest[:, 0]] + wts[:, 1:2] * ybuf[dest[:, 1]]


def kernel(x_prompt, x_sample, cache_cmp_kv, cache_slc_kv, cache_win_kv, state_wkv, state_shift, page_table, norm1, w_in, mu_shift, w0, w_decay_up, a0, w_iclr_up, w_gate_up, k_k, k_a, r_k, ln_x_w, ln_x_b, q_norm, kc_norm, ks_norm, kw_norm, cmp_pe_k, cmp_w1_k, cmp_b1_k, cmp_w2_k, cmp_pe_v, cmp_w1_v, cmp_b1_v, cmp_w2_v, w_branch_a, w_branch_b, w_out, norm2, w_route_group, b_route_group, w_route_expert, b_route_expert, w_exp_gate, w_exp_up, w_exp_down):
    assert norm1.shape[0] == 1, "single-layer trunk"
    Bp, Tp, _ = x_prompt.shape
    Bs, Ts, _ = x_sample.shape
    n_pool = cache_cmp_kv.shape[1]
    past = page_table.shape[1] * PAGE_SIZE
    n_buf = cache_win_kv.shape[2]
    kv5 = lambda a, b, t: a.reshape(1, b, t, 2, KV_HEADS, HEAD_DIM)

    w_pad = _pad_w_in(w_in[0])
    rw_p = (mu_shift[0], w0[0], w_decay_up[0], a0[0], w_iclr_up[0], w_gate_up[0], k_k[0], k_a[0],
            r_k[0].reshape(-1), ln_x_w[0], ln_x_b[0])
    wc, pea, peb, b1, w2b = _cmp_weights(cmp_pe_k[0], cmp_w1_k[0], cmp_b1_k[0], cmp_w2_k[0],
                                         cmp_pe_v[0], cmp_w1_v[0], cmp_b1_v[0], cmp_w2_v[0])
    merge_p = (w_branch_a[0], w_branch_b[0], w_out[0], norm2[0], w_route_group[0], b_route_group[0],
               w_route_expert[0], b_route_expert[0])
    moe_w = (w_exp_gate[0].astype(BF16), w_exp_up[0].astype(BF16), w_exp_down[0].astype(BF16))

    xp = x_prompt.reshape(Bp * Tp, D_MODEL)
    zrw, q, kvc, kvs, kvw, ng, mg = _in_proj(xp, norm1[0], w_pad, q_norm[0], ks_norm[0], kw_norm[0], 512)
    zrw3 = zrw.reshape(Bp, Tp, RW_PAD)
    oa, wkv_p = _rwkv(zrw3, jnp.zeros((Bp, 1, RW_PAD), F32), jnp.zeros((Bp, N_HEADS, HEAD_DIM, HEAD_DIM), F32),
                      Tp, *rw_p)
    ab = _cmp_proj(kvc.reshape(Bp * Tp // CMP_STRIDE, CH_W), pea, peb, wc, min(256, Bp * Tp // CMP_STRIDE))
    kc, vc = _cmp_finish(ab.reshape(Bp, Tp // CMP_STRIDE, -1), b1, w2b, kc_norm[0])
    q3 = q.reshape(Bp, Tp, C_MIX)
    o_c, sel = _cmp_attn(q3, kc, vc, 0, (Tp - CMP_LEN) // CMP_STRIDE + 1, Tp // SEL_BLOCK, 128)
    ob = _nsa_prompt(q3, kvs.reshape(Bp, Tp, KV_W), kvw.reshape(Bp, Tp, KV_W), sel, o_c, ng.reshape(Bp, Tp, LANE), 128)
    x1, h, route = _merge(xp, oa.reshape(Bp * Tp, C_MIX), ob.reshape(Bp * Tp, C_MIX), mg, *merge_p, 512)
    y_p = _moe(x1, h, route, *moe_w).reshape(Bp, Tp, D_MODEL)
    keep_p = min(WINDOW, Tp)

    xs = x_sample.reshape(Bs * Ts, D_MODEL)
    zrw_s, q_s, kvc_s, kvs_s, kvw_s, ng_s, mg_s = _in_proj(xs, norm1[0], w_pad, q_norm[0], ks_norm[0], kw_norm[0],
                                                           Bs * Ts)
    zrw_s3 = zrw_s.reshape(Bs, Ts, RW_PAD)
    oa_s, wkv_s = _rwkv(jnp.pad(zrw_s3, ((0, 0), (0, RW_CHUNK - Ts), (0, 0))),
                        jnp.pad(state_shift[0], ((0, 0), (0, RW_PAD - RW_IN)))[:, None], state_wkv[0], Ts, *rw_p)
    nc_s = (past + Ts - CMP_LEN) // CMP_STRIDE + 1
    assert (nc_s + CMP_LEN // CMP_STRIDE - 1) * CMP_STRIDE <= past, "compression blocks only cover cached rows"
    ab_s = _cmp_proj_paged(cache_cmp_kv[0].reshape(n_pool, PAGE_SIZE // CMP_STRIDE, CH_W), page_table, pea, peb, wc)
    kc_s, vc_s = _cmp_finish(ab_s, b1, w2b, kc_norm[0])
    rows8 = lambda a: jnp.pad(a.reshape(Bs, Ts, -1), ((0, 0), (0, DEC_ROWS - Ts), (0, 0)))
    q8 = rows8(q_s)
    oc_s, sel_s = _cmp_attn(q8, kc_s, vc_s, past, nc_s, -(-(past + Ts) // SEL_BLOCK), DEC_ROWS)
    os_s = _nsa_decode("sel", q8, cache_slc_kv[0].reshape(n_pool, PAGE_SIZE, KV_W), page_table, rows8(kvs_s),
                       (sel_s,), CMP_PAGES, past, 0)
    win_pages = n_buf // PAGE_SIZE
    ob_s = _nsa_decode("win", q8, cache_win_kv[0].reshape(Bs * win_pages, PAGE_SIZE, KV_W),
                       jnp.arange(Bs * win_pages, dtype=jnp.int32).reshape(Bs, win_pages), rows8(kvw_s),
                       (oc_s, os_s, rows8(ng_s)), win_pages, past, past - n_buf)
    x1_s, h_s, route_s = _merge(xs, oa_s[:, :Ts].reshape(Bs * Ts, C_MIX), ob_s[:, :Ts].reshape(Bs * Ts, C_MIX), mg_s,
                                *merge_p, Bs * Ts)
    y_s = _moe(x1_s, h_s, route_s, *moe_w).reshape(Bs, Ts, D_MODEL)
    keep_s = min(WINDOW, n_buf + Ts)
    win_s = jnp.concatenate([cache_win_kv[0].reshape(Bs, n_buf, KV_W), kvw_s.reshape(Bs, Ts, KV_W)], axis=1)

    return (y_p, y_s,
            kv5(kvc, Bp, Tp), kv5(kvs, Bp, Tp), kv5(kvw.reshape(Bp, Tp, KV_W)[:, Tp - keep_p:], Bp, keep_p),
            wkv_p[None], zrw3[:, -1, :RW_IN][None],
            kv5(kvc_s, Bs, Ts), kv5(kvs_s, Bs, Ts), kv5(win_s[:, n_buf + Ts - keep_s:], Bs, keep_s),
            wkv_s[None], zrw_s3[:, -1, :RW_IN][None])
```

```python
import functools
import math

import jax
import jax.numpy as jnp
from jax import lax
from jax.experimental import pallas as pl
from jax.experimental.pallas import tpu as pltpu

F32 = jnp.float32
BF16 = jnp.bfloat16

D_MODEL = 1024
HEAD_DIM = 64
N_HEADS = 8
C_MIX = N_HEADS * HEAD_DIM
R_DECAY, R_ICLR, R_GATE = 32, 32, 96
RW_IN = 3 * C_MIX + R_DECAY + R_ICLR + R_GATE
KV_HEADS = 2
KV_GROUP = N_HEADS // KV_HEADS
KV_W = 2 * KV_HEADS * HEAD_DIM
CMP_LEN, CMP_STRIDE = 32, 16
SEL_BLOCK = 64
N_SEL = 16
WINDOW = 512
PAGE_SIZE = 128
N_GROUPS, EXPERTS_PER_GROUP = 4, 8
N_EXPERTS = N_GROUPS * EXPERTS_PER_GROUP
D_EXPERT = D_MODEL // 2
RMS_EPS = 1e-6
GN_EPS = 64e-5
NEG_INF = -1e30
FORCE_SCORE = 1e6

LANE = 128
VMEM_LIMIT = 56 * 1024 * 1024

RW_PAD = 1792
OFF_Q = RW_PAD
OFF_KVC = OFF_Q + C_MIX
OFF_KVS = OFF_KVC + KV_W
OFF_KVW = OFF_KVS + KV_W
OFF_NG = OFF_KVW + KV_W
OFF_MG = OFF_NG + LANE
N_IN_PAD = OFF_MG + 2 * D_MODEL
RW_TAIL = 3 * C_MIX


def _params(*sem):
    return pltpu.CompilerParams(dimension_semantics=sem, vmem_limit_bytes=VMEM_LIMIT)


def _dot(a, b):
    return jnp.dot(a, b, preferred_element_type=F32)


def _dot_nt(a, b):
    return lax.dot_general(a, b, (((1,), (1,)), ((), ())), preferred_element_type=F32)


def _dot_tn(a, b):
    return lax.dot_general(a, b, (((0,), (0,)), ((), ())), preferred_element_type=F32)


def _split2(x):
    hi = x.astype(BF16)
    lo = (x - hi.astype(F32)).astype(BF16)
    return hi, lo


def _split3(x):
    hi = x.astype(BF16)
    r1 = x - hi.astype(F32)
    mid = r1.astype(BF16)
    lo = (r1 - mid.astype(F32)).astype(BF16)
    return hi, mid, lo


def _gsum(y, g):
    hi, lo = _split2(y)
    return _dot(hi, g) + _dot(lo, g)


def _block_ones(n, blk):
    i = jnp.arange(n) // blk
    return (i[:, None] == i[None, :]).astype(BF16)


def _sigmoid(x):
    return 1.0 / (1.0 + jnp.exp(-x))


def _inproj_body(x_ref, n1_ref, w_ref, g512_ref, g128_ref, qn_ref, ksn_ref, kwn_ref,
                 zrw_ref, q_ref, kvc_ref, kvs_ref, kvw_ref, ng_ref, mg_ref, kvsg_ref, kvwg_ref):
    x = x_ref[...]
    ms = jnp.mean(x * x, axis=-1, keepdims=True)
    xn = (x * lax.rsqrt(ms + RMS_EPS) * n1_ref[...]).astype(BF16)

    def proj(a, b):
        return _dot(xn, w_ref[:, a:b])

    zrw_ref[...] = proj(0, RW_PAD)
    q = proj(OFF_Q, OFF_KVC)
    q_ref[...] = q * lax.rsqrt(_gsum(q * q, g512_ref[...]) * (1.0 / HEAD_DIM) + RMS_EPS) * qn_ref[...]
    kvc_ref[...] = proj(OFF_KVC, OFF_KVS)
    for off, nref, oref, gref in ((OFF_KVS, ksn_ref, kvs_ref, kvsg_ref), (OFF_KVW, kwn_ref, kvw_ref, kvwg_ref)):
        kv = proj(off, off + KV_W)
        k = kv[:, :LANE]
        kn = k * lax.rsqrt(_gsum(k * k, g128_ref[...]) * (1.0 / HEAD_DIM) + RMS_EPS) * nref[...]
        v = kv[:, LANE:]
        oref[:, :LANE] = kn
        oref[:, LANE:] = v
        for g in range(KV_HEADS):
            hs = slice(g * HEAD_DIM, (g + 1) * HEAD_DIM)
            gref[g] = jnp.concatenate([kn[:, hs], v[:, hs]], axis=1).astype(BF16)
    ng_ref[...] = _sigmoid(proj(OFF_NG, OFF_MG))
    mg_ref[...] = _sigmoid(proj(OFF_MG, N_IN_PAD))


def _in_proj(x2d, norm1, w_pad, q_norm, ks_norm, kw_norm, tm):
    n = x2d.shape[0]
    const = lambda shape: pl.BlockSpec(shape, lambda i: (0,) * len(shape))
    row = lambda w: pl.BlockSpec((tm, w), lambda i: (i, 0))
    widths = (RW_PAD, C_MIX, KV_W, KV_W, KV_W, LANE, 2 * D_MODEL)
    return pl.pallas_call(
        _inproj_body,
        grid=(n // tm,),
        in_specs=[row(D_MODEL), const((1, D_MODEL)), const((D_MODEL, N_IN_PAD)), const((C_MIX, C_MIX)),
                  const((LANE, LANE)), const((1, C_MIX)), const((1, LANE)), const((1, LANE))],
        out_specs=[row(w) for w in widths] + [pl.BlockSpec((KV_HEADS, tm, LANE), lambda i: (0, i, 0))] * 2,
        out_shape=[jax.ShapeDtypeStruct((n, w), F32) for w in widths]
                  + [jax.ShapeDtypeStruct((KV_HEADS, n, LANE), BF16)] * 2,
        compiler_params=_params("parallel"),
        name="in_proj",
    )(x2d, norm1.reshape(1, D_MODEL), w_pad, _block_ones(C_MIX, HEAD_DIM), _block_ones(LANE, HEAD_DIM),
      jnp.tile(q_norm, N_HEADS).reshape(1, C_MIX), jnp.tile(ks_norm, KV_HEADS).reshape(1, LANE),
      jnp.tile(kw_norm, KV_HEADS).reshape(1, LANE))


def _pad_w_in(w_in):
    d = w_in.shape[0]
    z = lambda n: jnp.zeros((d, n), w_in.dtype)
    o_q = RW_IN
    o_ng = o_q + C_MIX + 3 * KV_W
    o_mg = o_ng + 3 * N_HEADS
    return jnp.concatenate([w_in[:, :RW_IN], z(RW_PAD - RW_IN), w_in[:, o_q:o_ng], w_in[:, o_ng:o_mg],
                            z(LANE - 3 * N_HEADS), w_in[:, o_mg:]], axis=1).astype(BF16)


RW_CHUNK = 64
RW_HSTACK = 4


def _rwkv_body(t_valid, z_ref, sp_ref, s0_ref, mu_ref, w0_ref, a0_ref, kk_ref, ka_ref, rk_ref, lnw_ref, lnb_ref,
               wd_ref, wi_ref, wg_ref, g512_ref, o_ref, s_ref, prev_scr):
    C = RW_CHUNK
    c = pl.program_id(1)

    @pl.when(c == 0)
    def _():
        s_ref[...] = s0_ref[...]
        prev_scr[...] = sp_ref[0]

    z = z_ref[0]
    row = lax.broadcasted_iota(jnp.int32, (C, 1), 0)
    z_prev = jnp.where(row == 0, prev_scr[...], pltpu.roll(z, 1, axis=0))
    prev_scr[...] = z[C - 1:C]
    zm = z + (z_prev - z) * mu_ref[...]
    r = zm[:, 0:C_MIX]
    k = zm[:, C_MIX:2 * C_MIX]
    v = zm[:, 2 * C_MIX:3 * C_MIX]
    tail = zm[:, RW_TAIL:RW_PAD]
    w_lora = _dot(jnp.tanh(tail).astype(BF16), wd_ref[...])
    a_lora = _dot(tail.astype(BF16), wi_ref[...])
    g = _dot(_sigmoid(tail).astype(BF16), wg_ref[...])
    u = -(w0_ref[...] + w_lora)
    softplus = jnp.maximum(u, 0.0) + jnp.log(1.0 + jnp.exp(-jnp.abs(u)))
    w_log = -softplus - 0.5
    valid = (c * C + row) < t_valid
    ld = jnp.where(valid, -jnp.exp(w_log), 0.0)
    a = _sigmoid(a0_ref[...] + a_lora)
    kk = k * kk_ref[...]
    kk = kk / jnp.maximum(jnp.sqrt(_gsum(kk * kk, g512_ref[...])), 1e-12)
    k2 = k * (1.0 + (a - 1.0) * ka_ref[...])

    ci = lax.broadcasted_iota(jnp.int32, (C, C), 0)
    cj = lax.broadcasted_iota(jnp.int32, (C, C), 1)
    tri = (ci >= cj).astype(BF16)
    h1, h2, h3 = _split3(ld)
    cl = _dot(tri, h1) + _dot(tri, h2) + _dot(tri, h3)
    p_in = jnp.exp(cl)
    p_inv = jnp.exp(-cl)
    r_t = r * p_in
    a_t = -kk * jnp.exp(cl - ld)
    b_t = jnp.where(valid, kk * a * p_inv, 0.0)
    k_t = jnp.where(valid, k2 * p_inv, 0.0)
    p_end = p_in[C - 1:C]

    HS = RW_HSTACK
    R = HS * C
    ri = lax.broadcasted_iota(jnp.int32, (R, R), 0)
    rj = lax.broadcasted_iota(jnp.int32, (R, R), 1)
    same = (ri // C) == (rj // C)
    lower = same & (ri > rj)
    lower_eq = same & (ri >= rj)
    eye = (ri == rj).astype(F32)
    n_lvl = int(math.log2(C))
    y_heads = []
    for hg in range(N_HEADS // HS):
        hs = [hg * HS + m for m in range(HS)]
        stack = lambda x: jnp.concatenate([x[:, h * HEAD_DIM:(h + 1) * HEAD_DIM] for h in hs], axis=0).astype(BF16)
        A, Rr, Bm, Km, V = stack(a_t), stack(r_t), stack(b_t), stack(k_t), stack(v)
        AR = jnp.concatenate([A, Rr], axis=0)
        S4 = _dot_nt(AR, jnp.concatenate([Bm, Km], axis=0))
        L = jnp.where(lower, S4[:R, :R], 0.0)
        Lak = jnp.where(lower, S4[:R, R:], 0.0).astype(BF16)
        Mr = jnp.concatenate([jnp.where(lower_eq, S4[R:, :R], 0.0), jnp.where(lower_eq, S4[R:, R:], 0.0)],
                             axis=1).astype(BF16)
        s0 = [s_ref[0, h] for h in hs]
        on_state = [_dot_nt(jnp.concatenate([A[m * C:(m + 1) * C], Rr[m * C:(m + 1) * C]], axis=0),
                            s0[m].astype(BF16)) for m in range(HS)]
        rhs = jnp.concatenate([o[:C] for o in on_state], axis=0) + _dot(Lak, V)
        X = eye + L
        Lp = L
        for _ in range(n_lvl - 1):
            Lpb = Lp.astype(BF16)
            Lp = _dot(Lpb, Lpb)
            X = X + _dot(X.astype(BF16), Lp.astype(BF16))
        U = _dot(X.astype(BF16), rhs.astype(BF16)).astype(BF16)
        Y = jnp.concatenate([o[C:] for o in on_state], axis=0) + _dot(Mr, jnp.concatenate([U, V], axis=0))
        for m, h in enumerate(hs):
            rs = slice(m * C, (m + 1) * C)
            upd = _dot_tn(jnp.concatenate([U[rs], V[rs]], axis=0), jnp.concatenate([Bm[rs], Km[rs]], axis=0))
            s_ref[0, h] = (s0[m] + upd) * p_end[:, h * HEAD_DIM:(h + 1) * HEAD_DIM]
            y_heads.append(Y[rs])
    y = jnp.concatenate(y_heads, axis=1)
    g512 = g512_ref[...]
    yc = y - _gsum(y, g512) * (1.0 / HEAD_DIM)
    var = _gsum(yc * yc, g512) * (1.0 / HEAD_DIM)
    yn = yc * lax.rsqrt(var + GN_EPS) * lnw_ref[...] + lnb_ref[...]
    bonus = _gsum(r * k2 * rk_ref[...], g512) * v
    o_ref[0] = (yn + bonus) * g


def _rwkv(z_rw, shift_prev, s0, t_valid, mu, w0, wd, a0, wi, wg, k_k, k_a, r_k, ln_w, ln_b):
    B, T, _ = z_rw.shape
    C = RW_CHUNK
    const = lambda shape: pl.BlockSpec(shape, lambda b, c: (0,) * len(shape))
    vec = lambda p: p.reshape(1, C_MIX)
    pad_rows = lambda w, off: jnp.zeros((RW_PAD - RW_TAIL, C_MIX), F32).at[off:off + w.shape[0]].set(w).astype(BF16)
    state_spec = pl.BlockSpec((1, N_HEADS, HEAD_DIM, HEAD_DIM), lambda b, c: (b, 0, 0, 0))
    return pl.pallas_call(
        functools.partial(_rwkv_body, t_valid),
        grid=(B, T // C),
        in_specs=[pl.BlockSpec((1, C, RW_PAD), lambda b, c: (b, c, 0)),
                  pl.BlockSpec((1, 1, RW_PAD), lambda b, c: (b, 0, 0)),
                  state_spec, const((1, RW_PAD))] + [const((1, C_MIX))] * 7
                 + [const((RW_PAD - RW_TAIL, C_MIX))] * 3 + [const((C_MIX, C_MIX))],
        out_specs=[pl.BlockSpec((1, C, C_MIX), lambda b, c: (b, c, 0)), state_spec],
        out_shape=[jax.ShapeDtypeStruct((B, T, C_MIX), F32),
                   jax.ShapeDtypeStruct((B, N_HEADS, HEAD_DIM, HEAD_DIM), F32)],
        scratch_shapes=[pltpu.VMEM((1, RW_PAD), F32)],
        compiler_params=_params("parallel", "arbitrary"),
        name="rwkv7",
    )(z_rw, shift_prev, s0, jnp.pad(mu, (0, RW_PAD - RW_IN)).reshape(1, RW_PAD), vec(w0), vec(a0), vec(k_k),
      vec(k_a), vec(r_k), vec(ln_w), vec(ln_b), pad_rows(wd, 0), pad_rows(wi, R_DECAY),
      pad_rows(wg, R_DECAY + R_ICLR), _block_ones(C_MIX, HEAD_DIM))


CH_W = CMP_STRIDE * KV_W
N_SLOT = 2 * KV_HEADS


def _cmp_weights(pe_k, w1_k, b1_k, w2_k, pe_v, w1_v, b1_v, w2_v):
    eye = jnp.eye(N_SLOT, dtype=F32)
    w1 = jnp.stack([w1_k, w1_k, w1_v, w1_v])
    pe = jnp.stack([pe_k, pe_k, pe_v, pe_v])
    halves = []
    pes = []
    for r in range(CMP_LEN // CMP_STRIDE):
        ls = slice(r * CMP_STRIDE, (r + 1) * CMP_STRIDE)
        halves.append(jnp.einsum('sldf,st->lsdtf', w1[:, ls], eye).reshape(CH_W, N_SLOT * HEAD_DIM))
        pes.append(jnp.transpose(pe[:, ls], (1, 0, 2)).reshape(1, CH_W))
    wc = jnp.concatenate(halves, axis=1).astype(BF16)
    w2 = jnp.stack([w2_k, w2_k, w2_v, w2_v])
    w2b = jnp.einsum('sfd,st->sftd', w2, eye).reshape(N_SLOT * HEAD_DIM, N_SLOT * HEAD_DIM).astype(BF16)
    b1 = jnp.concatenate([b1_k, b1_k, b1_v, b1_v]).reshape(1, N_SLOT * HEAD_DIM)
    return wc, pes[0], pes[1], b1, w2b


def _cmp_proj_body(ch_ref, pea_ref, peb_ref, wc_ref, ab_ref):
    ch = ch_ref[...]
    w = N_SLOT * HEAD_DIM
    ab_ref[:, :w] = _dot((ch + pea_ref[...]).astype(BF16), wc_ref[:, :w])
    ab_ref[:, w:] = _dot((ch + peb_ref[...]).astype(BF16), wc_ref[:, w:])


def _cmp_proj(chunks, pea, peb, wc, tr):
    n = chunks.shape[0]
    const = lambda shape: pl.BlockSpec(shape, lambda i: (0,) * len(shape))
    return pl.pallas_call(
        _cmp_proj_body,
        grid=(n // tr,),
        in_specs=[pl.BlockSpec((tr, CH_W), lambda i: (i, 0)), const((1, CH_W)), const((1, CH_W)),
                  const((CH_W, 2 * N_SLOT * HEAD_DIM))],
        out_specs=pl.BlockSpec((tr, 2 * N_SLOT * HEAD_DIM), lambda i: (i, 0)),
        out_shape=jax.ShapeDtypeStruct((n, 2 * N_SLOT * HEAD_DIM), F32),
        compiler_params=_params("parallel"),
        name="cmp_proj",
    )(chunks, pea, peb, wc)


CMP_PAGES = 16


def _paged_fetch(pt_ref, pool_ref, buf, sem, pages, page_rows):
    b = pl.program_id(0)
    st = pl.program_id(1)
    n_st = pl.num_programs(1)
    step = b * n_st + st
    slot = step % 2

    def copies(bb, stt, sl):
        return [pltpu.make_async_copy(pool_ref.at[pt_ref[bb, stt * pages + i]],
                                      buf.at[sl, pl.ds(i * page_rows, page_rows)], sem.at[sl]) for i in range(pages)]

    @pl.when(step == 0)
    def _():
        for c in copies(0, 0, 0):
            c.start()

    @pl.when(step + 1 < pl.num_programs(0) * n_st)
    def _():
        wrap = st + 1 == n_st
        for c in copies(jnp.where(wrap, b + 1, b), jnp.where(wrap, 0, st + 1), 1 - slot):
            c.start()

    for c in copies(b, st, slot):
        c.wait()
    return slot


def _cmp_proj_paged_body(pt_ref, cache_ref, pea_ref, peb_ref, wc_ref, ab_ref, buf, sem):
    slot = _paged_fetch(pt_ref, cache_ref, buf, sem, CMP_PAGES, PAGE_SIZE // CMP_STRIDE)
    _cmp_proj_body(buf.at[slot], pea_ref, peb_ref, wc_ref, ab_ref.at[0])


def _cmp_proj_paged(cache, page_table, pea, peb, wc):
    B, n_pages = page_table.shape
    rows = PAGE_SIZE // CMP_STRIDE
    const = lambda shape: pl.BlockSpec(shape, lambda b, g, pt: (0,) * len(shape))
    return pl.pallas_call(
        _cmp_proj_paged_body,
        grid_spec=pltpu.PrefetchScalarGridSpec(
            num_scalar_prefetch=1,
            grid=(B, n_pages // CMP_PAGES),
            in_specs=[pl.BlockSpec(memory_space=pl.ANY), const((1, CH_W)), const((1, CH_W)),
                      const((CH_W, 2 * N_SLOT * HEAD_DIM))],
            out_specs=pl.BlockSpec((1, CMP_PAGES * rows, 2 * N_SLOT * HEAD_DIM), lambda b, g, pt: (b, g, 0)),
            scratch_shapes=[pltpu.VMEM((2, CMP_PAGES * rows, CH_W), F32), pltpu.SemaphoreType.DMA((2,))],
        ),
        out_shape=jax.ShapeDtypeStruct((B, n_pages * rows, 2 * N_SLOT * HEAD_DIM), F32),
        compiler_params=_params("arbitrary", "arbitrary"),
        name="cmp_proj_paged",
    )(page_table, cache, pea, peb, wc)


def _cmp_finish_body(ab_ref, b1_ref, w2_ref, kcn_ref, g128_ref, kc_ref, vc_ref):
    ab = ab_ref[0]
    n = ab.shape[0]
    w = N_SLOT * HEAD_DIM
    pre = ab[:, :w] + pltpu.roll(ab[:, w:], n - 1, axis=0) + b1_ref[...]
    hid = pre * _sigmoid(pre)
    out = _dot(hid.astype(BF16), w2_ref[...])
    k = out[:, :LANE]
    kc_ref[0] = k * lax.rsqrt(_gsum(k * k, g128_ref[...]) * (1.0 / HEAD_DIM) + RMS_EPS) * kcn_ref[...]
    vc_ref[0] = out[:, LANE:]


def _cmp_finish(ab, b1, w2b, kc_norm):
    B, n, _ = ab.shape
    const = lambda shape: pl.BlockSpec(shape, lambda b: (0,) * len(shape))
    w = N_SLOT * HEAD_DIM
    return pl.pallas_call(
        _cmp_finish_body,
        grid=(B,),
        in_specs=[pl.BlockSpec((1, n, 2 * w), lambda b: (b, 0, 0)), const((1, w)), const((w, w)), const((1, LANE)),
                  const((LANE, LANE))],
        out_specs=[pl.BlockSpec((1, n, LANE), lambda b: (b, 0, 0))] * 2,
        out_shape=[jax.ShapeDtypeStruct((B, n, LANE), F32)] * 2,
        compiler_params=_params("parallel"),
        name="cmp_finish",
    )(ab, b1, w2b, jnp.tile(kc_norm, KV_HEADS).reshape(1, LANE), _block_ones(LANE, HEAD_DIM))


def _cmp_attn_body(pos0, nc, nb, nbp, blocks_on_rows, q_ref, kc_ref, vc_ref, ovl_ref, o_ref, sel_ref):
    tq = q_ref.shape[1]
    ncp = kc_ref.shape[1]
    q = q_ref[0] * (HEAD_DIM ** -0.5)
    t0 = pos0 + pl.program_id(1) * tq
    t = t0 + lax.broadcasted_iota(jnp.int32, (tq, 1), 0)
    cidx = lax.broadcasted_iota(jnp.int32, (1, ncp), 1)
    cmask = (cidx * CMP_STRIDE + (CMP_LEN - 1) <= t) & (cidx < nc)
    if blocks_on_rows:
        tt = t0 + lax.broadcasted_iota(jnp.int32, (1, tq), 1)
        j = lax.broadcasted_iota(jnp.int32, (nbp, 1), 0)
    else:
        tt = t
        j = lax.broadcasted_iota(jnp.int32, (1, nbp), 1)
    cur = tt // SEL_BLOCK
    valid = (j * SEL_BLOCK <= tt) & (j < nb)
    forced = (j == 0) | (j == cur) | (j == cur - 1)
    for g in range(KV_HEADS):
        kcg = kc_ref[0, :, g * HEAD_DIM:(g + 1) * HEAD_DIM].astype(BF16)
        vcg = vc_ref[0, :, g * HEAD_DIM:(g + 1) * HEAD_DIM].astype(BF16)
        psum = jnp.zeros((tq, ncp), F32)
        for m in range(KV_GROUP):
            sl = slice((g * KV_GROUP + m) * HEAD_DIM, (g * KV_GROUP + m + 1) * HEAD_DIM)
            s = jnp.where(cmask, _dot_nt(q[:, sl].astype(BF16), kcg), NEG_INF)
            e = jnp.where(cmask, jnp.exp(s - jnp.max(s, axis=-1, keepdims=True)), 0.0)
            p = e / jnp.maximum(jnp.sum(e, axis=-1, keepdims=True), 1e-30)
            o_ref[0, :, sl] = _dot(p.astype(BF16), vcg)
            psum = psum + p
        hi, lo = _split2(psum)
        if blocks_on_rows:
            imp = _dot_nt(ovl_ref[...], hi) + _dot_nt(ovl_ref[...], lo)
        else:
            imp = _dot(hi, ovl_ref[...]) + _dot(lo, ovl_ref[...])
        score = jnp.where(valid, jnp.where(forced, FORCE_SCORE, imp), NEG_INF)
        cnt = jnp.zeros(score.shape, jnp.int32)
        for jp in range(nb):
            cj = score[jp:jp + 1, :] if blocks_on_rows else score[:, jp:jp + 1]
            cnt = cnt + jnp.where(j > jp, jnp.where(cj >= score, 1, 0), jnp.where(cj > score, 1, 0))
        picked = (cnt < N_SEL).astype(F32)
        if blocks_on_rows:
            sel_ref[0, g] = picked
        else:
            sel_ref[0, :, g * nbp:(g + 1) * nbp] = picked


def _cmp_attn(q, kc, vc, pos0, nc, nb, tq, blocks_on_rows):
    B, Tq, _ = q.shape
    ncp = kc.shape[1]
    nbp = -(-nb // SEL_BLOCK) * SEL_BLOCK
    c0 = jnp.arange(ncp)[:, None] * CMP_STRIDE
    jj = jnp.arange(nbp)[None, :]
    ovl = ((c0 < (jj + 1) * SEL_BLOCK) & (c0 + CMP_LEN > jj * SEL_BLOCK) & (jnp.arange(ncp)[:, None] < nc)
           & (jj < nb)).astype(BF16)
    if blocks_on_rows:
        ovl = ovl.T
        sel_spec = pl.BlockSpec((1, KV_HEADS, nbp, tq), lambda b, i: (b, 0, 0, i))
        sel_shape = (B, KV_HEADS, nbp, Tq)
    else:
        sel_spec = pl.BlockSpec((1, tq, KV_HEADS * nbp), lambda b, i: (b, i, 0))
        sel_shape = (B, Tq, KV_HEADS * nbp)
    return pl.pallas_call(
        functools.partial(_cmp_attn_body, pos0, nc, nb, nbp, blocks_on_rows),
        grid=(B, Tq // tq),
        in_specs=[pl.BlockSpec((1, tq, C_MIX), lambda b, i: (b, i, 0)),
                  pl.BlockSpec((1, ncp, LANE), lambda b, i: (b, 0, 0)),
                  pl.BlockSpec((1, ncp, LANE), lambda b, i: (b, 0, 0)),
                  pl.BlockSpec(ovl.shape, lambda b, i: (0, 0))],
        out_specs=[pl.BlockSpec((1, tq, C_MIX), lambda b, i: (b, i, 0)), sel_spec],
        out_shape=[jax.ShapeDtypeStruct((B, Tq, C_MIX), F32), jax.ShapeDtypeStruct(sel_shape, F32)],
        compiler_params=_params("parallel", "parallel"),
        name="cmp_attn",
    )(q, kc, vc, ovl)


def _softmax_step(carry, s, mask, vb):
    m_, l_, acc = carry
    s = jnp.where(mask, s, NEG_INF)
    m_new = jnp.maximum(m_, jnp.max(s, axis=-1, keepdims=True))
    alpha = jnp.exp(m_ - m_new)
    p = jnp.where(mask, jnp.exp(s - m_new), 0.0)
    return m_new, alpha * l_ + jnp.sum(p, axis=-1, keepdims=True), alpha * acc + _dot(p.astype(BF16), vb)


def _softmax_init(rows):
    return jnp.full((rows, 1), NEG_INF, F32), jnp.zeros((rows, 1), F32), jnp.zeros((rows, HEAD_DIM), F32)


def _stack_heads(x, g):
    return jnp.concatenate([x[:, (g * KV_GROUP + m) * HEAD_DIM:(g * KV_GROUP + m + 1) * HEAD_DIM]
                            for m in range(KV_GROUP)], axis=0)


def _gate_mix(ng, o_c, o_s, o_w, g, tq, o_ref, os_stacked=True):
    for m in range(KV_GROUP):
        h = g * KV_GROUP + m
        sl = slice(h * HEAD_DIM, (h + 1) * HEAD_DIM)
        rs = slice(m * tq, (m + 1) * tq)
        o_ref[0, :, sl] = (ng[:, h:h + 1] * o_c[:, sl]
                           + ng[:, N_HEADS + h:N_HEADS + h + 1] * (o_s[rs] if os_stacked else o_s[:, sl])
                           + ng[:, 2 * N_HEADS + h:2 * N_HEADS + h + 1] * o_w[rs])


SEL_TK = 512
WIN_TK = 256


def _flash_step(carry, qg, kv, bias, tq):
    m_, l_, acc = carry
    tk = kv.shape[0]
    s = _dot_nt(qg, kv).reshape(KV_GROUP, tq, tk) + bias[None]
    m_new = jnp.maximum(m_, jnp.max(s, axis=-1, keepdims=True))
    alpha = jnp.exp(m_ - m_new)
    p = jnp.exp(s - m_new)
    l_new = alpha * l_ + jnp.sum(p, axis=-1, keepdims=True)
    pv = _dot(p.reshape(KV_GROUP * tq, tk).astype(BF16), kv).reshape(KV_GROUP, tq, LANE)
    return m_new, l_new, alpha * acc + pv


def _flash_init(tq):
    return (jnp.full((KV_GROUP, tq, 1), NEG_INF, F32), jnp.zeros((KV_GROUP, tq, 1), F32),
            jnp.zeros((KV_GROUP, tq, LANE), F32))


def _nsa_prompt_body(q_ref, kvs_ref, kvw_ref, sel_ref, oc_ref, ng_ref, o_ref):
    tq = q_ref.shape[1]
    nbp = sel_ref.shape[2]
    q0 = pl.program_id(1) * tq
    q = q_ref[0] * (HEAD_DIM ** -0.5)
    t = q0 + lax.broadcasted_iota(jnp.int32, (tq, 1), 0)
    blk = lax.broadcasted_iota(jnp.int32, (nbp, 1), 0)
    zpad = jnp.zeros((tq, LANE - HEAD_DIM), F32)
    qgs = []
    sels = []
    for g in range(KV_HEADS):
        heads = [jnp.concatenate([q[:, (g * KV_GROUP + m) * HEAD_DIM:(g * KV_GROUP + m + 1) * HEAD_DIM], zpad], axis=1)
                 for m in range(KV_GROUP)]
        qgs.append(jnp.concatenate(heads, axis=0).astype(BF16))
        sels.append(sel_ref[0, g].astype(BF16))

    def sel_step(kt, carry):
        k0 = pl.multiple_of(kt * SEL_TK, SEL_TK)
        kpos = k0 + lax.broadcasted_iota(jnp.int32, (1, SEL_TK), 1)
        expand = (kpos // SEL_BLOCK == blk).astype(BF16)
        causal = kpos <= t
        out = []
        for g in range(KV_HEADS):
            bias = jnp.where((_dot_tn(sels[g], expand) > 0.5) & causal, 0.0, NEG_INF)
            out.append(_flash_step(carry[g], qgs[g], kvs_ref[g, 0, pl.ds(k0, SEL_TK), :], bias, tq))
        return tuple(out)

    res_s = lax.fori_loop(0, (q0 + tq + SEL_TK - 1) // SEL_TK, sel_step, (_flash_init(tq),) * KV_HEADS)

    def win_step(kt, carry):
        k0 = pl.multiple_of(kt * WIN_TK, WIN_TK)
        kpos = k0 + lax.broadcasted_iota(jnp.int32, (1, WIN_TK), 1)
        bias = jnp.where((kpos <= t) & (kpos > t - WINDOW), 0.0, NEG_INF)
        return tuple(_flash_step(carry[g], qgs[g], kvw_ref[g, 0, pl.ds(k0, WIN_TK), :], bias, tq)
                     for g in range(KV_HEADS))

    first = jnp.maximum(q0 - WINDOW + 1, 0) // WIN_TK
    res_w = lax.fori_loop(first, (q0 + tq + WIN_TK - 1) // WIN_TK, win_step, (_flash_init(tq),) * KV_HEADS)

    ng = ng_ref[0]
    oc = oc_ref[0]
    for g in range(KV_HEADS):
        o_s = res_s[g][2][:, :, HEAD_DIM:] / jnp.maximum(res_s[g][1], 1e-30)
        o_w = res_w[g][2][:, :, HEAD_DIM:] / jnp.maximum(res_w[g][1], 1e-30)
        for m in range(KV_GROUP):
            h = g * KV_GROUP + m
            sl = slice(h * HEAD_DIM, (h + 1) * HEAD_DIM)
            o_ref[0, :, sl] = (ng[:, h:h + 1] * oc[:, sl] + ng[:, N_HEADS + h:N_HEADS + h + 1] * o_s[m]
                               + ng[:, 2 * N_HEADS + h:2 * N_HEADS + h + 1] * o_w[m])


def _nsa_prompt(q, kvs_g, kvw_g, sel, o_c, ng, tq):
    B, T, _ = q.shape
    tile = lambda w: pl.BlockSpec((1, tq, w), lambda b, i: (b, i, 0))
    whole = pl.BlockSpec((KV_HEADS, 1, T, LANE), lambda b, i: (0, b, 0, 0))
    return pl.pallas_call(
        _nsa_prompt_body,
        grid=(B, T // tq),
        in_specs=[tile(C_MIX), whole, whole, pl.BlockSpec((1, KV_HEADS, sel.shape[2], tq), lambda b, i: (b, 0, 0, i)),
                  tile(C_MIX), tile(LANE)],
        out_specs=tile(C_MIX),
        out_shape=jax.ShapeDtypeStruct((B, T, C_MIX), F32),
        compiler_params=_params("parallel", "arbitrary"),
        name="nsa_prompt",
    )(q, kvs_g, kvw_g, sel, o_c, ng)


DEC_ROWS = 8


def _nsa_decode_body(mode, pages, pos_q0, pos_k0, pt_ref, q_ref, pool_ref, new_ref, *rest):
    if mode == "sel":
        selst_ref, selnew_ref, o_ref, buf, sem, m_scr, l_scr, acc_scr = rest
    else:
        oc_ref, os_ref, ng_ref, o_ref, buf, sem, m_scr, l_scr, acc_scr = rest
    st = pl.program_id(1)
    tq = DEC_ROWS
    tk = pages * PAGE_SIZE
    bps = tk // SEL_BLOCK

    slot = _paged_fetch(pt_ref, pool_ref, buf, sem, pages, PAGE_SIZE)

    @pl.when(st == 0)
    def _():
        m_scr[...] = jnp.full(m_scr.shape, NEG_INF, F32)
        l_scr[...] = jnp.zeros(l_scr.shape, F32)
        acc_scr[...] = jnp.zeros(acc_scr.shape, F32)

    q = q_ref[0] * (HEAD_DIM ** -0.5)
    t = pos_q0 + lax.broadcasted_iota(jnp.int32, (tq, 1), 0)

    def update(kv, kpos, picked):
        for g in range(KV_HEADS):
            mask = (kpos <= t) & (picked(g) if mode == "sel" else (kpos > t - WINDOW))
            mask = jnp.concatenate([mask] * KV_GROUP, axis=0)
            qg = _stack_heads(q, g).astype(BF16)
            carry = (m_scr[g], l_scr[g], acc_scr[g])
            k = kv[:, g * HEAD_DIM:(g + 1) * HEAD_DIM].astype(BF16)
            v = kv[:, LANE + g * HEAD_DIM:LANE + (g + 1) * HEAD_DIM].astype(BF16)
            m_scr[g], l_scr[g], acc_scr[g] = _softmax_step(carry, _dot_nt(qg, k), mask, v)

    kidx = lax.broadcasted_iota(jnp.int32, (1, tk), 1)
    expand = (kidx // SEL_BLOCK == lax.broadcasted_iota(jnp.int32, (bps, 1), 0)).astype(BF16)
    update(buf[slot], pos_k0 + st * tk + kidx,
           lambda g: _dot(selst_ref[0, 0, :, g * bps:(g + 1) * bps].astype(BF16), expand) > 0.5)

    @pl.when(st == pl.num_programs(1) - 1)
    def _():
        update(new_ref[0], pos_q0 + lax.broadcasted_iota(jnp.int32, (1, tq), 1),
               lambda g: selnew_ref[0, :, g:g + 1] > 0.5)
        for g in range(KV_HEADS):
            o = acc_scr[g] / jnp.maximum(l_scr[g], 1e-30)
            if mode == "sel":
                for m in range(KV_GROUP):
                    h = g * KV_GROUP + m
                    o_ref[0, :, h * HEAD_DIM:(h + 1) * HEAD_DIM] = o[m * tq:(m + 1) * tq]
            else:
                _gate_mix(ng_ref[0], oc_ref[0], os_ref[0], o, g, tq, o_ref, os_stacked=False)


def _nsa_decode(mode, q, pool, page_table, kv_new, extras, pages, pos_q0, pos_k0):
    B, n_pages = page_table.shape
    rowblk = lambda a: pl.BlockSpec((1,) + a.shape[1:], lambda b, s, pt: (b,) + (0,) * (a.ndim - 1))
    stepblk = lambda a: pl.BlockSpec((1, 1) + a.shape[2:], lambda b, s, pt: (b, s, 0, 0))
    rows = KV_GROUP * DEC_ROWS
    return pl.pallas_call(
        functools.partial(_nsa_decode_body, mode, pages, pos_q0, pos_k0),
        grid_spec=pltpu.PrefetchScalarGridSpec(
            num_scalar_prefetch=1,
            grid=(B, n_pages // pages),
            in_specs=[rowblk(q), pl.BlockSpec(memory_space=pl.ANY), rowblk(kv_new)]
                     + [stepblk(e) if e.ndim == 4 else rowblk(e) for e in extras],
            out_specs=pl.BlockSpec((1, DEC_ROWS, C_MIX), lambda b, s, pt: (b, 0, 0)),
            scratch_shapes=[pltpu.VMEM((2, pages * PAGE_SIZE, KV_W), F32), pltpu.SemaphoreType.DMA((2,)),
                            pltpu.VMEM((KV_HEADS, rows, 1), F32), pltpu.VMEM((KV_HEADS, rows, 1), F32),
                            pltpu.VMEM((KV_HEADS, rows, HEAD_DIM), F32)],
        ),
        out_shape=jax.ShapeDtypeStruct((B, DEC_ROWS, C_MIX), F32),
        compiler_params=_params("arbitrary", "arbitrary"),
        name="nsa_decode_" + mode,
    )(page_table, q, pool, kv_new, *extras)


ROUTE_W = LANE


def _merge_body(x_ref, oa_ref, ob_ref, mg_ref, wa_ref, wb_ref, wo_ref, n2_ref, wrh_ref, wrl_ref, br_ref,
                x1_ref, h_ref, route_ref):
    mg = mg_ref[...]
    merged = (mg[:, :D_MODEL] * _dot(oa_ref[...].astype(BF16), wa_ref[...])
              + mg[:, D_MODEL:] * _dot(ob_ref[...].astype(BF16), wb_ref[...]))
    x1 = x_ref[...] + _dot(merged.astype(BF16), wo_ref[...])
    x1_ref[...] = x1
    h = x1 * lax.rsqrt(jnp.mean(x1 * x1, axis=-1, keepdims=True) + RMS_EPS) * n2_ref[...]
    h_ref[...] = h.astype(BF16)
    hh, hl = _split2(h)
    logits = _dot(hh, wrh_ref[...]) + _dot(hl, wrh_ref[...]) + _dot(hh, wrl_ref[...]) + br_ref[...]
    lane = lax.broadcasted_iota(jnp.int32, (1, ROUTE_W), 1)
    first = lambda hit: jnp.min(jnp.where(hit, lane, ROUTE_W), axis=-1, keepdims=True)
    is_g = lane < N_GROUPS
    gl = jnp.where(is_g, logits, NEG_INF)
    gmax = jnp.max(gl, axis=-1, keepdims=True)
    g_sel = first(gl == gmax)
    g_w = 1.0 / jnp.sum(jnp.where(is_g, jnp.exp(gl - gmax), 0.0), axis=-1, keepdims=True)
    in_grp = (lane >= N_GROUPS) & (lane < N_GROUPS + N_EXPERTS) & (((lane - N_GROUPS) >> 3) == g_sel)
    el = jnp.where(in_grp, logits, NEG_INF)
    v1 = jnp.max(el, axis=-1, keepdims=True)
    i1 = first(el == v1)
    el2 = jnp.where(lane == i1, NEG_INF, el)
    v2 = jnp.max(el2, axis=-1, keepdims=True)
    i2 = first(el2 == v2)
    d = jnp.exp(v2 - v1)
    w1 = g_w / (1.0 + d)
    w2 = g_w * d / (1.0 + d)
    route_ref[...] = jnp.where(lane == 0, (i1 - N_GROUPS).astype(F32),
                               jnp.where(lane == 1, (i2 - N_GROUPS).astype(F32),
                                         jnp.where(lane == 2, w1, jnp.where(lane == 3, w2, 0.0))))


def _merge(x2d, oa, ob, mg, wa, wb, wo, norm2, w_rg, b_rg, w_re, b_re, tm):
    n = x2d.shape[0]
    const = lambda shape: pl.BlockSpec(shape, lambda i: (0,) * len(shape))
    row = lambda w: pl.BlockSpec((tm, w), lambda i: (i, 0))
    wr = jnp.zeros((D_MODEL, ROUTE_W), F32).at[:, :N_GROUPS].set(w_rg).at[:, N_GROUPS:N_GROUPS + N_EXPERTS].set(w_re)
    br = jnp.zeros((1, ROUTE_W), F32).at[0, :N_GROUPS].set(b_rg).at[0, N_GROUPS:N_GROUPS + N_EXPERTS].set(b_re)
    wrh, wrl = _split2(wr)
    return pl.pallas_call(
        _merge_body,
        grid=(n // tm,),
        in_specs=[row(D_MODEL), row(C_MIX), row(C_MIX), row(2 * D_MODEL), const((C_MIX, D_MODEL)),
                  const((C_MIX, D_MODEL)), const((D_MODEL, D_MODEL)), const((1, D_MODEL)),
                  const((D_MODEL, ROUTE_W)), const((D_MODEL, ROUTE_W)), const((1, ROUTE_W))],
        out_specs=[row(D_MODEL), row(D_MODEL), row(ROUTE_W)],
        out_shape=[jax.ShapeDtypeStruct((n, D_MODEL), F32), jax.ShapeDtypeStruct((n, D_MODEL), BF16),
                   jax.ShapeDtypeStruct((n, ROUTE_W), F32)],
        compiler_params=_params("parallel"),
        name="merge_route",
    )(x2d, oa, ob, mg, wa.astype(BF16), wb.astype(BF16), wo.astype(BF16), norm2.reshape(1, D_MODEL), wrh, wrl, br)


MOE_TB = 256


def _moe_body(be_ref, nu_ref, x_ref, wg_ref, wu_ref, wd_ref, y_ref):
    @pl.when(pl.program_id(0) < nu_ref[0])
    def _():
        x = x_ref[...]
        gate = _dot(x, wg_ref[0])
        hid = gate * _sigmoid(gate) * _dot(x, wu_ref[0])
        y_ref[...] = _dot(hid.astype(BF16), wd_ref[0])

    @pl.when(pl.program_id(0) >= nu_ref[0])
    def _():
        y_ref[...] = jnp.zeros(y_ref.shape, F32)


def _moe_experts(xbuf, blk_e, n_used, wg, wu, wd):
    n_blk = blk_e.shape[0]
    wspec = lambda shape: pl.BlockSpec((1,) + shape, lambda i, be, nu: (be[i], 0, 0))
    return pl.pallas_call(
        _moe_body,
        grid_spec=pltpu.PrefetchScalarGridSpec(
            num_scalar_prefetch=2,
            grid=(n_blk,),
            in_specs=[pl.BlockSpec((MOE_TB, D_MODEL), lambda i, be, nu: (i, 0)), wspec((D_MODEL, D_EXPERT)),
                      wspec((D_MODEL, D_EXPERT)), wspec((D_EXPERT, D_MODEL))],
            out_specs=pl.BlockSpec((MOE_TB, D_MODEL), lambda i, be, nu: (i, 0)),
        ),
        out_shape=jax.ShapeDtypeStruct((n_blk * MOE_TB, D_MODEL), F32),
        compiler_params=_params("arbitrary"),
        name="moe_experts",
    )(blk_e, n_used, xbuf, wg, wu, wd)


def _moe(x1, h, route, wg, wu, wd):
    n = x1.shape[0]
    expert = route[:, :2].astype(jnp.int32).reshape(-1)
    wts = route[:, 2:4]
    n_slots = 2 * n
    n_blk = -(-n_slots // MOE_TB) + N_EXPERTS
    onehot = expert[:, None] == jnp.arange(N_EXPERTS, dtype=jnp.int32)[None, :]
    counts = jnp.sum(onehot, axis=0, dtype=jnp.int32)
    c_start = jnp.cumsum(counts) - counts
    padded = (counts + MOE_TB - 1) // MOE_TB * MOE_TB
    p_end = jnp.cumsum(padded)
    p_start = p_end - padded
    order = jnp.argsort(expert, stable=True).astype(jnp.int32)
    rank = jnp.argsort(order).astype(jnp.int32)
    dest = (rank + jnp.sum(jnp.where(onehot, (p_start - c_start)[None, :], 0), axis=1)).reshape(n, 2)
    blk_e = jnp.minimum(jnp.sum(p_end[None, :] <= (jnp.arange(n_blk, dtype=jnp.int32) * MOE_TB)[:, None], axis=1),
                        N_EXPERTS - 1).astype(jnp.int32)
    n_used = (p_end[-1:] // MOE_TB).astype(jnp.int32)
    k_in_e = jnp.arange(n_blk * MOE_TB, dtype=jnp.int32) - jnp.repeat(p_start[blk_e], MOE_TB)
    src = order[jnp.clip(jnp.repeat(c_start[blk_e], MOE_TB) + k_in_e, 0, n_slots - 1)]
    row_tok = jnp.where(k_in_e < jnp.repeat(counts[blk_e], MOE_TB), src // 2, 0)
    xbuf = h[row_tok]
    ybuf = _moe_experts(xbuf, blk_e, n_used, wg, wu, wd)
    return x1 + wts[:, 0:1] * ybuf[dest[:, 0]] + wts[:, 1:2] * ybuf[dest[:, 1]]


def kernel(x_prompt, x_sample, cache_cmp_kv, cache_slc_kv, cache_win_kv, state_wkv, state_shift, page_table, norm1, w_in, mu_shift, w0, w_decay_up, a0, w_iclr_up, w_gate_up, k_k, k_a, r_k, ln_x_w, ln_x_b, q_norm, kc_norm, ks_norm, kw_norm, cmp_pe_k, cmp_w1_k, cmp_b1_k, cmp_w2_k, cmp_pe_v, cmp_w1_v, cmp_b1_v, cmp_w2_v, w_branch_a, w_branch_b, w_out, norm2, w_route_group, b_route_group, w_route_expert, b_route_expert, w_exp_gate, w_exp_up, w_exp_down):
    assert norm1.shape[0] == 1, "single-layer trunk"
    Bp, Tp, _ = x_prompt.shape
    Bs, Ts, _ = x_sample.shape
    n_pool = cache_cmp_kv.shape[1]
    past = page_table.shape[1] * PAGE_SIZE
    n_buf = cache_win_kv.shape[2]
    kv5 = lambda a, b, t: a.reshape(1, b, t, 2, KV_HEADS, HEAD_DIM)

    w_pad = _pad_w_in(w_in[0])
    rw_p = (mu_shift[0], w0[0], w_decay_up[0], a0[0], w_iclr_up[0], w_gate_up[0], k_k[0], k_a[0],
            r_k[0].reshape(-1), ln_x_w[0], ln_x_b[0])
    wc, pea, peb, b1, w2b = _cmp_weights(cmp_pe_k[0], cmp_w1_k[0], cmp_b1_k[0], cmp_w2_k[0],
                                         cmp_pe_v[0], cmp_w1_v[0], cmp_b1_v[0], cmp_w2_v[0])
    merge_p = (w_branch_a[0], w_branch_b[0], w_out[0], norm2[0], w_route_group[0], b_route_group[0],
               w_route_expert[0], b_route_expert[0])
    moe_w = (w_exp_gate[0].astype(BF16), w_exp_up[0].astype(BF16), w_exp_down[0].astype(BF16))

    xp = x_prompt.reshape(Bp * Tp, D_MODEL)
    zrw, q, kvc, kvs, kvw, ng, mg, kvs_g, kvw_g = _in_proj(xp, norm1[0], w_pad, q_norm[0], ks_norm[0], kw_norm[0], 512)
    zrw3 = zrw.reshape(Bp, Tp, RW_PAD)
    oa, wkv_p = _rwkv(zrw3, jnp.zeros((Bp, 1, RW_PAD), F32), jnp.zeros((Bp, N_HEADS, HEAD_DIM, HEAD_DIM), F32),
                      Tp, *rw_p)
    ab = _cmp_proj(kvc.reshape(Bp * Tp // CMP_STRIDE, CH_W), pea, peb, wc, min(256, Bp * Tp // CMP_STRIDE))
    kc, vc = _cmp_finish(ab.reshape(Bp, Tp // CMP_STRIDE, -1), b1, w2b, kc_norm[0])
    q3 = q.reshape(Bp, Tp, C_MIX)
    o_c, sel = _cmp_attn(q3, kc, vc, 0, (Tp - CMP_LEN) // CMP_STRIDE + 1, Tp // SEL_BLOCK, 128, True)
    ob = _nsa_prompt(q3, kvs_g.reshape(KV_HEADS, Bp, Tp, LANE), kvw_g.reshape(KV_HEADS, Bp, Tp, LANE), sel, o_c,
                     ng.reshape(Bp, Tp, LANE), 128)
    x1, h, route = _merge(xp, oa.reshape(Bp * Tp, C_MIX), ob.reshape(Bp * Tp, C_MIX), mg, *merge_p, 512)
    y_p = _moe(x1, h, route, *moe_w).reshape(Bp, Tp, D_MODEL)
    keep_p = min(WINDOW, Tp)

    xs = x_sample.reshape(Bs * Ts, D_MODEL)
    zrw_s, q_s, kvc_s, kvs_s, kvw_s, ng_s, mg_s, _, _ = _in_proj(xs, norm1[0], w_pad, q_norm[0], ks_norm[0],
                                                                 kw_norm[0], Bs * Ts)
    zrw_s3 = zrw_s.reshape(Bs, Ts, RW_PAD)
    oa_s, wkv_s = _rwkv(jnp.pad(zrw_s3, ((0, 0), (0, RW_CHUNK - Ts), (0, 0))),
                        jnp.pad(state_shift[0], ((0, 0), (0, RW_PAD - RW_IN)))[:, None], state_wkv[0], Ts, *rw_p)
    nc_s = (past + Ts - CMP_LEN) // CMP_STRIDE + 1
    assert (nc_s + CMP_LEN // CMP_STRIDE - 1) * CMP_STRIDE <= past, "compression blocks only cover cached rows"
    ab_s = _cmp_proj_paged(cache_cmp_kv[0].reshape(n_pool, PAGE_SIZE // CMP_STRIDE, CH_W), page_table, pea, peb, wc)
    kc_s, vc_s = _cmp_finish(ab_s, b1, w2b, kc_norm[0])
    rows8 = lambda a: jnp.pad(a.reshape(Bs, Ts, -1), ((0, 0), (0, DEC_ROWS - Ts), (0, 0)))
    q8 = rows8(q_s)
    oc_s, sel_s = _cmp_attn(q8, kc_s, vc_s, past, nc_s, -(-(past + Ts) // SEL_BLOCK), DEC_ROWS, False)
    assert past % SEL_BLOCK == 0 and Ts <= SEL_BLOCK, "the new rows share one selection block"
    n_st = page_table.shape[1] // CMP_PAGES
    sel4 = sel_s.reshape(Bs, DEC_ROWS, KV_HEADS, -1)
    sel_steps = sel4[..., :past // SEL_BLOCK].reshape(Bs, DEC_ROWS, KV_HEADS, n_st, -1)
    sel_steps = sel_steps.transpose(0, 3, 1, 2, 4).reshape(Bs, n_st, DEC_ROWS, -1)
    sel_new = jnp.pad(sel4[..., past // SEL_BLOCK], ((0, 0), (0, 0), (0, LANE - KV_HEADS)))
    os_s = _nsa_decode("sel", q8, cache_slc_kv[0].reshape(n_pool, PAGE_SIZE, KV_W), page_table, rows8(kvs_s),
                       (sel_steps, sel_new), CMP_PAGES, past, 0)
    win_pages = n_buf // PAGE_SIZE
    ob_s = _nsa_decode("win", q8, cache_win_kv[0].reshape(Bs * win_pages, PAGE_SIZE, KV_W),
                       jnp.arange(Bs * win_pages, dtype=jnp.int32).reshape(Bs, win_pages), rows8(kvw_s),
                       (oc_s, os_s, rows8(ng_s)), win_pages, past, past - n_buf)
    x1_s, h_s, route_s = _merge(xs, oa_s[:, :Ts].reshape(Bs * Ts, C_MIX), ob_s[:, :Ts].reshape(Bs * Ts, C_MIX), mg_s,
                                *merge_p, Bs * Ts)
    y_s = _moe(x1_s, h_s, route_s, *moe_w).reshape(Bs, Ts, D_MODEL)
    keep_s = min(WINDOW, n_buf + Ts)
    win_s = jnp.concatenate([cache_win_kv[0].reshape(Bs, n_buf, KV_W), kvw_s.reshape(Bs, Ts, KV_W)], axis=1)

    return (y_p, y_s,
            kv5(kvc, Bp, Tp), kv5(kvs, Bp, Tp), kv5(kvw.reshape(Bp, Tp, KV_W)[:, Tp - keep_p:], Bp, keep_p),
            wkv_p[None], zrw3[:, -1, :RW_IN][None],
            kv5(kvc_s, Bs, Ts), kv5(kvs_s, Bs, Ts), kv5(win_s[:, n_buf + Ts - keep_s:], Bs, keep_s),
            wkv_s[None], zrw_s3[:, -1, :RW_IN][None])
```

```python
import functools
import math

import jax
import jax.numpy as jnp
from jax import lax
from jax.experimental import pallas as pl
from jax.experimental.pallas import tpu as pltpu

F32 = jnp.float32
BF16 = jnp.bfloat16

D_MODEL = 1024
HEAD_DIM = 64
N_HEADS = 8
C_MIX = N_HEADS * HEAD_DIM
R_DECAY, R_ICLR, R_GATE = 32, 32, 96
RW_IN = 3 * C_MIX + R_DECAY + R_ICLR + R_GATE
KV_HEADS = 2
KV_GROUP = N_HEADS // KV_HEADS
KV_W = 2 * KV_HEADS * HEAD_DIM
CMP_LEN, CMP_STRIDE = 32, 16
SEL_BLOCK = 64
N_SEL = 16
WINDOW = 512
PAGE_SIZE = 128
N_GROUPS, EXPERTS_PER_GROUP = 4, 8
N_EXPERTS = N_GROUPS * EXPERTS_PER_GROUP
D_EXPERT = D_MODEL // 2
RMS_EPS = 1e-6
GN_EPS = 64e-5
NEG_INF = -1e30
FORCE_SCORE = 1e6

LANE = 128
VMEM_LIMIT = 56 * 1024 * 1024

RW_PAD = 1792
OFF_Q = RW_PAD
OFF_KVC = OFF_Q + C_MIX
OFF_KVS = OFF_KVC + KV_W
OFF_KVW = OFF_KVS + KV_W
OFF_NG = OFF_KVW + KV_W
OFF_MG = OFF_NG + LANE
N_IN_PAD = OFF_MG + 2 * D_MODEL
RW_TAIL = 3 * C_MIX


def _params(*sem):
    return pltpu.CompilerParams(dimension_semantics=sem, vmem_limit_bytes=VMEM_LIMIT)


def _dot(a, b):
    return jnp.dot(a, b, preferred_element_type=F32)


def _dot_nt(a, b):
    return lax.dot_general(a, b, (((1,), (1,)), ((), ())), preferred_element_type=F32)


def _dot_tn(a, b):
    return lax.dot_general(a, b, (((0,), (0,)), ((), ())), preferred_element_type=F32)


def _split2(x):
    hi = x.astype(BF16)
    lo = (x - hi.astype(F32)).astype(BF16)
    return hi, lo


def _split3(x):
    hi = x.astype(BF16)
    r1 = x - hi.astype(F32)
    mid = r1.astype(BF16)
    lo = (r1 - mid.astype(F32)).astype(BF16)
    return hi, mid, lo


def _gsum(y, g):
    hi, lo = _split2(y)
    return _dot(hi, g) + _dot(lo, g)


def _block_ones(n, blk):
    i = jnp.arange(n) // blk
    return (i[:, None] == i[None, :]).astype(BF16)


def _sigmoid(x):
    return 1.0 / (1.0 + jnp.exp(-x))


def _inproj_body(x_ref, n1_ref, w_ref, g512_ref, g128_ref, qn_ref, ksn_ref, kwn_ref,
                 zrw_ref, q_ref, kvc_ref, kvs_ref, kvw_ref, ng_ref, mg_ref, kvsg_ref, kvwg_ref):
    x = x_ref[...]
    ms = jnp.mean(x * x, axis=-1, keepdims=True)
    xn = (x * lax.rsqrt(ms + RMS_EPS) * n1_ref[...]).astype(BF16)

    def proj(a, b):
        return _dot(xn, w_ref[:, a:b])

    zrw_ref[...] = proj(0, RW_PAD)
    q = proj(OFF_Q, OFF_KVC)
    q_ref[...] = q * lax.rsqrt(_gsum(q * q, g512_ref[...]) * (1.0 / HEAD_DIM) + RMS_EPS) * qn_ref[...]
    kvc_ref[...] = proj(OFF_KVC, OFF_KVS)
    for off, nref, oref, gref in ((OFF_KVS, ksn_ref, kvs_ref, kvsg_ref), (OFF_KVW, kwn_ref, kvw_ref, kvwg_ref)):
        kv = proj(off, off + KV_W)
        k = kv[:, :LANE]
        kn = k * lax.rsqrt(_gsum(k * k, g128_ref[...]) * (1.0 / HEAD_DIM) + RMS_EPS) * nref[...]
        v = kv[:, LANE:]
        oref[:, :LANE] = kn
        oref[:, LANE:] = v
        for g in range(KV_HEADS):
            hs = slice(g * HEAD_DIM, (g + 1) * HEAD_DIM)
            gref[g] = jnp.concatenate([kn[:, hs], v[:, hs]], axis=1).astype(BF16)
    ng_ref[...] = _sigmoid(proj(OFF_NG, OFF_MG))
    mg_ref[...] = _sigmoid(proj(OFF_MG, N_IN_PAD))


def _in_proj(x2d, norm1, w_pad, q_norm, ks_norm, kw_norm, tm):
    n = x2d.shape[0]
    const = lambda shape: pl.BlockSpec(shape, lambda i: (0,) * len(shape))
    row = lambda w: pl.BlockSpec((tm, w), lambda i: (i, 0))
    widths = (RW_PAD, C_MIX, KV_W, KV_W, KV_W, LANE, 2 * D_MODEL)
    return pl.pallas_call(
        _inproj_body,
        grid=(n // tm,),
        in_specs=[row(D_MODEL), const((1, D_MODEL)), const((D_MODEL, N_IN_PAD)), const((C_MIX, C_MIX)),
                  const((LANE, LANE)), const((1, C_MIX)), const((1, LANE)), const((1, LANE))],
        out_specs=[row(w) for w in widths] + [pl.BlockSpec((KV_HEADS, tm, LANE), lambda i: (0, i, 0))] * 2,
        out_shape=[jax.ShapeDtypeStruct((n, w), F32) for w in widths]
                  + [jax.ShapeDtypeStruct((KV_HEADS, n, LANE), BF16)] * 2,
        compiler_params=_params("parallel"),
        name="in_proj",
    )(x2d, norm1.reshape(1, D_MODEL), w_pad, _block_ones(C_MIX, HEAD_DIM), _block_ones(LANE, HEAD_DIM),
      jnp.tile(q_norm, N_HEADS).reshape(1, C_MIX), jnp.tile(ks_norm, KV_HEADS).reshape(1, LANE),
      jnp.tile(kw_norm, KV_HEADS).reshape(1, LANE))


def _pad_w_in(w_in):
    d = w_in.shape[0]
    z = lambda n: jnp.zeros((d, n), w_in.dtype)
    o_q = RW_IN
    o_ng = o_q + C_MIX + 3 * KV_W
    o_mg = o_ng + 3 * N_HEADS
    return jnp.concatenate([w_in[:, :RW_IN], z(RW_PAD - RW_IN), w_in[:, o_q:o_ng], w_in[:, o_ng:o_mg],
                            z(LANE - 3 * N_HEADS), w_in[:, o_mg:]], axis=1).astype(BF16)


RW_CHUNK = 64
RW_HSTACK = 4
RW_ROWS = 2


def _rwkv_body(t_valid, z_ref, sp_ref, s0_ref, mu_ref, w0_ref, a0_ref, kk_ref, ka_ref, rk_ref, lnw_ref, lnb_ref,
               wd_ref, wi_ref, wg_ref, g512_ref, o_ref, s_ref, prev_scr):
    @pl.when(pl.program_id(1) == 0)
    def _():
        s_ref[...] = s0_ref[...]
        prev_scr[...] = sp_ref[...]

    for bi in range(z_ref.shape[0]):
        _rwkv_chunk(t_valid, bi, z_ref, mu_ref, w0_ref, a0_ref, kk_ref, ka_ref, rk_ref, lnw_ref, lnb_ref,
                    wd_ref, wi_ref, wg_ref, g512_ref, o_ref, s_ref, prev_scr)


def _rwkv_chunk(t_valid, bi, z_ref, mu_ref, w0_ref, a0_ref, kk_ref, ka_ref, rk_ref, lnw_ref, lnb_ref,
                wd_ref, wi_ref, wg_ref, g512_ref, o_ref, s_ref, prev_scr):
    C = RW_CHUNK
    c = pl.program_id(1)
    z = z_ref[bi]
    row = lax.broadcasted_iota(jnp.int32, (C, 1), 0)
    z_prev = jnp.where(row == 0, prev_scr[bi], pltpu.roll(z, 1, axis=0))
    prev_scr[bi] = z[C - 1:C]
    zm = z + (z_prev - z) * mu_ref[...]
    r = zm[:, 0:C_MIX]
    k = zm[:, C_MIX:2 * C_MIX]
    v = zm[:, 2 * C_MIX:3 * C_MIX]
    tail = zm[:, RW_TAIL:RW_PAD]
    w_lora = _dot(jnp.tanh(tail).astype(BF16), wd_ref[...])
    a_lora = _dot(tail.astype(BF16), wi_ref[...])
    g = _dot(_sigmoid(tail).astype(BF16), wg_ref[...])
    u = -(w0_ref[...] + w_lora)
    softplus = jnp.maximum(u, 0.0) + jnp.log(1.0 + jnp.exp(-jnp.abs(u)))
    w_log = -softplus - 0.5
    valid = (c * C + row) < t_valid
    ld = jnp.where(valid, -jnp.exp(w_log), 0.0)
    a = _sigmoid(a0_ref[...] + a_lora)
    kk = k * kk_ref[...]
    kk = kk / jnp.maximum(jnp.sqrt(_gsum(kk * kk, g512_ref[...])), 1e-12)
    k2 = k * (1.0 + (a - 1.0) * ka_ref[...])

    ci = lax.broadcasted_iota(jnp.int32, (C, C), 0)
    cj = lax.broadcasted_iota(jnp.int32, (C, C), 1)
    tri = (ci >= cj).astype(BF16)
    h1, h2, h3 = _split3(ld)
    cl = _dot(tri, h1) + _dot(tri, h2) + _dot(tri, h3)
    p_in = jnp.exp(cl)
    p_inv = jnp.exp(-cl)
    r_t = r * p_in
    a_t = -kk * jnp.exp(cl - ld)
    b_t = jnp.where(valid, kk * a * p_inv, 0.0)
    k_t = jnp.where(valid, k2 * p_inv, 0.0)
    p_end = p_in[C - 1:C]

    HS = RW_HSTACK
    R = HS * C
    ri = lax.broadcasted_iota(jnp.int32, (R, R), 0)
    rj = lax.broadcasted_iota(jnp.int32, (R, R), 1)
    same = (ri // C) == (rj // C)
    lower = same & (ri > rj)
    lower_eq = same & (ri >= rj)
    eye = (ri == rj).astype(F32)
    n_lvl = int(math.log2(C))
    y_heads = []
    for hg in range(N_HEADS // HS):
        hs = [hg * HS + m for m in range(HS)]
        stack = lambda x: jnp.concatenate([x[:, h * HEAD_DIM:(h + 1) * HEAD_DIM] for h in hs], axis=0).astype(BF16)
        A, Rr, Bm, Km, V = stack(a_t), stack(r_t), stack(b_t), stack(k_t), stack(v)
        AR = jnp.concatenate([A, Rr], axis=0)
        S4 = _dot_nt(AR, jnp.concatenate([Bm, Km], axis=0))
        L = jnp.where(lower, S4[:R, :R], 0.0)
        Lak = jnp.where(lower, S4[:R, R:], 0.0).astype(BF16)
        Mr = jnp.concatenate([jnp.where(lower_eq, S4[R:, :R], 0.0), jnp.where(lower_eq, S4[R:, R:], 0.0)],
                             axis=1).astype(BF16)
        s0 = [s_ref[bi, h] for h in hs]
        on_state = [_dot_nt(jnp.concatenate([A[m * C:(m + 1) * C], Rr[m * C:(m + 1) * C]], axis=0),
                            s0[m].astype(BF16)) for m in range(HS)]
        rhs = jnp.concatenate([o[:C] for o in on_state], axis=0) + _dot(Lak, V)
        X = eye + L
        Lp = L
        for _ in range(n_lvl - 1):
            Lpb = Lp.astype(BF16)
            Lp = _dot(Lpb, Lpb)
            X = X + _dot(X.astype(BF16), Lp.astype(BF16))
        U = _dot(X.astype(BF16), rhs.astype(BF16)).astype(BF16)
        Y = jnp.concatenate([o[C:] for o in on_state], axis=0) + _dot(Mr, jnp.concatenate([U, V], axis=0))
        for m, h in enumerate(hs):
            rs = slice(m * C, (m + 1) * C)
            upd = _dot_tn(jnp.concatenate([U[rs], V[rs]], axis=0), jnp.concatenate([Bm[rs], Km[rs]], axis=0))
            s_ref[bi, h] = (s0[m] + upd) * p_end[:, h * HEAD_DIM:(h + 1) * HEAD_DIM]
            y_heads.append(Y[rs])
    y = jnp.concatenate(y_heads, axis=1)
    g512 = g512_ref[...]
    yc = y - _gsum(y, g512) * (1.0 / HEAD_DIM)
    var = _gsum(yc * yc, g512) * (1.0 / HEAD_DIM)
    yn = yc * lax.rsqrt(var + GN_EPS) * lnw_ref[...] + lnb_ref[...]
    bonus = _gsum(r * k2 * rk_ref[...], g512) * v
    o_ref[bi] = (yn + bonus) * g


def _rwkv(z_rw, shift_prev, s0, t_valid, mu, w0, wd, a0, wi, wg, k_k, k_a, r_k, ln_w, ln_b):
    B, T, _ = z_rw.shape
    C = RW_CHUNK
    const = lambda shape: pl.BlockSpec(shape, lambda b, c: (0,) * len(shape))
    vec = lambda p: p.reshape(1, C_MIX)
    pad_rows = lambda w, off: jnp.zeros((RW_PAD - RW_TAIL, C_MIX), F32).at[off:off + w.shape[0]].set(w).astype(BF16)
    nb = RW_ROWS
    assert B % nb == 0
    state_spec = pl.BlockSpec((nb, N_HEADS, HEAD_DIM, HEAD_DIM), lambda b, c: (b, 0, 0, 0))
    return pl.pallas_call(
        functools.partial(_rwkv_body, t_valid),
        grid=(B // nb, T // C),
        in_specs=[pl.BlockSpec((nb, C, RW_PAD), lambda b, c: (b, c, 0)),
                  pl.BlockSpec((nb, 1, RW_PAD), lambda b, c: (b, 0, 0)),
                  state_spec, const((1, RW_PAD))] + [const((1, C_MIX))] * 7
                 + [const((RW_PAD - RW_TAIL, C_MIX))] * 3 + [const((C_MIX, C_MIX))],
        out_specs=[pl.BlockSpec((nb, C, C_MIX), lambda b, c: (b, c, 0)), state_spec],
        out_shape=[jax.ShapeDtypeStruct((B, T, C_MIX), F32),
                   jax.ShapeDtypeStruct((B, N_HEADS, HEAD_DIM, HEAD_DIM), F32)],
        scratch_shapes=[pltpu.VMEM((nb, 1, RW_PAD), F32)],
        compiler_params=_params("parallel", "arbitrary"),
        name="rwkv7",
    )(z_rw, shift_prev, s0, jnp.pad(mu, (0, RW_PAD - RW_IN)).reshape(1, RW_PAD), vec(w0), vec(a0), vec(k_k),
      vec(k_a), vec(r_k), vec(ln_w), vec(ln_b), pad_rows(wd, 0), pad_rows(wi, R_DECAY),
      pad_rows(wg, R_DECAY + R_ICLR), _block_ones(C_MIX, HEAD_DIM))


CH_W = CMP_STRIDE * KV_W
N_SLOT = 2 * KV_HEADS


def _cmp_weights(pe_k, w1_k, b1_k, w2_k, pe_v, w1_v, b1_v, w2_v):
    eye = jnp.eye(N_SLOT, dtype=F32)
    w1 = jnp.stack([w1_k, w1_k, w1_v, w1_v])
    pe = jnp.stack([pe_k, pe_k, pe_v, pe_v])
    halves = []
    pes = []
    for r in range(CMP_LEN // CMP_STRIDE):
        ls = slice(r * CMP_STRIDE, (r + 1) * CMP_STRIDE)
        halves.append(jnp.einsum('sldf,st->lsdtf', w1[:, ls], eye).reshape(CH_W, N_SLOT * HEAD_DIM))
        pes.append(jnp.transpose(pe[:, ls], (1, 0, 2)).reshape(1, CH_W))
    wc = jnp.concatenate(halves, axis=1).astype(BF16)
    w2 = jnp.stack([w2_k, w2_k, w2_v, w2_v])
    w2b = jnp.einsum('sfd,st->sftd', w2, eye).reshape(N_SLOT * HEAD_DIM, N_SLOT * HEAD_DIM).astype(BF16)
    b1 = jnp.concatenate([b1_k, b1_k, b1_v, b1_v]).reshape(1, N_SLOT * HEAD_DIM)
    return wc, pes[0], pes[1], b1, w2b


def _cmp_proj_body(ch_ref, pea_ref, peb_ref, wc_ref, ab_ref):
    ch = ch_ref[...]
    w = N_SLOT * HEAD_DIM
    ab_ref[:, :w] = _dot((ch + pea_ref[...]).astype(BF16), wc_ref[:, :w])
    ab_ref[:, w:] = _dot((ch + peb_ref[...]).astype(BF16), wc_ref[:, w:])


def _cmp_proj(chunks, pea, peb, wc, tr):
    n = chunks.shape[0]
    const = lambda shape: pl.BlockSpec(shape, lambda i: (0,) * len(shape))
    return pl.pallas_call(
        _cmp_proj_body,
        grid=(n // tr,),
        in_specs=[pl.BlockSpec((tr, CH_W), lambda i: (i, 0)), const((1, CH_W)), const((1, CH_W)),
                  const((CH_W, 2 * N_SLOT * HEAD_DIM))],
        out_specs=pl.BlockSpec((tr, 2 * N_SLOT * HEAD_DIM), lambda i: (i, 0)),
        out_shape=jax.ShapeDtypeStruct((n, 2 * N_SLOT * HEAD_DIM), F32),
        compiler_params=_params("parallel"),
        name="cmp_proj",
    )(chunks, pea, peb, wc)


CMP_PAGES = 16


def _paged_fetch(pt_ref, pool_ref, buf, sem, pages, page_rows):
    b = pl.program_id(0)
    st = pl.program_id(1)
    n_st = pl.num_programs(1)
    step = b * n_st + st
    slot = step % 2
    n_split, lanes = buf.shape[1], buf.shape[3]

    def copies(bb, stt, sl):
        out = []
        for i in range(pages):
            page = pt_ref[bb, stt * pages + i]
            for j in range(n_split):
                out.append(pltpu.make_async_copy(pool_ref.at[page, :, pl.ds(j * lanes, lanes)],
                                                 buf.at[sl, j, pl.ds(i * page_rows, page_rows)], sem.at[sl]))
        return out

    @pl.when(step == 0)
    def _():
        for c in copies(0, 0, 0):
            c.start()

    @pl.when(step + 1 < pl.num_programs(0) * n_st)
    def _():
        wrap = st + 1 == n_st
        for c in copies(jnp.where(wrap, b + 1, b), jnp.where(wrap, 0, st + 1), 1 - slot):
            c.start()

    for c in copies(b, st, slot):
        c.wait()
    return slot


def _cmp_proj_paged_body(pt_ref, cache_ref, pe_ref, wl_ref, ab_ref, buf, sem):
    slot = _paged_fetch(pt_ref, cache_ref, buf, sem, CMP_PAGES, PAGE_SIZE)
    n_chunks = CMP_PAGES * PAGE_SIZE // CMP_STRIDE
    w = N_SLOT * HEAD_DIM
    acc = [jnp.zeros((n_chunks, w), F32) for _ in range(CMP_LEN // CMP_STRIDE)]
    for l in range(CMP_STRIDE):
        x = jnp.concatenate([buf[slot, j, pl.ds(l, n_chunks, stride=CMP_STRIDE), :] for j in range(buf.shape[1])],
                            axis=1)
        for r in range(CMP_LEN // CMP_STRIDE):
            acc[r] = acc[r] + _dot((x + pe_ref[r, l]).astype(BF16), wl_ref[l, :, r * w:(r + 1) * w])
    for r in range(CMP_LEN // CMP_STRIDE):
        ab_ref[0, :, r * w:(r + 1) * w] = acc[r]


def _cmp_proj_paged(cache, page_table, pea, peb, wc):
    B, n_pages = page_table.shape
    rows = PAGE_SIZE // CMP_STRIDE
    w = N_SLOT * HEAD_DIM
    const = lambda shape: pl.BlockSpec(shape, lambda b, g, pt: (0,) * len(shape))
    pe = jnp.stack([pea, peb]).reshape(CMP_LEN // CMP_STRIDE, CMP_STRIDE, 1, w)
    wl = wc.reshape(CMP_STRIDE, w, 2 * w)
    return pl.pallas_call(
        _cmp_proj_paged_body,
        grid_spec=pltpu.PrefetchScalarGridSpec(
            num_scalar_prefetch=1,
            grid=(B, n_pages // CMP_PAGES),
            in_specs=[pl.BlockSpec(memory_space=pl.ANY), const(pe.shape), const(wl.shape)],
            out_specs=pl.BlockSpec((1, CMP_PAGES * rows, 2 * w), lambda b, g, pt: (b, g, 0)),
            scratch_shapes=[pltpu.VMEM((2, KV_W // LANE, CMP_PAGES * PAGE_SIZE, LANE), F32),
                            pltpu.SemaphoreType.DMA((2,))],
        ),
        out_shape=jax.ShapeDtypeStruct((B, n_pages * rows, 2 * w), F32),
        compiler_params=_params("arbitrary", "arbitrary"),
        name="cmp_proj_paged",
    )(page_table, cache, pe, wl)


def _cmp_finish_body(ab_ref, b1_ref, w2_ref, kcn_ref, g128_ref, kc_ref, vc_ref):
    ab = ab_ref[0]
    n = ab.shape[0]
    w = N_SLOT * HEAD_DIM
    pre = ab[:, :w] + pltpu.roll(ab[:, w:], n - 1, axis=0) + b1_ref[...]
    hid = pre * _sigmoid(pre)
    out = _dot(hid.astype(BF16), w2_ref[...])
    k = out[:, :LANE]
    kc_ref[0] = k * lax.rsqrt(_gsum(k * k, g128_ref[...]) * (1.0 / HEAD_DIM) + RMS_EPS) * kcn_ref[...]
    vc_ref[0] = out[:, LANE:]


def _cmp_finish(ab, b1, w2b, kc_norm):
    B, n, _ = ab.shape
    const = lambda shape: pl.BlockSpec(shape, lambda b: (0,) * len(shape))
    w = N_SLOT * HEAD_DIM
    return pl.pallas_call(
        _cmp_finish_body,
        grid=(B,),
        in_specs=[pl.BlockSpec((1, n, 2 * w), lambda b: (b, 0, 0)), const((1, w)), const((w, w)), const((1, LANE)),
                  const((LANE, LANE))],
        out_specs=[pl.BlockSpec((1, n, LANE), lambda b: (b, 0, 0))] * 2,
        out_shape=[jax.ShapeDtypeStruct((B, n, LANE), F32)] * 2,
        compiler_params=_params("parallel"),
        name="cmp_finish",
    )(ab, b1, w2b, jnp.tile(kc_norm, KV_HEADS).reshape(1, LANE), _block_ones(LANE, HEAD_DIM))


def _cmp_attn_body(pos0, nc, nb, nbp, blocks_on_rows, q_ref, kc_ref, vc_ref, ovl_ref, o_ref, sel_ref):
    tq = q_ref.shape[1]
    ncp = kc_ref.shape[1]
    q = q_ref[0] * (HEAD_DIM ** -0.5)
    t0 = pos0 + pl.program_id(1) * tq
    t = t0 + lax.broadcasted_iota(jnp.int32, (tq, 1), 0)
    cidx = lax.broadcasted_iota(jnp.int32, (1, ncp), 1)
    cmask = (cidx * CMP_STRIDE + (CMP_LEN - 1) <= t) & (cidx < nc)
    if blocks_on_rows:
        tt = t0 + lax.broadcasted_iota(jnp.int32, (1, tq), 1)
        j = lax.broadcasted_iota(jnp.int32, (nbp, 1), 0)
    else:
        tt = t
        j = lax.broadcasted_iota(jnp.int32, (1, nbp), 1)
    cur = tt // SEL_BLOCK
    valid = (j * SEL_BLOCK <= tt) & (j < nb)
    forced = (j == 0) | (j == cur) | (j == cur - 1)
    for g in range(KV_HEADS):
        kcg = kc_ref[0, :, g * HEAD_DIM:(g + 1) * HEAD_DIM].astype(BF16)
        vcg = vc_ref[0, :, g * HEAD_DIM:(g + 1) * HEAD_DIM].astype(BF16)
        psum = jnp.zeros((tq, ncp), F32)
        for m in range(KV_GROUP):
            sl = slice((g * KV_GROUP + m) * HEAD_DIM, (g * KV_GROUP + m + 1) * HEAD_DIM)
            s = jnp.where(cmask, _dot_nt(q[:, sl].astype(BF16), kcg), NEG_INF)
            e = jnp.where(cmask, jnp.exp(s - jnp.max(s, axis=-1, keepdims=True)), 0.0)
            p = e / jnp.maximum(jnp.sum(e, axis=-1, keepdims=True), 1e-30)
            o_ref[0, :, sl] = _dot(p.astype(BF16), vcg)
            psum = psum + p
        hi, lo = _split2(psum)
        if blocks_on_rows:
            imp = _dot_nt(ovl_ref[...], hi) + _dot_nt(ovl_ref[...], lo)
        else:
            imp = _dot(hi, ovl_ref[...]) + _dot(lo, ovl_ref[...])
        score = jnp.where(valid, jnp.where(forced, FORCE_SCORE, imp), NEG_INF)
        cnt = jnp.zeros(score.shape, jnp.int32)
        for jp in range(nb):
            cj = score[jp:jp + 1, :] if blocks_on_rows else score[:, jp:jp + 1]
            cnt = cnt + jnp.where(j > jp, jnp.where(cj >= score, 1, 0), jnp.where(cj > score, 1, 0))
        picked = (cnt < N_SEL).astype(F32)
        if blocks_on_rows:
            sel_ref[0, g] = picked
        else:
            sel_ref[0, :, g * nbp:(g + 1) * nbp] = picked


def _cmp_attn(q, kc, vc, pos0, nc, nb, tq, blocks_on_rows):
    B, Tq, _ = q.shape
    ncp = kc.shape[1]
    nbp = -(-nb // SEL_BLOCK) * SEL_BLOCK
    c0 = jnp.arange(ncp)[:, None] * CMP_STRIDE
    jj = jnp.arange(nbp)[None, :]
    ovl = ((c0 < (jj + 1) * SEL_BLOCK) & (c0 + CMP_LEN > jj * SEL_BLOCK) & (jnp.arange(ncp)[:, None] < nc)
           & (jj < nb)).astype(BF16)
    if blocks_on_rows:
        ovl = ovl.T
        sel_spec = pl.BlockSpec((1, KV_HEADS, nbp, tq), lambda b, i: (b, 0, 0, i))
        sel_shape = (B, KV_HEADS, nbp, Tq)
    else:
        sel_spec = pl.BlockSpec((1, tq, KV_HEADS * nbp), lambda b, i: (b, i, 0))
        sel_shape = (B, Tq, KV_HEADS * nbp)
    return pl.pallas_call(
        functools.partial(_cmp_attn_body, pos0, nc, nb, nbp, blocks_on_rows),
        grid=(B, Tq // tq),
        in_specs=[pl.BlockSpec((1, tq, C_MIX), lambda b, i: (b, i, 0)),
                  pl.BlockSpec((1, ncp, LANE), lambda b, i: (b, 0, 0)),
                  pl.BlockSpec((1, ncp, LANE), lambda b, i: (b, 0, 0)),
                  pl.BlockSpec(ovl.shape, lambda b, i: (0, 0))],
        out_specs=[pl.BlockSpec((1, tq, C_MIX), lambda b, i: (b, i, 0)), sel_spec],
        out_shape=[jax.ShapeDtypeStruct((B, Tq, C_MIX), F32), jax.ShapeDtypeStruct(sel_shape, F32)],
        compiler_params=_params("parallel", "parallel"),
        name="cmp_attn",
    )(q, kc, vc, ovl)


def _softmax_step(carry, s, mask, vb):
    m_, l_, acc = carry
    s = jnp.where(mask, s, NEG_INF)
    m_new = jnp.maximum(m_, jnp.max(s, axis=-1, keepdims=True))
    alpha = jnp.exp(m_ - m_new)
    p = jnp.where(mask, jnp.exp(s - m_new), 0.0)
    return m_new, alpha * l_ + jnp.sum(p, axis=-1, keepdims=True), alpha * acc + _dot(p.astype(BF16), vb)


def _softmax_init(rows):
    return jnp.full((rows, 1), NEG_INF, F32), jnp.zeros((rows, 1), F32), jnp.zeros((rows, HEAD_DIM), F32)


def _stack_heads(x, g):
    return jnp.concatenate([x[:, (g * KV_GROUP + m) * HEAD_DIM:(g * KV_GROUP + m + 1) * HEAD_DIM]
                            for m in range(KV_GROUP)], axis=0)


def _gate_mix(ng, o_c, o_s, o_w, g, tq, o_ref, os_stacked=True):
    for m in range(KV_GROUP):
        h = g * KV_GROUP + m
        sl = slice(h * HEAD_DIM, (h + 1) * HEAD_DIM)
        rs = slice(m * tq, (m + 1) * tq)
        o_ref[0, :, sl] = (ng[:, h:h + 1] * o_c[:, sl]
                           + ng[:, N_HEADS + h:N_HEADS + h + 1] * (o_s[rs] if os_stacked else o_s[:, sl])
                           + ng[:, 2 * N_HEADS + h:2 * N_HEADS + h + 1] * o_w[rs])


SEL_TK = 512


def _flash_step(carry, qg, kv, bias, tq):
    m_, l_, acc = carry
    tk = kv.shape[0]
    s = _dot_nt(qg, kv).reshape(KV_GROUP, tq, tk) + bias[None]
    m_new = jnp.maximum(m_, jnp.max(s, axis=-1, keepdims=True))
    alpha = jnp.exp(m_ - m_new)
    p = jnp.exp(s - m_new)
    l_new = alpha * l_ + jnp.sum(p, axis=-1, keepdims=True)
    pv = _dot(p.reshape(KV_GROUP * tq, tk).astype(BF16), kv).reshape(KV_GROUP, tq, LANE)
    return m_new, l_new, alpha * acc + pv


def _flash_init(tq):
    return (jnp.full((KV_GROUP, tq, 1), NEG_INF, F32), jnp.zeros((KV_GROUP, tq, 1), F32),
            jnp.zeros((KV_GROUP, tq, LANE), F32))


def _nsa_prompt_body(q_ref, kvs_ref, kvw_ref, sel_ref, oc_ref, ng_ref, o_ref):
    tq = q_ref.shape[1]
    nbp = sel_ref.shape[2]
    q0 = pl.program_id(1) * tq
    q = q_ref[0] * (HEAD_DIM ** -0.5)
    t = q0 + lax.broadcasted_iota(jnp.int32, (tq, 1), 0)
    blk = lax.broadcasted_iota(jnp.int32, (nbp, 1), 0)
    zpad = jnp.zeros((tq, LANE - HEAD_DIM), F32)
    qgs = []
    sels = []
    for g in range(KV_HEADS):
        heads = [jnp.concatenate([q[:, (g * KV_GROUP + m) * HEAD_DIM:(g * KV_GROUP + m + 1) * HEAD_DIM], zpad], axis=1)
                 for m in range(KV_GROUP)]
        qgs.append(jnp.concatenate(heads, axis=0).astype(BF16))
        sels.append(sel_ref[0, g].astype(BF16))

    def sel_step(kt, carry):
        k0 = pl.multiple_of(kt * SEL_TK, SEL_TK)
        kpos = k0 + lax.broadcasted_iota(jnp.int32, (1, SEL_TK), 1)
        expand = (kpos // SEL_BLOCK == blk).astype(BF16)
        causal = kpos <= t
        out = []
        for g in range(KV_HEADS):
            bias = jnp.where((_dot_tn(sels[g], expand) > 0.5) & causal, 0.0, NEG_INF)
            out.append(_flash_step(carry[g], qgs[g], kvs_ref[g, 0, pl.ds(k0, SEL_TK), :], bias, tq))
        return tuple(out)

    res_s = lax.fori_loop(0, (q0 + tq + SEL_TK - 1) // SEL_TK, sel_step, (_flash_init(tq),) * KV_HEADS)

    span = WINDOW + tq
    w0 = pl.multiple_of(jnp.maximum(q0 - WINDOW, 0), tq)
    wpos = w0 + lax.broadcasted_iota(jnp.int32, (1, span), 1)
    wbias = jnp.where((wpos <= t) & (wpos > t - WINDOW), 0.0, NEG_INF)
    res_w = tuple(_flash_step(_flash_init(tq), qgs[g], kvw_ref[g, 0, pl.ds(w0, span), :], wbias, tq)
                  for g in range(KV_HEADS))

    ng = ng_ref[0]
    oc = oc_ref[0]
    for g in range(KV_HEADS):
        o_s = res_s[g][2][:, :, HEAD_DIM:] / jnp.maximum(res_s[g][1], 1e-30)
        o_w = res_w[g][2][:, :, HEAD_DIM:] / jnp.maximum(res_w[g][1], 1e-30)
        for m in range(KV_GROUP):
            h = g * KV_GROUP + m
            sl = slice(h * HEAD_DIM, (h + 1) * HEAD_DIM)
            o_ref[0, :, sl] = (ng[:, h:h + 1] * oc[:, sl] + ng[:, N_HEADS + h:N_HEADS + h + 1] * o_s[m]
                               + ng[:, 2 * N_HEADS + h:2 * N_HEADS + h + 1] * o_w[m])


def _nsa_prompt(q, kvs_g, kvw_g, sel, o_c, ng, tq):
    B, T, _ = q.shape
    tile = lambda w: pl.BlockSpec((1, tq, w), lambda b, i: (b, i, 0))
    whole = pl.BlockSpec((KV_HEADS, 1, T, LANE), lambda b, i: (0, b, 0, 0))
    return pl.pallas_call(
        _nsa_prompt_body,
        grid=(B, T // tq),
        in_specs=[tile(C_MIX), whole, whole, pl.BlockSpec((1, KV_HEADS, sel.shape[2], tq), lambda b, i: (b, 0, 0, i)),
                  tile(C_MIX), tile(LANE)],
        out_specs=tile(C_MIX),
        out_shape=jax.ShapeDtypeStruct((B, T, C_MIX), F32),
        compiler_params=_params("parallel", "arbitrary"),
        name="nsa_prompt",
    )(q, kvs_g, kvw_g, sel, o_c, ng)


DEC_ROWS = 8


def _nsa_decode_body(mode, pages, pos_q0, pos_k0, pt_ref, q_ref, pool_ref, new_ref, *rest):
    if mode == "sel":
        selst_ref, selnew_ref, o_ref, buf, sem, m_scr, l_scr, acc_scr = rest
    else:
        oc_ref, os_ref, ng_ref, o_ref, buf, sem, m_scr, l_scr, acc_scr = rest
    st = pl.program_id(1)
    tq = DEC_ROWS
    tk = pages * PAGE_SIZE
    bps = tk // SEL_BLOCK

    slot = _paged_fetch(pt_ref, pool_ref, buf, sem, pages, PAGE_SIZE)

    @pl.when(st == 0)
    def _():
        m_scr[...] = jnp.full(m_scr.shape, NEG_INF, F32)
        l_scr[...] = jnp.zeros(l_scr.shape, F32)
        acc_scr[...] = jnp.zeros(acc_scr.shape, F32)

    q = q_ref[0] * (HEAD_DIM ** -0.5)
    t = pos_q0 + lax.broadcasted_iota(jnp.int32, (tq, 1), 0)

    def update(kv, kpos, picked):
        for g in range(KV_HEADS):
            mask = (kpos <= t) & (picked(g) if mode == "sel" else (kpos > t - WINDOW))
            mask = jnp.concatenate([mask] * KV_GROUP, axis=0)
            qg = _stack_heads(q, g).astype(BF16)
            carry = (m_scr[g], l_scr[g], acc_scr[g])
            k = kv[:, g * HEAD_DIM:(g + 1) * HEAD_DIM].astype(BF16)
            v = kv[:, LANE + g * HEAD_DIM:LANE + (g + 1) * HEAD_DIM].astype(BF16)
            m_scr[g], l_scr[g], acc_scr[g] = _softmax_step(carry, _dot_nt(qg, k), mask, v)

    kidx = lax.broadcasted_iota(jnp.int32, (1, tk), 1)
    expand = (kidx // SEL_BLOCK == lax.broadcasted_iota(jnp.int32, (bps, 1), 0)).astype(BF16)
    update(buf[slot, 0], pos_k0 + st * tk + kidx,
           lambda g: _dot(selst_ref[0, 0, :, g * bps:(g + 1) * bps].astype(BF16), expand) > 0.5)

    @pl.when(st == pl.num_programs(1) - 1)
    def _():
        update(new_ref[0], pos_q0 + lax.broadcasted_iota(jnp.int32, (1, tq), 1),
               lambda g: selnew_ref[0, :, g:g + 1] > 0.5)
        for g in range(KV_HEADS):
            o = acc_scr[g] / jnp.maximum(l_scr[g], 1e-30)
            if mode == "sel":
                for m in range(KV_GROUP):
                    h = g * KV_GROUP + m
                    o_ref[0, :, h * HEAD_DIM:(h + 1) * HEAD_DIM] = o[m * tq:(m + 1) * tq]
            else:
                _gate_mix(ng_ref[0], oc_ref[0], os_ref[0], o, g, tq, o_ref, os_stacked=False)


def _nsa_decode(mode, q, pool, page_table, kv_new, extras, pages, pos_q0, pos_k0):
    B, n_pages = page_table.shape
    rowblk = lambda a: pl.BlockSpec((1,) + a.shape[1:], lambda b, s, pt: (b,) + (0,) * (a.ndim - 1))
    stepblk = lambda a: pl.BlockSpec((1, 1) + a.shape[2:], lambda b, s, pt: (b, s, 0, 0))
    rows = KV_GROUP * DEC_ROWS
    return pl.pallas_call(
        functools.partial(_nsa_decode_body, mode, pages, pos_q0, pos_k0),
        grid_spec=pltpu.PrefetchScalarGridSpec(
            num_scalar_prefetch=1,
            grid=(B, n_pages // pages),
            in_specs=[rowblk(q), pl.BlockSpec(memory_space=pl.ANY), rowblk(kv_new)]
                     + [stepblk(e) if e.ndim == 4 else rowblk(e) for e in extras],
            out_specs=pl.BlockSpec((1, DEC_ROWS, C_MIX), lambda b, s, pt: (b, 0, 0)),
            scratch_shapes=[pltpu.VMEM((2, 1, pages * PAGE_SIZE, KV_W), F32), pltpu.SemaphoreType.DMA((2,)),
                            pltpu.VMEM((KV_HEADS, rows, 1), F32), pltpu.VMEM((KV_HEADS, rows, 1), F32),
                            pltpu.VMEM((KV_HEADS, rows, HEAD_DIM), F32)],
        ),
        out_shape=jax.ShapeDtypeStruct((B, DEC_ROWS, C_MIX), F32),
        compiler_params=_params("arbitrary", "arbitrary"),
        name="nsa_decode_" + mode,
    )(page_table, q, pool, kv_new, *extras)


ROUTE_W = LANE


def _merge_body(x_ref, oa_ref, ob_ref, mg_ref, wa_ref, wb_ref, wo_ref, n2_ref, wrh_ref, wrl_ref, br_ref,
                x1_ref, h_ref, route_ref):
    mg = mg_ref[...]
    merged = (mg[:, :D_MODEL] * _dot(oa_ref[...].astype(BF16), wa_ref[...])
              + mg[:, D_MODEL:] * _dot(ob_ref[...].astype(BF16), wb_ref[...]))
    x1 = x_ref[...] + _dot(merged.astype(BF16), wo_ref[...])
    x1_ref[...] = x1
    h = x1 * lax.rsqrt(jnp.mean(x1 * x1, axis=-1, keepdims=True) + RMS_EPS) * n2_ref[...]
    h_ref[...] = h.astype(BF16)
    hh, hl = _split2(h)
    logits = _dot(hh, wrh_ref[...]) + _dot(hl, wrh_ref[...]) + _dot(hh, wrl_ref[...]) + br_ref[...]
    lane = lax.broadcasted_iota(jnp.int32, (1, ROUTE_W), 1)
    first = lambda hit: jnp.min(jnp.where(hit, lane, ROUTE_W), axis=-1, keepdims=True)
    is_g = lane < N_GROUPS
    gl = jnp.where(is_g, logits, NEG_INF)
    gmax = jnp.max(gl, axis=-1, keepdims=True)
    g_sel = first(gl == gmax)
    g_w = 1.0 / jnp.sum(jnp.where(is_g, jnp.exp(gl - gmax), 0.0), axis=-1, keepdims=True)
    in_grp = (lane >= N_GROUPS) & (lane < N_GROUPS + N_EXPERTS) & (((lane - N_GROUPS) >> 3) == g_sel)
    el = jnp.where(in_grp, logits, NEG_INF)
    v1 = jnp.max(el, axis=-1, keepdims=True)
    i1 = first(el == v1)
    el2 = jnp.where(lane == i1, NEG_INF, el)
    v2 = jnp.max(el2, axis=-1, keepdims=True)
    i2 = first(el2 == v2)
    d = jnp.exp(v2 - v1)
    w1 = g_w / (1.0 + d)
    w2 = g_w * d / (1.0 + d)
    route_ref[...] = jnp.where(lane == 0, (i1 - N_GROUPS).astype(F32),
                               jnp.where(lane == 1, (i2 - N_GROUPS).astype(F32),
                                         jnp.where(lane == 2, w1, jnp.where(lane == 3, w2, 0.0))))


def _merge(x2d, oa, ob, mg, wa, wb, wo, norm2, w_rg, b_rg, w_re, b_re, tm):
    n = x2d.shape[0]
    const = lambda shape: pl.BlockSpec(shape, lambda i: (0,) * len(shape))
    row = lambda w: pl.BlockSpec((tm, w), lambda i: (i, 0))
    wr = jnp.zeros((D_MODEL, ROUTE_W), F32).at[:, :N_GROUPS].set(w_rg).at[:, N_GROUPS:N_GROUPS + N_EXPERTS].set(w_re)
    br = jnp.zeros((1, ROUTE_W), F32).at[0, :N_GROUPS].set(b_rg).at[0, N_GROUPS:N_GROUPS + N_EXPERTS].set(b_re)
    wrh, wrl = _split2(wr)
    return pl.pallas_call(
        _merge_body,
        grid=(n // tm,),
        in_specs=[row(D_MODEL), row(C_MIX), row(C_MIX), row(2 * D_MODEL), const((C_MIX, D_MODEL)),
                  const((C_MIX, D_MODEL)), const((D_MODEL, D_MODEL)), const((1, D_MODEL)),
                  const((D_MODEL, ROUTE_W)), const((D_MODEL, ROUTE_W)), const((1, ROUTE_W))],
        out_specs=[row(D_MODEL), row(D_MODEL), row(ROUTE_W)],
        out_shape=[jax.ShapeDtypeStruct((n, D_MODEL), F32), jax.ShapeDtypeStruct((n, D_MODEL), BF16),
                   jax.ShapeDtypeStruct((n, ROUTE_W), F32)],
        compiler_params=_params("parallel"),
        name="merge_route",
    )(x2d, oa, ob, mg, wa.astype(BF16), wb.astype(BF16), wo.astype(BF16), norm2.reshape(1, D_MODEL), wrh, wrl, br)


MOE_TB = 256


def _moe_body(be_ref, nu_ref, x_ref, wg_ref, wu_ref, wd_ref, y_ref, wg_b, wu_b, wd_b):
    i = pl.program_id(0)

    @pl.when((i == 0) | (be_ref[i] != be_ref[jnp.maximum(i - 1, 0)]))
    def _():
        wg_b[...] = wg_ref[0].astype(BF16)
        wu_b[...] = wu_ref[0].astype(BF16)
        wd_b[...] = wd_ref[0].astype(BF16)

    @pl.when(i < nu_ref[0])
    def _():
        x = x_ref[...]
        gate = _dot(x, wg_b[...])
        hid = gate * _sigmoid(gate) * _dot(x, wu_b[...])
        y_ref[...] = _dot(hid.astype(BF16), wd_b[...])

    @pl.when(i >= nu_ref[0])
    def _():
        y_ref[...] = jnp.zeros(y_ref.shape, F32)


def _moe_experts(xbuf, blk_e, n_used, wg, wu, wd):
    n_blk = blk_e.shape[0]
    wspec = lambda shape: pl.BlockSpec((1,) + shape, lambda i, be, nu: (be[i], 0, 0))
    return pl.pallas_call(
        _moe_body,
        grid_spec=pltpu.PrefetchScalarGridSpec(
            num_scalar_prefetch=2,
            grid=(n_blk,),
            in_specs=[pl.BlockSpec((MOE_TB, D_MODEL), lambda i, be, nu: (i, 0)), wspec((D_MODEL, D_EXPERT)),
                      wspec((D_MODEL, D_EXPERT)), wspec((D_EXPERT, D_MODEL))],
            out_specs=pl.BlockSpec((MOE_TB, D_MODEL), lambda i, be, nu: (i, 0)),
            scratch_shapes=[pltpu.VMEM((D_MODEL, D_EXPERT), BF16), pltpu.VMEM((D_MODEL, D_EXPERT), BF16),
                            pltpu.VMEM((D_EXPERT, D_MODEL), BF16)],
        ),
        out_shape=jax.ShapeDtypeStruct((n_blk * MOE_TB, D_MODEL), F32),
        compiler_params=_params("arbitrary"),
        name="moe_experts",
    )(blk_e, n_used, xbuf, wg, wu, wd)


def _moe(x1, h, route, wg, wu, wd):
    n = x1.shape[0]
    expert = route[:, :2].astype(jnp.int32).reshape(-1)
    wts = route[:, 2:4]
    n_slots = 2 * n
    n_blk = -(-n_slots // MOE_TB) + N_EXPERTS
    onehot = expert[:, None] == jnp.arange(N_EXPERTS, dtype=jnp.int32)[None, :]
    counts = jnp.sum(onehot, axis=0, dtype=jnp.int32)
    c_start = jnp.cumsum(counts) - counts
    padded = (counts + MOE_TB - 1) // MOE_TB * MOE_TB
    p_end = jnp.cumsum(padded)
    p_start = p_end - padded
    order = jnp.argsort(expert, stable=True).astype(jnp.int32)
    rank = jnp.argsort(order).astype(jnp.int32)
    dest = (rank + jnp.sum(jnp.where(onehot, (p_start - c_start)[None, :], 0), axis=1)).reshape(n, 2)
    blk_e = jnp.minimum(jnp.sum(p_end[None, :] <= (jnp.arange(n_blk, dtype=jnp.int32) * MOE_TB)[:, None], axis=1),
                        N_EXPERTS - 1).astype(jnp.int32)
    n_used = (p_end[-1:] // MOE_TB).astype(jnp.int32)
    k_in_e = jnp.arange(n_blk * MOE_TB, dtype=jnp.int32) - jnp.repeat(p_start[blk_e], MOE_TB)
    src = order[jnp.clip(jnp.repeat(c_start[blk_e], MOE_TB) + k_in_e, 0, n_slots - 1)]
    row_tok = jnp.where(k_in_e < jnp.repeat(counts[blk_e], MOE_TB), src // 2, 0)
    xbuf = h[row_tok]
    ybuf = _moe_experts(xbuf, blk_e, n_used, wg, wu, wd)
    return x1 + wts[:, 0:1] * ybuf[dest[:, 0]] + wts[:, 1:2] * ybuf[dest[:, 1]]


def kernel(x_prompt, x_sample, cache_cmp_kv, cache_slc_kv, cache_win_kv, state_wkv, state_shift, page_table, norm1, w_in, mu_shift, w0, w_decay_up, a0, w_iclr_up, w_gate_up, k_k, k_a, r_k, ln_x_w, ln_x_b, q_norm, kc_norm, ks_norm, kw_norm, cmp_pe_k, cmp_w1_k, cmp_b1_k, cmp_w2_k, cmp_pe_v, cmp_w1_v, cmp_b1_v, cmp_w2_v, w_branch_a, w_branch_b, w_out, norm2, w_route_group, b_route_group, w_route_expert, b_route_expert, w_exp_gate, w_exp_up, w_exp_down):
    assert norm1.shape[0] == 1, "single-layer trunk"
    Bp, Tp, _ = x_prompt.shape
    Bs, Ts, _ = x_sample.shape
    n_pool = cache_cmp_kv.shape[1]
    past = page_table.shape[1] * PAGE_SIZE
    n_buf = cache_win_kv.shape[2]
    kv5 = lambda a, b, t: a.reshape(1, b, t, 2, KV_HEADS, HEAD_DIM)

    w_pad = _pad_w_in(w_in[0])
    rw_p = (mu_shift[0], w0[0], w_decay_up[0], a0[0], w_iclr_up[0], w_gate_up[0], k_k[0], k_a[0],
            r_k[0].reshape(-1), ln_x_w[0], ln_x_b[0])
    wc, pea, peb, b1, w2b = _cmp_weights(cmp_pe_k[0], cmp_w1_k[0], cmp_b1_k[0], cmp_w2_k[0],
                                         cmp_pe_v[0], cmp_w1_v[0], cmp_b1_v[0], cmp_w2_v[0])
    merge_p = (w_branch_a[0], w_branch_b[0], w_out[0], norm2[0], w_route_group[0], b_route_group[0],
               w_route_expert[0], b_route_expert[0])
    moe_w = (w_exp_gate[0], w_exp_up[0], w_exp_down[0])

    xp = x_prompt.reshape(Bp * Tp, D_MODEL)
    zrw, q, kvc, kvs, kvw, ng, mg, kvs_g, kvw_g = _in_proj(xp, norm1[0], w_pad, q_norm[0], ks_norm[0], kw_norm[0], 512)
    zrw3 = zrw.reshape(Bp, Tp, RW_PAD)
    oa, wkv_p = _rwkv(zrw3, jnp.zeros((Bp, 1, RW_PAD), F32), jnp.zeros((Bp, N_HEADS, HEAD_DIM, HEAD_DIM), F32),
                      Tp, *rw_p)
    ab = _cmp_proj(kvc.reshape(Bp * Tp // CMP_STRIDE, CH_W), pea, peb, wc, min(256, Bp * Tp // CMP_STRIDE))
    kc, vc = _cmp_finish(ab.reshape(Bp, Tp // CMP_STRIDE, -1), b1, w2b, kc_norm[0])
    q3 = q.reshape(Bp, Tp, C_MIX)
    o_c, sel = _cmp_attn(q3, kc, vc, 0, (Tp - CMP_LEN) // CMP_STRIDE + 1, Tp // SEL_BLOCK, 128, True)
    ob = _nsa_prompt(q3, kvs_g.reshape(KV_HEADS, Bp, Tp, LANE), kvw_g.reshape(KV_HEADS, Bp, Tp, LANE), sel, o_c,
                     ng.reshape(Bp, Tp, LANE), 128)
    x1, h, route = _merge(xp, oa.reshape(Bp * Tp, C_MIX), ob.reshape(Bp * Tp, C_MIX), mg, *merge_p, 512)
    y_p = _moe(x1, h, route, *moe_w).reshape(Bp, Tp, D_MODEL)
    keep_p = min(WINDOW, Tp)

    xs = x_sample.reshape(Bs * Ts, D_MODEL)
    zrw_s, q_s, kvc_s, kvs_s, kvw_s, ng_s, mg_s, _, _ = _in_proj(xs, norm1[0], w_pad, q_norm[0], ks_norm[0],
                                                                 kw_norm[0], Bs * Ts)
    zrw_s3 = zrw_s.reshape(Bs, Ts, RW_PAD)
    oa_s, wkv_s = _rwkv(jnp.pad(zrw_s3, ((0, 0), (0, RW_CHUNK - Ts), (0, 0))),
                        jnp.pad(state_shift[0], ((0, 0), (0, RW_PAD - RW_IN)))[:, None], state_wkv[0], Ts, *rw_p)
    nc_s = (past + Ts - CMP_LEN) // CMP_STRIDE + 1
    assert (nc_s + CMP_LEN // CMP_STRIDE - 1) * CMP_STRIDE <= past, "compression blocks only cover cached rows"
    ab_s = _cmp_proj_paged(cache_cmp_kv[0].reshape(n_pool, PAGE_SIZE, KV_W), page_table, pea, peb, wc)
    kc_s, vc_s = _cmp_finish(ab_s, b1, w2b, kc_norm[0])
    rows8 = lambda a: jnp.pad(a.reshape(Bs, Ts, -1), ((0, 0), (0, DEC_ROWS - Ts), (0, 0)))
    q8 = rows8(q_s)
    oc_s, sel_s = _cmp_attn(q8, kc_s, vc_s, past, nc_s, -(-(past + Ts) // SEL_BLOCK), DEC_ROWS, False)
    assert past % SEL_BLOCK == 0 and Ts <= SEL_BLOCK, "the new rows share one selection block"
    n_st = page_table.shape[1] // CMP_PAGES
    sel4 = sel_s.reshape(Bs, DEC_ROWS, KV_HEADS, -1)
    sel_steps = sel4[..., :past // SEL_BLOCK].reshape(Bs, DEC_ROWS, KV_HEADS, n_st, -1)
    sel_steps = sel_steps.transpose(0, 3, 1, 2, 4).reshape(Bs, n_st, DEC_ROWS, -1)
    sel_new = jnp.pad(sel4[..., past // SEL_BLOCK], ((0, 0), (0, 0), (0, LANE - KV_HEADS)))
    os_s = _nsa_decode("sel", q8, cache_slc_kv[0].reshape(n_pool, PAGE_SIZE, KV_W), page_table, rows8(kvs_s),
                       (sel_steps, sel_new), CMP_PAGES, past, 0)
    win_pages = n_buf // PAGE_SIZE
    ob_s = _nsa_decode("win", q8, cache_win_kv[0].reshape(Bs * win_pages, PAGE_SIZE, KV_W),
                       jnp.arange(Bs * win_pages, dtype=jnp.int32).reshape(Bs, win_pages), rows8(kvw_s),
                       (oc_s, os_s, rows8(ng_s)), win_pages, past, past - n_buf)
    x1_s, h_s, route_s = _merge(xs, oa_s[:, :Ts].reshape(Bs * Ts, C_MIX), ob_s[:, :Ts].reshape(Bs * Ts, C_MIX), mg_s,
                                *merge_p, Bs * Ts)
    y_s = _moe(x1_s, h_s, route_s, *moe_w).reshape(Bs, Ts, D_MODEL)
    keep_s = min(WINDOW, n_buf + Ts)
    win_s = jnp.concatenate([cache_win_kv[0].reshape(Bs, n_buf, KV_W), kvw_s.reshape(Bs, Ts, KV_W)], axis=1)

    return (y_p, y_s,
            kv5(kvc, Bp, Tp), kv5(kvs, Bp, Tp), kv5(kvw.reshape(Bp, Tp, KV_W)[:, Tp - keep_p:], Bp, keep_p),
            wkv_p[None], zrw3[:, -1, :RW_IN][None],
            kv5(kvc_s, Bs, Ts), kv5(kvs_s, Bs, Ts), kv5(win_s[:, n_buf + Ts - keep_s:], Bs, keep_s),
            wkv_s[None], zrw_s3[:, -1, :RW_IN][None])
```

```python
import functools
import math

import jax
import jax.numpy as jnp
from jax import lax
from jax.experimental import pallas as pl
from jax.experimental.pallas import tpu as pltpu

F32 = jnp.float32
BF16 = jnp.bfloat16

D_MODEL = 1024
HEAD_DIM = 64
N_HEADS = 8
C_MIX = N_HEADS * HEAD_DIM
R_DECAY, R_ICLR, R_GATE = 32, 32, 96
RW_IN = 3 * C_MIX + R_DECAY + R_ICLR + R_GATE
KV_HEADS = 2
KV_GROUP = N_HEADS // KV_HEADS
KV_W = 2 * KV_HEADS * HEAD_DIM
CMP_LEN, CMP_STRIDE = 32, 16
SEL_BLOCK = 64
N_SEL = 16
WINDOW = 512
PAGE_SIZE = 128
N_GROUPS, EXPERTS_PER_GROUP = 4, 8
N_EXPERTS = N_GROUPS * EXPERTS_PER_GROUP
D_EXPERT = D_MODEL // 2
RMS_EPS = 1e-6
GN_EPS = 64e-5
NEG_INF = -1e30
FORCE_SCORE = 1e6

LANE = 128
VMEM_LIMIT = 56 * 1024 * 1024

RW_PAD = 1792
OFF_Q = RW_PAD
OFF_KVC = OFF_Q + C_MIX
OFF_KVS = OFF_KVC + KV_W
OFF_KVW = OFF_KVS + KV_W
OFF_NG = OFF_KVW + KV_W
OFF_MG = OFF_NG + LANE
N_IN_PAD = OFF_MG + 2 * D_MODEL
RW_TAIL = 3 * C_MIX


def _params(*sem):
    return pltpu.CompilerParams(dimension_semantics=sem, vmem_limit_bytes=VMEM_LIMIT)


def _dot(a, b):
    return jnp.dot(a, b, preferred_element_type=F32)


def _dot_nt(a, b):
    return lax.dot_general(a, b, (((1,), (1,)), ((), ())), preferred_element_type=F32)


def _dot_tn(a, b):
    return lax.dot_general(a, b, (((0,), (0,)), ((), ())), preferred_element_type=F32)


def _split2(x):
    hi = x.astype(BF16)
    lo = (x - hi.astype(F32)).astype(BF16)
    return hi, lo


def _split3(x):
    hi = x.astype(BF16)
    r1 = x - hi.astype(F32)
    mid = r1.astype(BF16)
    lo = (r1 - mid.astype(F32)).astype(BF16)
    return hi, mid, lo


def _gsum(y, g):
    hi, lo = _split2(y)
    return _dot(hi, g) + _dot(lo, g)


def _block_ones(n, blk):
    i = jnp.arange(n) // blk
    return (i[:, None] == i[None, :]).astype(BF16)


def _sigmoid(x):
    return 1.0 / (1.0 + jnp.exp(-x))


def _inproj_body(x_ref, n1_ref, w_ref, g512_ref, g128_ref, qn_ref, ksn_ref, kwn_ref,
                 zrw_ref, q_ref, kvc_ref, kvs_ref, kvw_ref, ng_ref, mg_ref, kvsg_ref, kvwg_ref):
    x = x_ref[...]
    ms = jnp.mean(x * x, axis=-1, keepdims=True)
    xn = (x * lax.rsqrt(ms + RMS_EPS) * n1_ref[...]).astype(BF16)

    def proj(a, b):
        return _dot(xn, w_ref[:, a:b])

    zrw_ref[...] = proj(0, RW_PAD)
    q = proj(OFF_Q, OFF_KVC)
    q_ref[...] = q * lax.rsqrt(_gsum(q * q, g512_ref[...]) * (1.0 / HEAD_DIM) + RMS_EPS) * qn_ref[...]
    kvc_ref[...] = proj(OFF_KVC, OFF_KVS)
    for off, nref, oref, gref in ((OFF_KVS, ksn_ref, kvs_ref, kvsg_ref), (OFF_KVW, kwn_ref, kvw_ref, kvwg_ref)):
        kv = proj(off, off + KV_W)
        k = kv[:, :LANE]
        kn = k * lax.rsqrt(_gsum(k * k, g128_ref[...]) * (1.0 / HEAD_DIM) + RMS_EPS) * nref[...]
        v = kv[:, LANE:]
        oref[:, :LANE] = kn
        oref[:, LANE:] = v
        for g in range(KV_HEADS):
            hs = slice(g * HEAD_DIM, (g + 1) * HEAD_DIM)
            gref[g] = jnp.concatenate([kn[:, hs], v[:, hs]], axis=1).astype(BF16)
    ng_ref[...] = _sigmoid(proj(OFF_NG, OFF_MG))
    mg_ref[...] = _sigmoid(proj(OFF_MG, N_IN_PAD))


def _in_proj(x2d, norm1, w_pad, q_norm, ks_norm, kw_norm, tm):
    n = x2d.shape[0]
    const = lambda shape: pl.BlockSpec(shape, lambda i: (0,) * len(shape))
    row = lambda w: pl.BlockSpec((tm, w), lambda i: (i, 0))
    widths = (RW_PAD, C_MIX, KV_W, KV_W, KV_W, LANE, 2 * D_MODEL)
    return pl.pallas_call(
        _inproj_body,
        grid=(n // tm,),
        in_specs=[row(D_MODEL), const((1, D_MODEL)), const((D_MODEL, N_IN_PAD)), const((C_MIX, C_MIX)),
                  const((LANE, LANE)), const((1, C_MIX)), const((1, LANE)), const((1, LANE))],
        out_specs=[row(w) for w in widths] + [pl.BlockSpec((KV_HEADS, tm, LANE), lambda i: (0, i, 0))] * 2,
        out_shape=[jax.ShapeDtypeStruct((n, w), F32) for w in widths]
                  + [jax.ShapeDtypeStruct((KV_HEADS, n, LANE), BF16)] * 2,
        compiler_params=_params("parallel"),
        name="in_proj",
    )(x2d, norm1.reshape(1, D_MODEL), w_pad, _block_ones(C_MIX, HEAD_DIM), _block_ones(LANE, HEAD_DIM),
      jnp.tile(q_norm, N_HEADS).reshape(1, C_MIX), jnp.tile(ks_norm, KV_HEADS).reshape(1, LANE),
      jnp.tile(kw_norm, KV_HEADS).reshape(1, LANE))


def _pad_w_in(w_in):
    d = w_in.shape[0]
    z = lambda n: jnp.zeros((d, n), w_in.dtype)
    o_q = RW_IN
    o_ng = o_q + C_MIX + 3 * KV_W
    o_mg = o_ng + 3 * N_HEADS
    return jnp.concatenate([w_in[:, :RW_IN], z(RW_PAD - RW_IN), w_in[:, o_q:o_ng], w_in[:, o_ng:o_mg],
                            z(LANE - 3 * N_HEADS), w_in[:, o_mg:]], axis=1).astype(BF16)


RW_CHUNK = 64
RW_HSTACK = 4
RW_ROWS = 4


def _rwkv_body(t_valid, z_ref, sp_ref, s0_ref, mu_ref, w0_ref, a0_ref, kk_ref, ka_ref, rk_ref, lnw_ref, lnb_ref,
               wd_ref, wi_ref, wg_ref, g512_ref, o_ref, s_ref, prev_scr):
    @pl.when(pl.program_id(1) == 0)
    def _():
        s_ref[...] = s0_ref[...]
        prev_scr[...] = sp_ref[...]

    prep = [_rwkv_prep(t_valid, bi, z_ref, mu_ref, w0_ref, a0_ref, kk_ref, ka_ref, rk_ref, wd_ref, wi_ref, wg_ref,
                       g512_ref, prev_scr) for bi in range(z_ref.shape[0])]
    ys = _rwkv_chains(prep, s_ref)
    g512 = g512_ref[...]
    for bi, (p, y) in enumerate(zip(prep, ys)):
        yc = y - _gsum(y, g512) * (1.0 / HEAD_DIM)
        var = _gsum(yc * yc, g512) * (1.0 / HEAD_DIM)
        yn = yc * lax.rsqrt(var + GN_EPS) * lnw_ref[...] + lnb_ref[...]
        o_ref[bi] = (yn + _gsum(p["rkk"], g512) * p["v"]) * p["g"]


def _rwkv_prep(t_valid, bi, z_ref, mu_ref, w0_ref, a0_ref, kk_ref, ka_ref, rk_ref, wd_ref, wi_ref, wg_ref,
               g512_ref, prev_scr):
    C = RW_CHUNK
    c = pl.program_id(1)
    z = z_ref[bi]
    row = lax.broadcasted_iota(jnp.int32, (C, 1), 0)
    z_prev = jnp.where(row == 0, prev_scr[bi], pltpu.roll(z, 1, axis=0))
    prev_scr[bi] = z[C - 1:C]
    zm = z + (z_prev - z) * mu_ref[...]
    r = zm[:, 0:C_MIX]
    k = zm[:, C_MIX:2 * C_MIX]
    v = zm[:, 2 * C_MIX:3 * C_MIX]
    tail = zm[:, RW_TAIL:RW_PAD]
    w_lora = _dot(jnp.tanh(tail).astype(BF16), wd_ref[...])
    a_lora = _dot(tail.astype(BF16), wi_ref[...])
    g = _dot(_sigmoid(tail).astype(BF16), wg_ref[...])
    u = -(w0_ref[...] + w_lora)
    softplus = jnp.maximum(u, 0.0) + jnp.log(1.0 + jnp.exp(-jnp.abs(u)))
    w_log = -softplus - 0.5
    valid = (c * C + row) < t_valid
    ld = jnp.where(valid, -jnp.exp(w_log), 0.0)
    a = _sigmoid(a0_ref[...] + a_lora)
    kk = k * kk_ref[...]
    kk = kk / jnp.maximum(jnp.sqrt(_gsum(kk * kk, g512_ref[...])), 1e-12)
    k2 = k * (1.0 + (a - 1.0) * ka_ref[...])

    ci = lax.broadcasted_iota(jnp.int32, (C, C), 0)
    cj = lax.broadcasted_iota(jnp.int32, (C, C), 1)
    tri = (ci >= cj).astype(BF16)
    h1, h2, h3 = _split3(ld)
    cl = _dot(tri, h1) + _dot(tri, h2) + _dot(tri, h3)
    p_in = jnp.exp(cl)
    p_inv = jnp.exp(-cl)
    r_t = r * p_in
    a_t = -kk * jnp.exp(cl - ld)
    b_t = jnp.where(valid, kk * a * p_inv, 0.0)
    k_t = jnp.where(valid, k2 * p_inv, 0.0)
    p_end = p_in[C - 1:C]

    return dict(a_t=a_t, r_t=r_t, b_t=b_t, k_t=k_t, v=v, p_end=p_end, rkk=r * k2 * rk_ref[...], g=g)


def _rwkv_chains(prep, s_ref):
    C = RW_CHUNK
    HS = RW_HSTACK
    R = HS * C
    ri = lax.broadcasted_iota(jnp.int32, (R, R), 0)
    rj = lax.broadcasted_iota(jnp.int32, (R, R), 1)
    same = (ri // C) == (rj // C)
    lower = same & (ri > rj)
    lower_eq = same & (ri >= rj)
    eye = (ri == rj).astype(F32)
    chains = [(bi, hg) for bi in range(len(prep)) for hg in range(N_HEADS // HS)]
    heads = lambda hg: [hg * HS + m for m in range(HS)]
    stack = lambda x, hg: jnp.concatenate([x[:, h * HEAD_DIM:(h + 1) * HEAD_DIM] for h in heads(hg)],
                                          axis=0).astype(BF16)
    A, Rr, Bm, Km, V = ([stack(prep[bi][name], hg) for bi, hg in chains]
                        for name in ("a_t", "r_t", "b_t", "k_t", "v"))
    S4 = [_dot_nt(jnp.concatenate([a, r], axis=0), jnp.concatenate([b, k], axis=0))
          for a, r, b, k in zip(A, Rr, Bm, Km)]
    L = [jnp.where(lower, s[:R, :R], 0.0) for s in S4]
    Lak = [jnp.where(lower, s[:R, R:], 0.0).astype(BF16) for s in S4]
    Mr = [jnp.concatenate([jnp.where(lower_eq, s[R:, :R], 0.0), jnp.where(lower_eq, s[R:, R:], 0.0)],
                          axis=1).astype(BF16) for s in S4]
    s0 = [[s_ref[bi, h] for h in heads(hg)] for bi, hg in chains]
    on_state = [[_dot_nt(jnp.concatenate([a[m * C:(m + 1) * C], r[m * C:(m + 1) * C]], axis=0),
                         s0c[m].astype(BF16)) for m in range(HS)]
                for a, r, s0c in zip(A, Rr, s0)]
    rhs = [jnp.concatenate([o[:C] for o in os], axis=0) + _dot(lak, v) for os, lak, v in zip(on_state, Lak, V)]
    X = [eye + l for l in L]
    Lp = L
    for _ in range(int(math.log2(C)) - 1):
        Lpb = [lp.astype(BF16) for lp in Lp]
        Lp = [_dot(lp, lp) for lp in Lpb]
        X = [x + _dot(x.astype(BF16), lp.astype(BF16)) for x, lp in zip(X, Lp)]
    U = [_dot(x.astype(BF16), r.astype(BF16)).astype(BF16) for x, r in zip(X, rhs)]
    Y = [jnp.concatenate([o[C:] for o in os], axis=0) + _dot(mr, jnp.concatenate([u, v], axis=0))
         for os, mr, u, v in zip(on_state, Mr, U, V)]
    for ci, (bi, hg) in enumerate(chains):
        for m, h in enumerate(heads(hg)):
            rs = slice(m * C, (m + 1) * C)
            upd = _dot_tn(jnp.concatenate([U[ci][rs], V[ci][rs]], axis=0),
                          jnp.concatenate([Bm[ci][rs], Km[ci][rs]], axis=0))
            s_ref[bi, h] = (s0[ci][m] + upd) * prep[bi]["p_end"][:, h * HEAD_DIM:(h + 1) * HEAD_DIM]
    return [jnp.concatenate([Y[ci][m * C:(m + 1) * C] for ci, (cb, _) in enumerate(chains) if cb == bi
                             for m in range(HS)], axis=1) for bi in range(len(prep))]


def _rwkv(z_rw, shift_prev, s0, t_valid, mu, w0, wd, a0, wi, wg, k_k, k_a, r_k, ln_w, ln_b):
    B, T, _ = z_rw.shape
    C = RW_CHUNK
    const = lambda shape: pl.BlockSpec(shape, lambda b, c: (0,) * len(shape))
    vec = lambda p: p.reshape(1, C_MIX)
    pad_rows = lambda w, off: jnp.zeros((RW_PAD - RW_TAIL, C_MIX), F32).at[off:off + w.shape[0]].set(w).astype(BF16)
    nb = math.gcd(B, RW_ROWS)
    state_spec = pl.BlockSpec((nb, N_HEADS, HEAD_DIM, HEAD_DIM), lambda b, c: (b, 0, 0, 0))
    return pl.pallas_call(
        functools.partial(_rwkv_body, t_valid),
        grid=(B // nb, T // C),
        in_specs=[pl.BlockSpec((nb, C, RW_PAD), lambda b, c: (b, c, 0)),
                  pl.BlockSpec((nb, 1, RW_PAD), lambda b, c: (b, 0, 0)),
                  state_spec, const((1, RW_PAD))] + [const((1, C_MIX))] * 7
                 + [const((RW_PAD - RW_TAIL, C_MIX))] * 3 + [const((C_MIX, C_MIX))],
        out_specs=[pl.BlockSpec((nb, C, C_MIX), lambda b, c: (b, c, 0)), state_spec],
        out_shape=[jax.ShapeDtypeStruct((B, T, C_MIX), F32),
                   jax.ShapeDtypeStruct((B, N_HEADS, HEAD_DIM, HEAD_DIM), F32)],
        scratch_shapes=[pltpu.VMEM((nb, 1, RW_PAD), F32)],
        compiler_params=_params("parallel", "arbitrary"),
        name="rwkv7",
    )(z_rw, shift_prev, s0, jnp.pad(mu, (0, RW_PAD - RW_IN)).reshape(1, RW_PAD), vec(w0), vec(a0), vec(k_k),
      vec(k_a), vec(r_k), vec(ln_w), vec(ln_b), pad_rows(wd, 0), pad_rows(wi, R_DECAY),
      pad_rows(wg, R_DECAY + R_ICLR), _block_ones(C_MIX, HEAD_DIM))


CH_W = CMP_STRIDE * KV_W
N_SLOT = 2 * KV_HEADS


def _cmp_weights(pe_k, w1_k, b1_k, w2_k, pe_v, w1_v, b1_v, w2_v):
    eye = jnp.eye(N_SLOT, dtype=F32)
    w1 = jnp.stack([w1_k, w1_k, w1_v, w1_v])
    pe = jnp.stack([pe_k, pe_k, pe_v, pe_v])
    halves = []
    pes = []
    for r in range(CMP_LEN // CMP_STRIDE):
        ls = slice(r * CMP_STRIDE, (r + 1) * CMP_STRIDE)
        halves.append(jnp.einsum('sldf,st->lsdtf', w1[:, ls], eye).reshape(CH_W, N_SLOT * HEAD_DIM))
        pes.append(jnp.transpose(pe[:, ls], (1, 0, 2)).reshape(1, CH_W))
    wc = jnp.concatenate(halves, axis=1).astype(BF16)
    w2 = jnp.stack([w2_k, w2_k, w2_v, w2_v])
    w2b = jnp.einsum('sfd,st->sftd', w2, eye).reshape(N_SLOT * HEAD_DIM, N_SLOT * HEAD_DIM).astype(BF16)
    b1 = jnp.concatenate([b1_k, b1_k, b1_v, b1_v]).reshape(1, N_SLOT * HEAD_DIM)
    return wc, pes[0], pes[1], b1, w2b


def _cmp_proj_body(ch_ref, pea_ref, peb_ref, wc_ref, ab_ref):
    ch = ch_ref[...]
    w = N_SLOT * HEAD_DIM
    ab_ref[:, :w] = _dot((ch + pea_ref[...]).astype(BF16), wc_ref[:, :w])
    ab_ref[:, w:] = _dot((ch + peb_ref[...]).astype(BF16), wc_ref[:, w:])


def _cmp_proj(chunks, pea, peb, wc, tr):
    n = chunks.shape[0]
    const = lambda shape: pl.BlockSpec(shape, lambda i: (0,) * len(shape))
    return pl.pallas_call(
        _cmp_proj_body,
        grid=(n // tr,),
        in_specs=[pl.BlockSpec((tr, CH_W), lambda i: (i, 0)), const((1, CH_W)), const((1, CH_W)),
                  const((CH_W, 2 * N_SLOT * HEAD_DIM))],
        out_specs=pl.BlockSpec((tr, 2 * N_SLOT * HEAD_DIM), lambda i: (i, 0)),
        out_shape=jax.ShapeDtypeStruct((n, 2 * N_SLOT * HEAD_DIM), F32),
        compiler_params=_params("parallel"),
        name="cmp_proj",
    )(chunks, pea, peb, wc)


CMP_PAGES = 16


def _paged_fetch(pt_ref, pool_ref, buf, sem, pages, page_rows):
    b = pl.program_id(0)
    st = pl.program_id(1)
    n_st = pl.num_programs(1)
    step = b * n_st + st
    slot = step % 2
    n_split, lanes = buf.shape[1], buf.shape[3]

    def copies(bb, stt, sl):
        out = []
        for i in range(pages):
            page = pt_ref[bb, stt * pages + i]
            for j in range(n_split):
                out.append(pltpu.make_async_copy(pool_ref.at[page, :, pl.ds(j * lanes, lanes)],
                                                 buf.at[sl, j, pl.ds(i * page_rows, page_rows)], sem.at[sl]))
        return out

    @pl.when(step == 0)
    def _():
        for c in copies(0, 0, 0):
            c.start()

    @pl.when(step + 1 < pl.num_programs(0) * n_st)
    def _():
        wrap = st + 1 == n_st
        for c in copies(jnp.where(wrap, b + 1, b), jnp.where(wrap, 0, st + 1), 1 - slot):
            c.start()

    for c in copies(b, st, slot):
        c.wait()
    return slot


def _cmp_proj_paged_body(pt_ref, cache_ref, pe_ref, wl_ref, ab_ref, buf, sem):
    slot = _paged_fetch(pt_ref, cache_ref, buf, sem, CMP_PAGES, PAGE_SIZE)
    n_chunks = CMP_PAGES * PAGE_SIZE // CMP_STRIDE
    w = N_SLOT * HEAD_DIM
    acc = [jnp.zeros((n_chunks, w), F32) for _ in range(CMP_LEN // CMP_STRIDE)]
    for l in range(CMP_STRIDE):
        x = jnp.concatenate([buf[slot, j, pl.ds(l, n_chunks, stride=CMP_STRIDE), :] for j in range(buf.shape[1])],
                            axis=1)
        for r in range(CMP_LEN // CMP_STRIDE):
            acc[r] = acc[r] + _dot((x + pe_ref[r, l]).astype(BF16), wl_ref[l, :, r * w:(r + 1) * w])
    for r in range(CMP_LEN // CMP_STRIDE):
        ab_ref[0, :, r * w:(r + 1) * w] = acc[r]


def _cmp_proj_paged(cache, page_table, pea, peb, wc):
    B, n_pages = page_table.shape
    rows = PAGE_SIZE // CMP_STRIDE
    w = N_SLOT * HEAD_DIM
    const = lambda shape: pl.BlockSpec(shape, lambda b, g, pt: (0,) * len(shape))
    pe = jnp.stack([pea, peb]).reshape(CMP_LEN // CMP_STRIDE, CMP_STRIDE, 1, w)
    wl = wc.reshape(CMP_STRIDE, w, 2 * w)
    return pl.pallas_call(
        _cmp_proj_paged_body,
        grid_spec=pltpu.PrefetchScalarGridSpec(
            num_scalar_prefetch=1,
            grid=(B, n_pages // CMP_PAGES),
            in_specs=[pl.BlockSpec(memory_space=pl.ANY), const(pe.shape), const(wl.shape)],
            out_specs=pl.BlockSpec((1, CMP_PAGES * rows, 2 * w), lambda b, g, pt: (b, g, 0)),
            scratch_shapes=[pltpu.VMEM((2, KV_W // LANE, CMP_PAGES * PAGE_SIZE, LANE), F32),
                            pltpu.SemaphoreType.DMA((2,))],
        ),
        out_shape=jax.ShapeDtypeStruct((B, n_pages * rows, 2 * w), F32),
        compiler_params=_params("arbitrary", "arbitrary"),
        name="cmp_proj_paged",
    )(page_table, cache, pe, wl)


def _cmp_finish_body(ab_ref, b1_ref, w2_ref, kcn_ref, g128_ref, kc_ref, vc_ref):
    ab = ab_ref[0]
    n = ab.shape[0]
    w = N_SLOT * HEAD_DIM
    pre = ab[:, :w] + pltpu.roll(ab[:, w:], n - 1, axis=0) + b1_ref[...]
    hid = pre * _sigmoid(pre)
    out = _dot(hid.astype(BF16), w2_ref[...])
    k = out[:, :LANE]
    kc_ref[0] = k * lax.rsqrt(_gsum(k * k, g128_ref[...]) * (1.0 / HEAD_DIM) + RMS_EPS) * kcn_ref[...]
    vc_ref[0] = out[:, LANE:]


def _cmp_finish(ab, b1, w2b, kc_norm):
    B, n, _ = ab.shape
    const = lambda shape: pl.BlockSpec(shape, lambda b: (0,) * len(shape))
    w = N_SLOT * HEAD_DIM
    return pl.pallas_call(
        _cmp_finish_body,
        grid=(B,),
        in_specs=[pl.BlockSpec((1, n, 2 * w), lambda b: (b, 0, 0)), const((1, w)), const((w, w)), const((1, LANE)),
                  const((LANE, LANE))],
        out_specs=[pl.BlockSpec((1, n, LANE), lambda b: (b, 0, 0))] * 2,
        out_shape=[jax.ShapeDtypeStruct((B, n, LANE), F32)] * 2,
        compiler_params=_params("parallel"),
        name="cmp_finish",
    )(ab, b1, w2b, jnp.tile(kc_norm, KV_HEADS).reshape(1, LANE), _block_ones(LANE, HEAD_DIM))


def _cmp_attn_body(pos0, nc, nb, nbp, blocks_on_rows, q_ref, kc_ref, vc_ref, ovl_ref, o_ref, sel_ref):
    tq = q_ref.shape[1]
    ncp = kc_ref.shape[1]
    q = q_ref[0] * (HEAD_DIM ** -0.5)
    t0 = pos0 + pl.program_id(1) * tq
    t = t0 + lax.broadcasted_iota(jnp.int32, (tq, 1), 0)
    cidx = lax.broadcasted_iota(jnp.int32, (1, ncp), 1)
    cmask = (cidx * CMP_STRIDE + (CMP_LEN - 1) <= t) & (cidx < nc)
    if blocks_on_rows:
        tt = t0 + lax.broadcasted_iota(jnp.int32, (1, tq), 1)
        j = lax.broadcasted_iota(jnp.int32, (nbp, 1), 0)
    else:
        tt = t
        j = lax.broadcasted_iota(jnp.int32, (1, nbp), 1)
    cur = tt // SEL_BLOCK
    valid = (j * SEL_BLOCK <= tt) & (j < nb)
    forced = (j == 0) | (j == cur) | (j == cur - 1)
    for g in range(KV_HEADS):
        kcg = kc_ref[0, :, g * HEAD_DIM:(g + 1) * HEAD_DIM].astype(BF16)
        vcg = vc_ref[0, :, g * HEAD_DIM:(g + 1) * HEAD_DIM].astype(BF16)
        s = _dot_nt(_stack_heads(q, g).astype(BF16), kcg).reshape(KV_GROUP, tq, ncp)
        s = jnp.where(cmask[None], s, NEG_INF)
        e = jnp.where(cmask[None], jnp.exp(s - jnp.max(s, axis=-1, keepdims=True)), 0.0)
        p = e / jnp.maximum(jnp.sum(e, axis=-1, keepdims=True), 1e-30)
        o = _dot(p.reshape(KV_GROUP * tq, ncp).astype(BF16), vcg)
        for m in range(KV_GROUP):
            h = g * KV_GROUP + m
            o_ref[0, :, h * HEAD_DIM:(h + 1) * HEAD_DIM] = o[m * tq:(m + 1) * tq]
        hi, lo = _split2(jnp.sum(p, axis=0))
        if blocks_on_rows:
            imp = _dot_nt(ovl_ref[...], hi) + _dot_nt(ovl_ref[...], lo)
        else:
            imp = _dot(hi, ovl_ref[...]) + _dot(lo, ovl_ref[...])
        score = jnp.where(valid, jnp.where(forced, FORCE_SCORE, imp), NEG_INF)
        cnt = jnp.zeros(score.shape, jnp.int32)
        for jp in range(nb):
            cj = score[jp:jp + 1, :] if blocks_on_rows else score[:, jp:jp + 1]
            cnt = cnt + jnp.where(j > jp, jnp.where(cj >= score, 1, 0), jnp.where(cj > score, 1, 0))
        picked = (cnt < N_SEL).astype(F32)
        if blocks_on_rows:
            sel_ref[0, g] = picked
        else:
            sel_ref[0, :, g * nbp:(g + 1) * nbp] = picked


def _cmp_attn(q, kc, vc, pos0, nc, nb, tq, blocks_on_rows):
    B, Tq, _ = q.shape
    ncp = kc.shape[1]
    nbp = -(-nb // SEL_BLOCK) * SEL_BLOCK
    c0 = jnp.arange(ncp)[:, None] * CMP_STRIDE
    jj = jnp.arange(nbp)[None, :]
    ovl = ((c0 < (jj + 1) * SEL_BLOCK) & (c0 + CMP_LEN > jj * SEL_BLOCK) & (jnp.arange(ncp)[:, None] < nc)
           & (jj < nb)).astype(BF16)
    if blocks_on_rows:
        ovl = ovl.T
        sel_spec = pl.BlockSpec((1, KV_HEADS, nbp, tq), lambda b, i: (b, 0, 0, i))
        sel_shape = (B, KV_HEADS, nbp, Tq)
    else:
        sel_spec = pl.BlockSpec((1, tq, KV_HEADS * nbp), lambda b, i: (b, i, 0))
        sel_shape = (B, Tq, KV_HEADS * nbp)
    return pl.pallas_call(
        functools.partial(_cmp_attn_body, pos0, nc, nb, nbp, blocks_on_rows),
        grid=(B, Tq // tq),
        in_specs=[pl.BlockSpec((1, tq, C_MIX), lambda b, i: (b, i, 0)),
                  pl.BlockSpec((1, ncp, LANE), lambda b, i: (b, 0, 0)),
                  pl.BlockSpec((1, ncp, LANE), lambda b, i: (b, 0, 0)),
                  pl.BlockSpec(ovl.shape, lambda b, i: (0, 0))],
        out_specs=[pl.BlockSpec((1, tq, C_MIX), lambda b, i: (b, i, 0)), sel_spec],
        out_shape=[jax.ShapeDtypeStruct((B, Tq, C_MIX), F32), jax.ShapeDtypeStruct(sel_shape, F32)],
        compiler_params=_params("parallel", "parallel"),
        name="cmp_attn",
    )(q, kc, vc, ovl)


def _softmax_step(carry, s, mask, vb):
    m_, l_, acc = carry
    s = jnp.where(mask, s, NEG_INF)
    m_new = jnp.maximum(m_, jnp.max(s, axis=-1, keepdims=True))
    alpha = jnp.exp(m_ - m_new)
    p = jnp.where(mask, jnp.exp(s - m_new), 0.0)
    return m_new, alpha * l_ + jnp.sum(p, axis=-1, keepdims=True), alpha * acc + _dot(p.astype(BF16), vb)


def _softmax_init(rows):
    return jnp.full((rows, 1), NEG_INF, F32), jnp.zeros((rows, 1), F32), jnp.zeros((rows, HEAD_DIM), F32)


def _stack_heads(x, g):
    return jnp.concatenate([x[:, (g * KV_GROUP + m) * HEAD_DIM:(g * KV_GROUP + m + 1) * HEAD_DIM]
                            for m in range(KV_GROUP)], axis=0)


def _gate_mix(ng, o_c, o_s, o_w, g, tq, o_ref, os_stacked=True):
    for m in range(KV_GROUP):
        h = g * KV_GROUP + m
        sl = slice(h * HEAD_DIM, (h + 1) * HEAD_DIM)
        rs = slice(m * tq, (m + 1) * tq)
        o_ref[0, :, sl] = (ng[:, h:h + 1] * o_c[:, sl]
                           + ng[:, N_HEADS + h:N_HEADS + h + 1] * (o_s[rs] if os_stacked else o_s[:, sl])
                           + ng[:, 2 * N_HEADS + h:2 * N_HEADS + h + 1] * o_w[rs])


SEL_TK = 512


def _flash_step(carry, qg, kv, bias, tq):
    m_, l_, acc = carry
    tk = kv.shape[0]
    s = _dot_nt(qg, kv).reshape(KV_GROUP, tq, tk) + bias[None]
    m_new = jnp.maximum(m_, jnp.max(s, axis=-1, keepdims=True))
    alpha = jnp.exp(m_ - m_new)
    p = jnp.exp(s - m_new)
    l_new = alpha * l_ + jnp.sum(p, axis=-1, keepdims=True)
    pv = _dot(p.reshape(KV_GROUP * tq, tk).astype(BF16), kv).reshape(KV_GROUP, tq, LANE)
    return m_new, l_new, alpha * acc + pv


def _flash_init(tq):
    return (jnp.full((KV_GROUP, tq, 1), NEG_INF, F32), jnp.zeros((KV_GROUP, tq, 1), F32),
            jnp.zeros((KV_GROUP, tq, LANE), F32))


def _nsa_prompt_body(q_ref, kvs_ref, kvw_ref, sel_ref, oc_ref, ng_ref, o_ref):
    tq = q_ref.shape[1]
    nbp = sel_ref.shape[2]
    q0 = pl.program_id(1) * tq
    q = q_ref[0] * (HEAD_DIM ** -0.5)
    t = q0 + lax.broadcasted_iota(jnp.int32, (tq, 1), 0)
    blk = lax.broadcasted_iota(jnp.int32, (nbp, 1), 0)
    zpad = jnp.zeros((tq, LANE - HEAD_DIM), F32)
    qgs = []
    sels = []
    for g in range(KV_HEADS):
        heads = [jnp.concatenate([q[:, (g * KV_GROUP + m) * HEAD_DIM:(g * KV_GROUP + m + 1) * HEAD_DIM], zpad], axis=1)
                 for m in range(KV_GROUP)]
        qgs.append(jnp.concatenate(heads, axis=0).astype(BF16))
        sels.append(sel_ref[0, g].astype(BF16))

    def sel_step(kt, carry):
        k0 = pl.multiple_of(kt * SEL_TK, SEL_TK)
        kpos = k0 + lax.broadcasted_iota(jnp.int32, (1, SEL_TK), 1)
        expand = (kpos // SEL_BLOCK == blk).astype(BF16)
        causal = kpos <= t
        out = []
        for g in range(KV_HEADS):
            bias = jnp.where((_dot_tn(sels[g], expand) > 0.5) & causal, 0.0, NEG_INF)
            out.append(_flash_step(carry[g], qgs[g], kvs_ref[g, 0, pl.ds(k0, SEL_TK), :], bias, tq))
        return tuple(out)

    res_s = lax.fori_loop(0, (q0 + tq + SEL_TK - 1) // SEL_TK, sel_step, (_flash_init(tq),) * KV_HEADS)

    span = WINDOW + tq
    w0 = pl.multiple_of(jnp.maximum(q0 - WINDOW, 0), tq)
    wpos = w0 + lax.broadcasted_iota(jnp.int32, (1, span), 1)
    wbias = jnp.where((wpos <= t) & (wpos > t - WINDOW), 0.0, NEG_INF)
    res_w = tuple(_flash_step(_flash_init(tq), qgs[g], kvw_ref[g, 0, pl.ds(w0, span), :], wbias, tq)
                  for g in range(KV_HEADS))

    ng = ng_ref[0]
    oc = oc_ref[0]
    for g in range(KV_HEADS):
        o_s = res_s[g][2][:, :, HEAD_DIM:] / jnp.maximum(res_s[g][1], 1e-30)
        o_w = res_w[g][2][:, :, HEAD_DIM:] / jnp.maximum(res_w[g][1], 1e-30)
        for m in range(KV_GROUP):
            h = g * KV_GROUP + m
            sl = slice(h * HEAD_DIM, (h + 1) * HEAD_DIM)
            o_ref[0, :, sl] = (ng[:, h:h + 1] * oc[:, sl] + ng[:, N_HEADS + h:N_HEADS + h + 1] * o_s[m]
                               + ng[:, 2 * N_HEADS + h:2 * N_HEADS + h + 1] * o_w[m])


def _nsa_prompt(q, kvs_g, kvw_g, sel, o_c, ng, tq):
    B, T, _ = q.shape
    tile = lambda w: pl.BlockSpec((1, tq, w), lambda b, i: (b, i, 0))
    whole = pl.BlockSpec((KV_HEADS, 1, T, LANE), lambda b, i: (0, b, 0, 0))
    return pl.pallas_call(
        _nsa_prompt_body,
        grid=(B, T // tq),
        in_specs=[tile(C_MIX), whole, whole, pl.BlockSpec((1, KV_HEADS, sel.shape[2], tq), lambda b, i: (b, 0, 0, i)),
                  tile(C_MIX), tile(LANE)],
        out_specs=tile(C_MIX),
        out_shape=jax.ShapeDtypeStruct((B, T, C_MIX), F32),
        compiler_params=_params("parallel", "arbitrary"),
        name="nsa_prompt",
    )(q, kvs_g, kvw_g, sel, o_c, ng)


DEC_ROWS = 8


def _nsa_decode_body(mode, pages, pos_q0, pos_k0, pt_ref, q_ref, pool_ref, new_ref, *rest):
    if mode == "sel":
        selst_ref, selnew_ref, o_ref, buf, sem, m_scr, l_scr, acc_scr = rest
    else:
        oc_ref, os_ref, ng_ref, o_ref, buf, sem, m_scr, l_scr, acc_scr = rest
    st = pl.program_id(1)
    tq = DEC_ROWS
    tk = pages * PAGE_SIZE
    bps = tk // SEL_BLOCK

    slot = _paged_fetch(pt_ref, pool_ref, buf, sem, pages, PAGE_SIZE)

    @pl.when(st == 0)
    def _():
        m_scr[...] = jnp.full(m_scr.shape, NEG_INF, F32)
        l_scr[...] = jnp.zeros(l_scr.shape, F32)
        acc_scr[...] = jnp.zeros(acc_scr.shape, F32)

    q = q_ref[0] * (HEAD_DIM ** -0.5)
    t = pos_q0 + lax.broadcasted_iota(jnp.int32, (tq, 1), 0)

    def update(kv, kpos, picked):
        for g in range(KV_HEADS):
            mask = (kpos <= t) & (picked(g) if mode == "sel" else (kpos > t - WINDOW))
            mask = jnp.concatenate([mask] * KV_GROUP, axis=0)
            qg = _stack_heads(q, g).astype(BF16)
            carry = (m_scr[g], l_scr[g], acc_scr[g])
            k = kv[:, g * HEAD_DIM:(g + 1) * HEAD_DIM].astype(BF16)
            v = kv[:, LANE + g * HEAD_DIM:LANE + (g + 1) * HEAD_DIM].astype(BF16)
            m_scr[g], l_scr[g], acc_scr[g] = _softmax_step(carry, _dot_nt(qg, k), mask, v)

    kidx = lax.broadcasted_iota(jnp.int32, (1, tk), 1)
    expand = (kidx // SEL_BLOCK == lax.broadcasted_iota(jnp.int32, (bps, 1), 0)).astype(BF16)
    update(buf[slot, 0], pos_k0 + st * tk + kidx,
           lambda g: _dot(selst_ref[0, 0, :, g * bps:(g + 1) * bps].astype(BF16), expand) > 0.5)

    @pl.when(st == pl.num_programs(1) - 1)
    def _():
        update(new_ref[0], pos_q0 + lax.broadcasted_iota(jnp.int32, (1, tq), 1),
               lambda g: selnew_ref[0, :, g:g + 1] > 0.5)
        for g in range(KV_HEADS):
            o = acc_scr[g] / jnp.maximum(l_scr[g], 1e-30)
            if mode == "sel":
                for m in range(KV_GROUP):
                    h = g * KV_GROUP + m
                    o_ref[0, :, h * HEAD_DIM:(h + 1) * HEAD_DIM] = o[m * tq:(m + 1) * tq]
            else:
                _gate_mix(ng_ref[0], oc_ref[0], os_ref[0], o, g, tq, o_ref, os_stacked=False)


def _nsa_decode(mode, q, pool, page_table, kv_new, extras, pages, pos_q0, pos_k0):
    B, n_pages = page_table.shape
    rowblk = lambda a: pl.BlockSpec((1,) + a.shape[1:], lambda b, s, pt: (b,) + (0,) * (a.ndim - 1))
    stepblk = lambda a: pl.BlockSpec((1, 1) + a.shape[2:], lambda b, s, pt: (b, s, 0, 0))
    rows = KV_GROUP * DEC_ROWS
    return pl.pallas_call(
        functools.partial(_nsa_decode_body, mode, pages, pos_q0, pos_k0),
        grid_spec=pltpu.PrefetchScalarGridSpec(
            num_scalar_prefetch=1,
            grid=(B, n_pages // pages),
            in_specs=[rowblk(q), pl.BlockSpec(memory_space=pl.ANY), rowblk(kv_new)]
                     + [stepblk(e) if e.ndim == 4 else rowblk(e) for e in extras],
            out_specs=pl.BlockSpec((1, DEC_ROWS, C_MIX), lambda b, s, pt: (b, 0, 0)),
            scratch_shapes=[pltpu.VMEM((2, 1, pages * PAGE_SIZE, KV_W), F32), pltpu.SemaphoreType.DMA((2,)),
                            pltpu.VMEM((KV_HEADS, rows, 1), F32), pltpu.VMEM((KV_HEADS, rows, 1), F32),
                            pltpu.VMEM((KV_HEADS, rows, HEAD_DIM), F32)],
        ),
        out_shape=jax.ShapeDtypeStruct((B, DEC_ROWS, C_MIX), F32),
        compiler_params=_params("arbitrary", "arbitrary"),
        name="nsa_decode_" + mode,
    )(page_table, q, pool, kv_new, *extras)


ROUTE_W = LANE


def _merge_body(x_ref, oa_ref, ob_ref, mg_ref, wa_ref, wb_ref, wo_ref, n2_ref, wrh_ref, wrl_ref, br_ref,
                x1_ref, h_ref, route_ref):
    mg = mg_ref[...]
    merged = (mg[:, :D_MODEL] * _dot(oa_ref[...].astype(BF16), wa_ref[...])
              + mg[:, D_MODEL:] * _dot(ob_ref[...].astype(BF16), wb_ref[...]))
    x1 = x_ref[...] + _dot(merged.astype(BF16), wo_ref[...])
    x1_ref[...] = x1
    h = x1 * lax.rsqrt(jnp.mean(x1 * x1, axis=-1, keepdims=True) + RMS_EPS) * n2_ref[...]
    h_ref[...] = h.astype(BF16)
    hh, hl = _split2(h)
    logits = _dot(hh, wrh_ref[...]) + _dot(hl, wrh_ref[...]) + _dot(hh, wrl_ref[...]) + br_ref[...]
    lane = lax.broadcasted_iota(jnp.int32, (1, ROUTE_W), 1)
    first = lambda hit: jnp.min(jnp.where(hit, lane, ROUTE_W), axis=-1, keepdims=True)
    is_g = lane < N_GROUPS
    gl = jnp.where(is_g, logits, NEG_INF)
    gmax = jnp.max(gl, axis=-1, keepdims=True)
    g_sel = first(gl == gmax)
    g_w = 1.0 / jnp.sum(jnp.where(is_g, jnp.exp(gl - gmax), 0.0), axis=-1, keepdims=True)
    in_grp = (lane >= N_GROUPS) & (lane < N_GROUPS + N_EXPERTS) & (((lane - N_GROUPS) >> 3) == g_sel)
    el = jnp.where(in_grp, logits, NEG_INF)
    v1 = jnp.max(el, axis=-1, keepdims=True)
    i1 = first(el == v1)
    el2 = jnp.where(lane == i1, NEG_INF, el)
    v2 = jnp.max(el2, axis=-1, keepdims=True)
    i2 = first(el2 == v2)
    d = jnp.exp(v2 - v1)
    w1 = g_w / (1.0 + d)
    w2 = g_w * d / (1.0 + d)
    route_ref[...] = jnp.where(lane == 0, (i1 - N_GROUPS).astype(F32),
                               jnp.where(lane == 1, (i2 - N_GROUPS).astype(F32),
                                         jnp.where(lane == 2, w1, jnp.where(lane == 3, w2, 0.0))))


def _merge(x2d, oa, ob, mg, wa, wb, wo, norm2, w_rg, b_rg, w_re, b_re, tm):
    n = x2d.shape[0]
    const = lambda shape: pl.BlockSpec(shape, lambda i: (0,) * len(shape))
    row = lambda w: pl.BlockSpec((tm, w), lambda i: (i, 0))
    wr = jnp.zeros((D_MODEL, ROUTE_W), F32).at[:, :N_GROUPS].set(w_rg).at[:, N_GROUPS:N_GROUPS + N_EXPERTS].set(w_re)
    br = jnp.zeros((1, ROUTE_W), F32).at[0, :N_GROUPS].set(b_rg).at[0, N_GROUPS:N_GROUPS + N_EXPERTS].set(b_re)
    wrh, wrl = _split2(wr)
    return pl.pallas_call(
        _merge_body,
        grid=(n // tm,),
        in_specs=[row(D_MODEL), row(C_MIX), row(C_MIX), row(2 * D_MODEL), const((C_MIX, D_MODEL)),
                  const((C_MIX, D_MODEL)), const((D_MODEL, D_MODEL)), const((1, D_MODEL)),
                  const((D_MODEL, ROUTE_W)), const((D_MODEL, ROUTE_W)), const((1, ROUTE_W))],
        out_specs=[row(D_MODEL), row(D_MODEL), row(ROUTE_W)],
        out_shape=[jax.ShapeDtypeStruct((n, D_MODEL), F32), jax.ShapeDtypeStruct((n, D_MODEL), BF16),
                   jax.ShapeDtypeStruct((n, ROUTE_W), F32)],
        compiler_params=_params("parallel"),
        name="merge_route",
    )(x2d, oa, ob, mg, wa.astype(BF16), wb.astype(BF16), wo.astype(BF16), norm2.reshape(1, D_MODEL), wrh, wrl, br)


MOE_TB = 256


def _moe_body(be_ref, nu_ref, x_ref, wg_ref, wu_ref, wd_ref, y_ref, wg_b, wu_b, wd_b):
    i = pl.program_id(0)

    @pl.when((i == 0) | (be_ref[i] != be_ref[jnp.maximum(i - 1, 0)]))
    def _():
        wg_b[...] = wg_ref[0].astype(BF16)
        wu_b[...] = wu_ref[0].astype(BF16)
        wd_b[...] = wd_ref[0].astype(BF16)

    @pl.when(i < nu_ref[0])
    def _():
        x = x_ref[...]
        gate = _dot(x, wg_b[...])
        hid = gate * _sigmoid(gate) * _dot(x, wu_b[...])
        y_ref[...] = _dot(hid.astype(BF16), wd_b[...])

    @pl.when(i >= nu_ref[0])
    def _():
        y_ref[...] = jnp.zeros(y_ref.shape, F32)


def _moe_experts(xbuf, blk_e, n_used, wg, wu, wd):
    n_blk = blk_e.shape[0]
    wspec = lambda shape: pl.BlockSpec((1,) + shape, lambda i, be, nu: (be[i], 0, 0))
    return pl.pallas_call(
        _moe_body,
        grid_spec=pltpu.PrefetchScalarGridSpec(
            num_scalar_prefetch=2,
            grid=(n_blk,),
            in_specs=[pl.BlockSpec((MOE_TB, D_MODEL), lambda i, be, nu: (i, 0)), wspec((D_MODEL, D_EXPERT)),
                      wspec((D_MODEL, D_EXPERT)), wspec((D_EXPERT, D_MODEL))],
            out_specs=pl.BlockSpec((MOE_TB, D_MODEL), lambda i, be, nu: (i, 0)),
            scratch_shapes=[pltpu.VMEM((D_MODEL, D_EXPERT), BF16), pltpu.VMEM((D_MODEL, D_EXPERT), BF16),
                            pltpu.VMEM((D_EXPERT, D_MODEL), BF16)],
        ),
        out_shape=jax.ShapeDtypeStruct((n_blk * MOE_TB, D_MODEL), F32),
        compiler_params=_params("arbitrary"),
        name="moe_experts",
    )(blk_e, n_used, xbuf, wg, wu, wd)


def _moe_dispatch(h, route):
    n = h.shape[0]
    expert = route[:, :2].astype(jnp.int32).reshape(-1)
    wts = route[:, 2:4]
    n_slots = 2 * n
    n_blk = -(-n_slots // MOE_TB) + N_EXPERTS
    onehot = expert[:, None] == jnp.arange(N_EXPERTS, dtype=jnp.int32)[None, :]
    counts = jnp.sum(onehot, axis=0, dtype=jnp.int32)
    c_start = jnp.cumsum(counts) - counts
    padded = (counts + MOE_TB - 1) // MOE_TB * MOE_TB
    p_end = jnp.cumsum(padded)
    p_start = p_end - padded
    order = jnp.argsort(expert, stable=True).astype(jnp.int32)
    rank = jnp.argsort(order).astype(jnp.int32)
    dest = (rank + jnp.sum(jnp.where(onehot, (p_start - c_start)[None, :], 0), axis=1)).reshape(n, 2)
    blk_e = jnp.minimum(jnp.sum(p_end[None, :] <= (jnp.arange(n_blk, dtype=jnp.int32) * MOE_TB)[:, None], axis=1),
                        N_EXPERTS - 1).astype(jnp.int32)
    n_used = (p_end[-1:] // MOE_TB).astype(jnp.int32)
    k_in_e = jnp.arange(n_blk * MOE_TB, dtype=jnp.int32) - jnp.repeat(p_start[blk_e], MOE_TB)
    src = order[jnp.clip(jnp.repeat(c_start[blk_e], MOE_TB) + k_in_e, 0, n_slots - 1)]
    row_tok = jnp.where(k_in_e < jnp.repeat(counts[blk_e], MOE_TB), src // 2, 0)
    return h[row_tok], blk_e, n_used, dest, wts


def _moe_apply(x1, dispatch, wg, wu, wd):
    xbuf, blk_e, n_used, dest, wts = dispatch
    ybuf = _moe_experts(xbuf, blk_e, n_used, wg, wu, wd)
    return x1 + wts[:, 0:1] * ybuf[dest[:, 0]] + wts[:, 1:2] * ybuf[dest[:, 1]]


def kernel(x_prompt, x_sample, cache_cmp_kv, cache_slc_kv, cache_win_kv, state_wkv, state_shift, page_table, norm1, w_in, mu_shift, w0, w_decay_up, a0, w_iclr_up, w_gate_up, k_k, k_a, r_k, ln_x_w, ln_x_b, q_norm, kc_norm, ks_norm, kw_norm, cmp_pe_k, cmp_w1_k, cmp_b1_k, cmp_w2_k, cmp_pe_v, cmp_w1_v, cmp_b1_v, cmp_w2_v, w_branch_a, w_branch_b, w_out, norm2, w_route_group, b_route_group, w_route_expert, b_route_expert, w_exp_gate, w_exp_up, w_exp_down):
    assert norm1.shape[0] == 1, "single-layer trunk"
    Bp, Tp, _ = x_prompt.shape
    Bs, Ts, _ = x_sample.shape
    n_pool = cache_cmp_kv.shape[1]
    past = page_table.shape[1] * PAGE_SIZE
    n_buf = cache_win_kv.shape[2]
    kv5 = lambda a, b, t: a.reshape(1, b, t, 2, KV_HEADS, HEAD_DIM)

    w_pad = _pad_w_in(w_in[0])
    rw_p = (mu_shift[0], w0[0], w_decay_up[0], a0[0], w_iclr_up[0], w_gate_up[0], k_k[0], k_a[0],
            r_k[0].reshape(-1), ln_x_w[0], ln_x_b[0])
    wc, pea, peb, b1, w2b = _cmp_weights(cmp_pe_k[0], cmp_w1_k[0], cmp_b1_k[0], cmp_w2_k[0],
                                         cmp_pe_v[0], cmp_w1_v[0], cmp_b1_v[0], cmp_w2_v[0])
    merge_p = (w_branch_a[0], w_branch_b[0], w_out[0], norm2[0], w_route_group[0], b_route_group[0],
               w_route_expert[0], b_route_expert[0])
    moe_w = (w_exp_gate[0], w_exp_up[0], w_exp_down[0])

    xp = x_prompt.reshape(Bp * Tp, D_MODEL)
    zrw, q, kvc, kvs, kvw, ng, mg, kvs_g, kvw_g = _in_proj(xp, norm1[0], w_pad, q_norm[0], ks_norm[0], kw_norm[0], 512)
    zrw3 = zrw.reshape(Bp, Tp, RW_PAD)
    oa, wkv_p = _rwkv(zrw3, jnp.zeros((Bp, 1, RW_PAD), F32), jnp.zeros((Bp, N_HEADS, HEAD_DIM, HEAD_DIM), F32),
                      Tp, *rw_p)
    ab = _cmp_proj(kvc.reshape(Bp * Tp // CMP_STRIDE, CH_W), pea, peb, wc, min(256, Bp * Tp // CMP_STRIDE))
    kc, vc = _cmp_finish(ab.reshape(Bp, Tp // CMP_STRIDE, -1), b1, w2b, kc_norm[0])
    q3 = q.reshape(Bp, Tp, C_MIX)
    o_c, sel = _cmp_attn(q3, kc, vc, 0, (Tp - CMP_LEN) // CMP_STRIDE + 1, Tp // SEL_BLOCK, 128, True)
    ob = _nsa_prompt(q3, kvs_g.reshape(KV_HEADS, Bp, Tp, LANE), kvw_g.reshape(KV_HEADS, Bp, Tp, LANE), sel, o_c,
                     ng.reshape(Bp, Tp, LANE), 128)
    x1, h, route = _merge(xp, oa.reshape(Bp * Tp, C_MIX), ob.reshape(Bp * Tp, C_MIX), mg, *merge_p, 512)
    disp_p = _moe_dispatch(h, route)
    keep_p = min(WINDOW, Tp)

    xs = x_sample.reshape(Bs * Ts, D_MODEL)
    zrw_s, q_s, kvc_s, kvs_s, kvw_s, ng_s, mg_s, _, _ = _in_proj(xs, norm1[0], w_pad, q_norm[0], ks_norm[0],
                                                                 kw_norm[0], Bs * Ts)
    zrw_s3 = zrw_s.reshape(Bs, Ts, RW_PAD)
    oa_s, wkv_s = _rwkv(jnp.pad(zrw_s3, ((0, 0), (0, RW_CHUNK - Ts), (0, 0))),
                        jnp.pad(state_shift[0], ((0, 0), (0, RW_PAD - RW_IN)))[:, None], state_wkv[0], Ts, *rw_p)
    nc_s = (past + Ts - CMP_LEN) // CMP_STRIDE + 1
    assert (nc_s + CMP_LEN // CMP_STRIDE - 1) * CMP_STRIDE <= past, "compression blocks only cover cached rows"
    ab_s = _cmp_proj_paged(cache_cmp_kv[0].reshape(n_pool, PAGE_SIZE, KV_W), page_table, pea, peb, wc)
    kc_s, vc_s = _cmp_finish(ab_s, b1, w2b, kc_norm[0])
    rows8 = lambda a: jnp.pad(a.reshape(Bs, Ts, -1), ((0, 0), (0, DEC_ROWS - Ts), (0, 0)))
    q8 = rows8(q_s)
    oc_s, sel_s = _cmp_attn(q8, kc_s, vc_s, past, nc_s, -(-(past + Ts) // SEL_BLOCK), DEC_ROWS, False)
    assert past % SEL_BLOCK == 0 and Ts <= SEL_BLOCK, "the new rows share one selection block"
    n_st = page_table.shape[1] // CMP_PAGES
    sel4 = sel_s.reshape(Bs, DEC_ROWS, KV_HEADS, -1)
    sel_steps = sel4[..., :past // SEL_BLOCK].reshape(Bs, DEC_ROWS, KV_HEADS, n_st, -1)
    sel_steps = sel_steps.transpose(0, 3, 1, 2, 4).reshape(Bs, n_st, DEC_ROWS, -1)
    sel_new = jnp.pad(sel4[..., past // SEL_BLOCK], ((0, 0), (0, 0), (0, LANE - KV_HEADS)))
    os_s = _nsa_decode("sel", q8, cache_slc_kv[0].reshape(n_pool, PAGE_SIZE, KV_W), page_table, rows8(kvs_s),
                       (sel_steps, sel_new), CMP_PAGES, past, 0)
    win_pages = n_buf // PAGE_SIZE
    ob_s = _nsa_decode("win", q8, cache_win_kv[0].reshape(Bs * win_pages, PAGE_SIZE, KV_W),
                       jnp.arange(Bs * win_pages, dtype=jnp.int32).reshape(Bs, win_pages), rows8(kvw_s),
                       (oc_s, os_s, rows8(ng_s)), win_pages, past, past - n_buf)
    x1_s, h_s, route_s = _merge(xs, oa_s[:, :Ts].reshape(Bs * Ts, C_MIX), ob_s[:, :Ts].reshape(Bs * Ts, C_MIX), mg_s,
                                *merge_p, Bs * Ts)
    disp_p, x1_s = lax.optimization_barrier((disp_p, x1_s))
    y_p = _moe_apply(x1, disp_p, *moe_w).reshape(Bp, Tp, D_MODEL)
    y_s = _moe_apply(x1_s, _moe_dispatch(h_s, route_s), *moe_w).reshape(Bs, Ts, D_MODEL)
    keep_s = min(WINDOW, n_buf + Ts)
    win_s = jnp.concatenate([cache_win_kv[0].reshape(Bs, n_buf, KV_W), kvw_s.reshape(Bs, Ts, KV_W)], axis=1)

    return (y_p, y_s,
            kv5(kvc, Bp, Tp), kv5(kvs, Bp, Tp), kv5(kvw.reshape(Bp, Tp, KV_W)[:, Tp - keep_p:], Bp, keep_p),
            wkv_p[None], zrw3[:, -1, :RW_IN][None],
            kv5(kvc_s, Bs, Ts), kv5(kvs_s, Bs, Ts), kv5(win_s[:, n_buf + Ts - keep_s:], Bs, keep_s),
            wkv_s[None], zrw_s3[:, -1, :RW_IN][None])
```

```python
import functools
import math

import jax
import jax.numpy as jnp
from jax import lax
from jax.experimental import pallas as pl
from jax.experimental.pallas import tpu as pltpu

F32 = jnp.float32
BF16 = jnp.bfloat16

D_MODEL = 1024
HEAD_DIM = 64
N_HEADS = 8
C_MIX = N_HEADS * HEAD_DIM
R_DECAY, R_ICLR, R_GATE = 32, 32, 96
RW_IN = 3 * C_MIX + R_DECAY + R_ICLR + R_GATE
KV_HEADS = 2
KV_GROUP = N_HEADS // KV_HEADS
KV_W = 2 * KV_HEADS * HEAD_DIM
CMP_LEN, CMP_STRIDE = 32, 16
SEL_BLOCK = 64
N_SEL = 16
WINDOW = 512
PAGE_SIZE = 128
N_GROUPS, EXPERTS_PER_GROUP = 4, 8
N_EXPERTS = N_GROUPS * EXPERTS_PER_GROUP
D_EXPERT = D_MODEL // 2
RMS_EPS = 1e-6
GN_EPS = 64e-5
NEG_INF = -1e30
FORCE_SCORE = 1e6

LANE = 128
VMEM_LIMIT = 56 * 1024 * 1024

RW_PAD = 1792
OFF_Q = RW_PAD
OFF_KVC = OFF_Q + C_MIX
OFF_KVS = OFF_KVC + KV_W
OFF_KVW = OFF_KVS + KV_W
OFF_NG = OFF_KVW + KV_W
OFF_MG = OFF_NG + LANE
N_IN_PAD = OFF_MG + 2 * D_MODEL
RW_TAIL = 3 * C_MIX


def _params(*sem):
    return pltpu.CompilerParams(dimension_semantics=sem, vmem_limit_bytes=VMEM_LIMIT)


def _dot(a, b):
    return jnp.dot(a, b, preferred_element_type=F32)


def _dot_nt(a, b):
    return lax.dot_general(a, b, (((1,), (1,)), ((), ())), preferred_element_type=F32)


def _dot_tn(a, b):
    return lax.dot_general(a, b, (((0,), (0,)), ((), ())), preferred_element_type=F32)


def _split2(x):
    hi = x.astype(BF16)
    lo = (x - hi.astype(F32)).astype(BF16)
    return hi, lo


def _split3(x):
    hi = x.astype(BF16)
    r1 = x - hi.astype(F32)
    mid = r1.astype(BF16)
    lo = (r1 - mid.astype(F32)).astype(BF16)
    return hi, mid, lo


def _gsum(y, g):
    hi, lo = _split2(y)
    return _dot(hi, g) + _dot(lo, g)


def _block_ones(n, blk):
    i = jnp.arange(n) // blk
    return (i[:, None] == i[None, :]).astype(BF16)


def _sigmoid(x):
    return 1.0 / (1.0 + jnp.exp(-x))


def _inproj_body(x_ref, n1_ref, w_ref, g512_ref, g128_ref, qn_ref, ksn_ref, kwn_ref,
                 zrw_ref, q_ref, kvc_ref, kvs_ref, kvw_ref, ng_ref, mg_ref, kvsg_ref, kvwg_ref,
                 kvct_ref, kvst_ref, kvwt_ref):
    x = x_ref[...]
    ms = jnp.mean(x * x, axis=-1, keepdims=True)
    xn = (x * lax.rsqrt(ms + RMS_EPS) * n1_ref[...]).astype(BF16)

    def proj(a, b):
        return _dot(xn, w_ref[:, a:b])

    zrw_ref[...] = proj(0, RW_PAD)
    q = proj(OFF_Q, OFF_KVC)
    q_ref[...] = q * lax.rsqrt(_gsum(q * q, g512_ref[...]) * (1.0 / HEAD_DIM) + RMS_EPS) * qn_ref[...]
    kvc = proj(OFF_KVC, OFF_KVS)
    kvc_ref[...] = kvc
    kvct_ref[0] = kvc.T
    for off, nref, oref, gref, tref in ((OFF_KVS, ksn_ref, kvs_ref, kvsg_ref, kvst_ref),
                                        (OFF_KVW, kwn_ref, kvw_ref, kvwg_ref, kvwt_ref)):
        kv = proj(off, off + KV_W)
        k = kv[:, :LANE]
        kn = k * lax.rsqrt(_gsum(k * k, g128_ref[...]) * (1.0 / HEAD_DIM) + RMS_EPS) * nref[...]
        v = kv[:, LANE:]
        oref[:, :LANE] = kn
        oref[:, LANE:] = v
        tref[0, :LANE] = kn.T
        tref[0, LANE:] = v.T
        for g in range(KV_HEADS):
            hs = slice(g * HEAD_DIM, (g + 1) * HEAD_DIM)
            gref[g] = jnp.concatenate([kn[:, hs], v[:, hs]], axis=1).astype(BF16)
    ng_ref[...] = _sigmoid(proj(OFF_NG, OFF_MG))
    mg_ref[...] = _sigmoid(proj(OFF_MG, N_IN_PAD))


def _in_proj(x2d, norm1, w_pad, q_norm, ks_norm, kw_norm, tm, seq_len):
    n = x2d.shape[0]
    seq_tiles = seq_len // tm
    const = lambda shape: pl.BlockSpec(shape, lambda i: (0,) * len(shape))
    row = lambda w: pl.BlockSpec((tm, w), lambda i: (i, 0))
    widths = (RW_PAD, C_MIX, KV_W, KV_W, KV_W, LANE, 2 * D_MODEL)
    return pl.pallas_call(
        _inproj_body,
        grid=(n // tm,),
        in_specs=[row(D_MODEL), const((1, D_MODEL)), const((D_MODEL, N_IN_PAD)), const((C_MIX, C_MIX)),
                  const((LANE, LANE)), const((1, C_MIX)), const((1, LANE)), const((1, LANE))],
        out_specs=[row(w) for w in widths] + [pl.BlockSpec((KV_HEADS, tm, LANE), lambda i: (0, i, 0))] * 2
                  + [pl.BlockSpec((1, KV_W, tm), lambda i: (i // seq_tiles, 0, i % seq_tiles))] * 3,
        out_shape=[jax.ShapeDtypeStruct((n, w), F32) for w in widths]
                  + [jax.ShapeDtypeStruct((KV_HEADS, n, LANE), BF16)] * 2
                  + [jax.ShapeDtypeStruct((n // seq_len, KV_W, seq_len), F32)] * 3,
        compiler_params=_params("parallel"),
        name="in_proj",
    )(x2d, norm1.reshape(1, D_MODEL), w_pad, _block_ones(C_MIX, HEAD_DIM), _block_ones(LANE, HEAD_DIM),
      jnp.tile(q_norm, N_HEADS).reshape(1, C_MIX), jnp.tile(ks_norm, KV_HEADS).reshape(1, LANE),
      jnp.tile(kw_norm, KV_HEADS).reshape(1, LANE))


def _pad_w_in(w_in):
    d = w_in.shape[0]
    z = lambda n: jnp.zeros((d, n), w_in.dtype)
    o_q = RW_IN
    o_ng = o_q + C_MIX + 3 * KV_W
    o_mg = o_ng + 3 * N_HEADS
    return jnp.concatenate([w_in[:, :RW_IN], z(RW_PAD - RW_IN), w_in[:, o_q:o_ng], w_in[:, o_ng:o_mg],
                            z(LANE - 3 * N_HEADS), w_in[:, o_mg:]], axis=1).astype(BF16)


RW_CHUNK = 64
RW_HSTACK = 4
RW_ROWS = 4


def _rwkv_body(t_valid, z_ref, sp_ref, s0_ref, mu_ref, w0_ref, a0_ref, kk_ref, ka_ref, rk_ref, lnw_ref, lnb_ref,
               wd_ref, wi_ref, wg_ref, g512_ref, o_ref, s_ref, prev_scr):
    @pl.when(pl.program_id(1) == 0)
    def _():
        s_ref[...] = s0_ref[...]
        prev_scr[...] = sp_ref[...]

    prep = [_rwkv_prep(t_valid, bi, z_ref, mu_ref, w0_ref, a0_ref, kk_ref, ka_ref, rk_ref, wd_ref, wi_ref, wg_ref,
                       g512_ref, prev_scr) for bi in range(z_ref.shape[0])]
    ys = _rwkv_chains(prep, s_ref)
    g512 = g512_ref[...]
    for bi, (p, y) in enumerate(zip(prep, ys)):
        yc = y - _gsum(y, g512) * (1.0 / HEAD_DIM)
        var = _gsum(yc * yc, g512) * (1.0 / HEAD_DIM)
        yn = yc * lax.rsqrt(var + GN_EPS) * lnw_ref[...] + lnb_ref[...]
        o_ref[bi] = (yn + _gsum(p["rkk"], g512) * p["v"]) * p["g"]


def _rwkv_prep(t_valid, bi, z_ref, mu_ref, w0_ref, a0_ref, kk_ref, ka_ref, rk_ref, wd_ref, wi_ref, wg_ref,
               g512_ref, prev_scr):
    C = RW_CHUNK
    c = pl.program_id(1)
    z = z_ref[bi]
    row = lax.broadcasted_iota(jnp.int32, (C, 1), 0)
    z_prev = jnp.where(row == 0, prev_scr[bi], pltpu.roll(z, 1, axis=0))
    prev_scr[bi] = z[C - 1:C]
    zm = z + (z_prev - z) * mu_ref[...]
    r = zm[:, 0:C_MIX]
    k = zm[:, C_MIX:2 * C_MIX]
    v = zm[:, 2 * C_MIX:3 * C_MIX]
    tail = zm[:, RW_TAIL:RW_PAD]
    w_lora = _dot(jnp.tanh(tail).astype(BF16), wd_ref[...])
    a_lora = _dot(tail.astype(BF16), wi_ref[...])
    g = _dot(_sigmoid(tail).astype(BF16), wg_ref[...])
    u = -(w0_ref[...] + w_lora)
    softplus = jnp.maximum(u, 0.0) + jnp.log(1.0 + jnp.exp(-jnp.abs(u)))
    w_log = -softplus - 0.5
    valid = (c * C + row) < t_valid
    ld = jnp.where(valid, -jnp.exp(w_log), 0.0)
    a = _sigmoid(a0_ref[...] + a_lora)
    kk = k * kk_ref[...]
    kk = kk / jnp.maximum(jnp.sqrt(_gsum(kk * kk, g512_ref[...])), 1e-12)
    k2 = k * (1.0 + (a - 1.0) * ka_ref[...])

    ci = lax.broadcasted_iota(jnp.int32, (C, C), 0)
    cj = lax.broadcasted_iota(jnp.int32, (C, C), 1)
    tri = (ci >= cj).astype(BF16)
    h1, h2, h3 = _split3(ld)
    cl = _dot(tri, h1) + _dot(tri, h2) + _dot(tri, h3)
    p_in = jnp.exp(cl)
    p_inv = jnp.exp(-cl)
    r_t = r * p_in
    a_t = -kk * jnp.exp(cl - ld)
    b_t = jnp.where(valid, kk * a * p_inv, 0.0)
    k_t = jnp.where(valid, k2 * p_inv, 0.0)
    p_end = p_in[C - 1:C]

    return dict(a_t=a_t, r_t=r_t, b_t=b_t, k_t=k_t, v=v, p_end=p_end, rkk=r * k2 * rk_ref[...], g=g)


def _rwkv_chains(prep, s_ref):
    C = RW_CHUNK
    HS = RW_HSTACK
    R = HS * C
    ri = lax.broadcasted_iota(jnp.int32, (R, R), 0)
    rj = lax.broadcasted_iota(jnp.int32, (R, R), 1)
    same = (ri // C) == (rj // C)
    lower = same & (ri > rj)
    lower_eq = same & (ri >= rj)
    eye = (ri == rj).astype(F32)
    chains = [(bi, hg) for bi in range(len(prep)) for hg in range(N_HEADS // HS)]
    heads = lambda hg: [hg * HS + m for m in range(HS)]
    stack = lambda x, hg: jnp.concatenate([x[:, h * HEAD_DIM:(h + 1) * HEAD_DIM] for h in heads(hg)],
                                          axis=0).astype(BF16)
    A, Rr, Bm, Km, V = ([stack(prep[bi][name], hg) for bi, hg in chains]
                        for name in ("a_t", "r_t", "b_t", "k_t", "v"))
    S4 = [_dot_nt(jnp.concatenate([a, r], axis=0), jnp.concatenate([b, k], axis=0))
          for a, r, b, k in zip(A, Rr, Bm, Km)]
    L = [jnp.where(lower, s[:R, :R], 0.0) for s in S4]
    Lak = [jnp.where(lower, s[:R, R:], 0.0).astype(BF16) for s in S4]
    Mr = [jnp.concatenate([jnp.where(lower_eq, s[R:, :R], 0.0), jnp.where(lower_eq, s[R:, R:], 0.0)],
                          axis=1).astype(BF16) for s in S4]
    s0 = [[s_ref[bi, h] for h in heads(hg)] for bi, hg in chains]
    on_state = [[_dot_nt(jnp.concatenate([a[m * C:(m + 1) * C], r[m * C:(m + 1) * C]], axis=0),
                         s0c[m].astype(BF16)) for m in range(HS)]
                for a, r, s0c in zip(A, Rr, s0)]
    rhs = [jnp.concatenate([o[:C] for o in os], axis=0) + _dot(lak, v) for os, lak, v in zip(on_state, Lak, V)]
    X = [eye + l for l in L]
    Lp = L
    for _ in range(int(math.log2(C)) - 1):
        Lpb = [lp.astype(BF16) for lp in Lp]
        Lp = [_dot(lp, lp) for lp in Lpb]
        X = [x + _dot(x.astype(BF16), lp.astype(BF16)) for x, lp in zip(X, Lp)]
    U = [_dot(x.astype(BF16), r.astype(BF16)).astype(BF16) for x, r in zip(X, rhs)]
    Y = [jnp.concatenate([o[C:] for o in os], axis=0) + _dot(mr, jnp.concatenate([u, v], axis=0))
         for os, mr, u, v in zip(on_state, Mr, U, V)]
    for ci, (bi, hg) in enumerate(chains):
        for m, h in enumerate(heads(hg)):
            rs = slice(m * C, (m + 1) * C)
            upd = _dot_tn(jnp.concatenate([U[ci][rs], V[ci][rs]], axis=0),
                          jnp.concatenate([Bm[ci][rs], Km[ci][rs]], axis=0))
            s_ref[bi, h] = (s0[ci][m] + upd) * prep[bi]["p_end"][:, h * HEAD_DIM:(h + 1) * HEAD_DIM]
    return [jnp.concatenate([Y[ci][m * C:(m + 1) * C] for ci, (cb, _) in enumerate(chains) if cb == bi
                             for m in range(HS)], axis=1) for bi in range(len(prep))]


def _rwkv(z_rw, shift_prev, s0, t_valid, mu, w0, wd, a0, wi, wg, k_k, k_a, r_k, ln_w, ln_b):
    B, T, _ = z_rw.shape
    C = RW_CHUNK
    const = lambda shape: pl.BlockSpec(shape, lambda b, c: (0,) * len(shape))
    vec = lambda p: p.reshape(1, C_MIX)
    pad_rows = lambda w, off: jnp.zeros((RW_PAD - RW_TAIL, C_MIX), F32).at[off:off + w.shape[0]].set(w).astype(BF16)
    nb = math.gcd(B, RW_ROWS)
    state_spec = pl.BlockSpec((nb, N_HEADS, HEAD_DIM, HEAD_DIM), lambda b, c: (b, 0, 0, 0))
    return pl.pallas_call(
        functools.partial(_rwkv_body, t_valid),
        grid=(B // nb, T // C),
        in_specs=[pl.BlockSpec((nb, C, RW_PAD), lambda b, c: (b, c, 0)),
                  pl.BlockSpec((nb, 1, RW_PAD), lambda b, c: (b, 0, 0)),
                  state_spec, const((1, RW_PAD))] + [const((1, C_MIX))] * 7
                 + [const((RW_PAD - RW_TAIL, C_MIX))] * 3 + [const((C_MIX, C_MIX))],
        out_specs=[pl.BlockSpec((nb, C, C_MIX), lambda b, c: (b, c, 0)), state_spec],
        out_shape=[jax.ShapeDtypeStruct((B, T, C_MIX), F32),
                   jax.ShapeDtypeStruct((B, N_HEADS, HEAD_DIM, HEAD_DIM), F32)],
        scratch_shapes=[pltpu.VMEM((nb, 1, RW_PAD), F32)],
        compiler_params=_params("parallel", "arbitrary"),
        name="rwkv7",
    )(z_rw, shift_prev, s0, jnp.pad(mu, (0, RW_PAD - RW_IN)).reshape(1, RW_PAD), vec(w0), vec(a0), vec(k_k),
      vec(k_a), vec(r_k), vec(ln_w), vec(ln_b), pad_rows(wd, 0), pad_rows(wi, R_DECAY),
      pad_rows(wg, R_DECAY + R_ICLR), _block_ones(C_MIX, HEAD_DIM))


CH_W = CMP_STRIDE * KV_W
N_SLOT = 2 * KV_HEADS


def _cmp_weights(pe_k, w1_k, b1_k, w2_k, pe_v, w1_v, b1_v, w2_v):
    eye = jnp.eye(N_SLOT, dtype=F32)
    w1 = jnp.stack([w1_k, w1_k, w1_v, w1_v])
    pe = jnp.stack([pe_k, pe_k, pe_v, pe_v])
    halves = []
    pes = []
    for r in range(CMP_LEN // CMP_STRIDE):
        ls = slice(r * CMP_STRIDE, (r + 1) * CMP_STRIDE)
        halves.append(jnp.einsum('sldf,st->lsdtf', w1[:, ls], eye).reshape(CH_W, N_SLOT * HEAD_DIM))
        pes.append(jnp.transpose(pe[:, ls], (1, 0, 2)).reshape(1, CH_W))
    wc = jnp.concatenate(halves, axis=1).astype(BF16)
    w2 = jnp.stack([w2_k, w2_k, w2_v, w2_v])
    w2b = jnp.einsum('sfd,st->sftd', w2, eye).reshape(N_SLOT * HEAD_DIM, N_SLOT * HEAD_DIM).astype(BF16)
    b1 = jnp.concatenate([b1_k, b1_k, b1_v, b1_v]).reshape(1, N_SLOT * HEAD_DIM)
    return wc, pes[0], pes[1], b1, w2b


def _cmp_proj_body(ch_ref, pea_ref, peb_ref, wc_ref, ab_ref):
    ch = ch_ref[...]
    w = N_SLOT * HEAD_DIM
    ab_ref[:, :w] = _dot((ch + pea_ref[...]).astype(BF16), wc_ref[:, :w])
    ab_ref[:, w:] = _dot((ch + peb_ref[...]).astype(BF16), wc_ref[:, w:])


def _cmp_proj(chunks, pea, peb, wc, tr):
    n = chunks.shape[0]
    const = lambda shape: pl.BlockSpec(shape, lambda i: (0,) * len(shape))
    return pl.pallas_call(
        _cmp_proj_body,
        grid=(n // tr,),
        in_specs=[pl.BlockSpec((tr, CH_W), lambda i: (i, 0)), const((1, CH_W)), const((1, CH_W)),
                  const((CH_W, 2 * N_SLOT * HEAD_DIM))],
        out_specs=pl.BlockSpec((tr, 2 * N_SLOT * HEAD_DIM), lambda i: (i, 0)),
        out_shape=jax.ShapeDtypeStruct((n, 2 * N_SLOT * HEAD_DIM), F32),
        compiler_params=_params("parallel"),
        name="cmp_proj",
    )(chunks, pea, peb, wc)


CMP_PAGES = 16


def _paged_fetch(pt_ref, pool_ref, buf, sem, pages):
    b = pl.program_id(0)
    st = pl.program_id(1)
    n_st = pl.num_programs(1)
    step = b * n_st + st
    slot = step % 2
    ptok = pool_ref.shape[2]

    def copies(bb, stt, sl):
        return [pltpu.make_async_copy(pool_ref.at[pt_ref[bb, stt * pages + i]],
                                      buf.at[sl, :, pl.ds(i * ptok, ptok)], sem.at[sl]) for i in range(pages)]

    @pl.when(step == 0)
    def _():
        for c in copies(0, 0, 0):
            c.start()

    @pl.when(step + 1 < pl.num_programs(0) * n_st)
    def _():
        wrap = st + 1 == n_st
        for c in copies(jnp.where(wrap, b + 1, b), jnp.where(wrap, 0, st + 1), 1 - slot):
            c.start()

    for c in copies(b, st, slot):
        c.wait()
    return slot


def _cmp_proj_paged_body(pt_ref, cache_ref, pe_ref, wl_ref, ab_ref, buf, sem, rows_scr):
    slot = _paged_fetch(pt_ref, cache_ref, buf, sem, CMP_PAGES)
    n_chunks = CMP_PAGES * PAGE_SIZE // CMP_STRIDE
    w = N_SLOT * HEAD_DIM
    for i in range(CMP_PAGES):
        for j in range(rows_scr.shape[0]):
            rows_scr[j, i * PAGE_SIZE:(i + 1) * PAGE_SIZE, :] = buf[slot, j * LANE:(j + 1) * LANE,
                                                                    i * PAGE_SIZE:(i + 1) * PAGE_SIZE].T
    acc = [jnp.zeros((n_chunks, w), F32) for _ in range(CMP_LEN // CMP_STRIDE)]
    for l in range(CMP_STRIDE):
        x = jnp.concatenate([rows_scr[j, pl.ds(l, n_chunks, stride=CMP_STRIDE), :] for j in range(rows_scr.shape[0])],
                            axis=1)
        for r in range(CMP_LEN // CMP_STRIDE):
            acc[r] = acc[r] + _dot((x + pe_ref[r, l]).astype(BF16), wl_ref[l, :, r * w:(r + 1) * w])
    for r in range(CMP_LEN // CMP_STRIDE):
        ab_ref[0, :, r * w:(r + 1) * w] = acc[r]


def _cmp_proj_paged(cache, page_table, pea, peb, wc):
    B, n_pages = page_table.shape
    rows = PAGE_SIZE // CMP_STRIDE
    w = N_SLOT * HEAD_DIM
    const = lambda shape: pl.BlockSpec(shape, lambda b, g, pt: (0,) * len(shape))
    pe = jnp.stack([pea, peb]).reshape(CMP_LEN // CMP_STRIDE, CMP_STRIDE, 1, w)
    wl = wc.reshape(CMP_STRIDE, w, 2 * w)
    return pl.pallas_call(
        _cmp_proj_paged_body,
        grid_spec=pltpu.PrefetchScalarGridSpec(
            num_scalar_prefetch=1,
            grid=(B, n_pages // CMP_PAGES),
            in_specs=[pl.BlockSpec(memory_space=pl.ANY), const(pe.shape), const(wl.shape)],
            out_specs=pl.BlockSpec((1, CMP_PAGES * rows, 2 * w), lambda b, g, pt: (b, g, 0)),
            scratch_shapes=[pltpu.VMEM((2, KV_W, CMP_PAGES * PAGE_SIZE), F32), pltpu.SemaphoreType.DMA((2,)),
                            pltpu.VMEM((KV_W // LANE, CMP_PAGES * PAGE_SIZE, LANE), F32)],
        ),
        out_shape=jax.ShapeDtypeStruct((B, n_pages * rows, 2 * w), F32),
        compiler_params=_params("arbitrary", "arbitrary"),
        name="cmp_proj_paged",
    )(page_table, cache, pe, wl)


def _cmp_finish_body(ab_ref, b1_ref, w2_ref, kcn_ref, g128_ref, kc_ref, vc_ref):
    ab = ab_ref[0]
    n = ab.shape[0]
    w = N_SLOT * HEAD_DIM
    pre = ab[:, :w] + pltpu.roll(ab[:, w:], n - 1, axis=0) + b1_ref[...]
    hid = pre * _sigmoid(pre)
    out = _dot(hid.astype(BF16), w2_ref[...])
    k = out[:, :LANE]
    kc_ref[0] = k * lax.rsqrt(_gsum(k * k, g128_ref[...]) * (1.0 / HEAD_DIM) + RMS_EPS) * kcn_ref[...]
    vc_ref[0] = out[:, LANE:]


def _cmp_finish(ab, b1, w2b, kc_norm):
    B, n, _ = ab.shape
    const = lambda shape: pl.BlockSpec(shape, lambda b: (0,) * len(shape))
    w = N_SLOT * HEAD_DIM
    return pl.pallas_call(
        _cmp_finish_body,
        grid=(B,),
        in_specs=[pl.BlockSpec((1, n, 2 * w), lambda b: (b, 0, 0)), const((1, w)), const((w, w)), const((1, LANE)),
                  const((LANE, LANE))],
        out_specs=[pl.BlockSpec((1, n, LANE), lambda b: (b, 0, 0))] * 2,
        out_shape=[jax.ShapeDtypeStruct((B, n, LANE), F32)] * 2,
        compiler_params=_params("parallel"),
        name="cmp_finish",
    )(ab, b1, w2b, jnp.tile(kc_norm, KV_HEADS).reshape(1, LANE), _block_ones(LANE, HEAD_DIM))


def _cmp_attn_body(pos0, nc, nb, nbp, blocks_on_rows, q_ref, kc_ref, vc_ref, ovl_ref, o_ref, sel_ref):
    tq = q_ref.shape[1]
    ncp = kc_ref.shape[1]
    q = q_ref[0] * (HEAD_DIM ** -0.5)
    t0 = pos0 + pl.program_id(1) * tq
    t = t0 + lax.broadcasted_iota(jnp.int32, (tq, 1), 0)
    cidx = lax.broadcasted_iota(jnp.int32, (1, ncp), 1)
    cmask = (cidx * CMP_STRIDE + (CMP_LEN - 1) <= t) & (cidx < nc)
    if blocks_on_rows:
        tt = t0 + lax.broadcasted_iota(jnp.int32, (1, tq), 1)
        j = lax.broadcasted_iota(jnp.int32, (nbp, 1), 0)
    else:
        tt = t
        j = lax.broadcasted_iota(jnp.int32, (1, nbp), 1)
    cur = tt // SEL_BLOCK
    valid = (j * SEL_BLOCK <= tt) & (j < nb)
    forced = (j == 0) | (j == cur) | (j == cur - 1)
    for g in range(KV_HEADS):
        kcg = kc_ref[0, :, g * HEAD_DIM:(g + 1) * HEAD_DIM].astype(BF16)
        vcg = vc_ref[0, :, g * HEAD_DIM:(g + 1) * HEAD_DIM].astype(BF16)
        s = _dot_nt(_stack_heads(q, g).astype(BF16), kcg).reshape(KV_GROUP, tq, ncp)
        s = jnp.where(cmask[None], s, NEG_INF)
        e = jnp.where(cmask[None], jnp.exp(s - jnp.max(s, axis=-1, keepdims=True)), 0.0)
        p = e / jnp.maximum(jnp.sum(e, axis=-1, keepdims=True), 1e-30)
        o = _dot(p.reshape(KV_GROUP * tq, ncp).astype(BF16), vcg)
        for m in range(KV_GROUP):
            h = g * KV_GROUP + m
            o_ref[0, :, h * HEAD_DIM:(h + 1) * HEAD_DIM] = o[m * tq:(m + 1) * tq]
        hi, lo = _split2(jnp.sum(p, axis=0))
        if blocks_on_rows:
            imp = _dot_nt(ovl_ref[...], hi) + _dot_nt(ovl_ref[...], lo)
        else:
            imp = _dot(hi, ovl_ref[...]) + _dot(lo, ovl_ref[...])
        score = jnp.where(valid, jnp.where(forced, FORCE_SCORE, imp), NEG_INF)
        cnt = jnp.zeros(score.shape, jnp.int32)
        for jp in range(nb):
            cj = score[jp:jp + 1, :] if blocks_on_rows else score[:, jp:jp + 1]
            cnt = cnt + jnp.where(j > jp, jnp.where(cj >= score, 1, 0), jnp.where(cj > score, 1, 0))
        picked = (cnt < N_SEL).astype(F32)
        if blocks_on_rows:
            sel_ref[0, g] = picked
        else:
            sel_ref[0, :, g * nbp:(g + 1) * nbp] = picked


def _cmp_attn(q, kc, vc, pos0, nc, nb, tq, blocks_on_rows):
    B, Tq, _ = q.shape
    ncp = kc.shape[1]
    nbp = -(-nb // SEL_BLOCK) * SEL_BLOCK
    c0 = jnp.arange(ncp)[:, None] * CMP_STRIDE
    jj = jnp.arange(nbp)[None, :]
    ovl = ((c0 < (jj + 1) * SEL_BLOCK) & (c0 + CMP_LEN > jj * SEL_BLOCK) & (jnp.arange(ncp)[:, None] < nc)
           & (jj < nb)).astype(BF16)
    if blocks_on_rows:
        ovl = ovl.T
        sel_spec = pl.BlockSpec((1, KV_HEADS, nbp, tq), lambda b, i: (b, 0, 0, i))
        sel_shape = (B, KV_HEADS, nbp, Tq)
    else:
        sel_spec = pl.BlockSpec((1, tq, KV_HEADS * nbp), lambda b, i: (b, i, 0))
        sel_shape = (B, Tq, KV_HEADS * nbp)
    return pl.pallas_call(
        functools.partial(_cmp_attn_body, pos0, nc, nb, nbp, blocks_on_rows),
        grid=(B, Tq // tq),
        in_specs=[pl.BlockSpec((1, tq, C_MIX), lambda b, i: (b, i, 0)),
                  pl.BlockSpec((1, ncp, LANE), lambda b, i: (b, 0, 0)),
                  pl.BlockSpec((1, ncp, LANE), lambda b, i: (b, 0, 0)),
                  pl.BlockSpec(ovl.shape, lambda b, i: (0, 0))],
        out_specs=[pl.BlockSpec((1, tq, C_MIX), lambda b, i: (b, i, 0)), sel_spec],
        out_shape=[jax.ShapeDtypeStruct((B, Tq, C_MIX), F32), jax.ShapeDtypeStruct(sel_shape, F32)],
        compiler_params=_params("parallel", "parallel"),
        name="cmp_attn",
    )(q, kc, vc, ovl)


def _softmax_step(carry, s, mask, vb, v_feature_major=False):
    m_, l_, acc = carry
    s = jnp.where(mask, s, NEG_INF)
    m_new = jnp.maximum(m_, jnp.max(s, axis=-1, keepdims=True))
    alpha = jnp.exp(m_ - m_new)
    p = jnp.where(mask, jnp.exp(s - m_new), 0.0)
    pv = _dot_nt(p.astype(BF16), vb) if v_feature_major else _dot(p.astype(BF16), vb)
    return m_new, alpha * l_ + jnp.sum(p, axis=-1, keepdims=True), alpha * acc + pv


def _softmax_init(rows):
    return jnp.full((rows, 1), NEG_INF, F32), jnp.zeros((rows, 1), F32), jnp.zeros((rows, HEAD_DIM), F32)


def _stack_heads(x, g):
    return jnp.concatenate([x[:, (g * KV_GROUP + m) * HEAD_DIM:(g * KV_GROUP + m + 1) * HEAD_DIM]
                            for m in range(KV_GROUP)], axis=0)


def _gate_mix(ng, o_c, o_s, o_w, g, tq, o_ref, os_stacked=True):
    for m in range(KV_GROUP):
        h = g * KV_GROUP + m
        sl = slice(h * HEAD_DIM, (h + 1) * HEAD_DIM)
        rs = slice(m * tq, (m + 1) * tq)
        o_ref[0, :, sl] = (ng[:, h:h + 1] * o_c[:, sl]
                           + ng[:, N_HEADS + h:N_HEADS + h + 1] * (o_s[rs] if os_stacked else o_s[:, sl])
                           + ng[:, 2 * N_HEADS + h:2 * N_HEADS + h + 1] * o_w[rs])


SEL_TK = 1024


def _flash_step(carries, qgs, kvs, biases, tq):
    tk = kvs[0].shape[0]
    left = lax.broadcasted_iota(jnp.int32, (1, LANE), 1) < HEAD_DIM
    s = [_dot_nt(qg, kv).reshape(KV_GROUP, tq, tk) + bias[None] for qg, kv, bias in zip(qgs, kvs, biases)]
    m_new = [jnp.maximum(c[0], jnp.max(x, axis=-1, keepdims=True)) for c, x in zip(carries, s)]
    p = [jnp.exp(x - m).astype(BF16).reshape(KV_GROUP * tq, tk) for x, m in zip(s, m_new)]
    pv = [_dot(x, jnp.where(left, jnp.ones((), BF16), kv)).reshape(KV_GROUP, tq, LANE) for x, kv in zip(p, kvs)]
    return tuple((m, jnp.exp(c[0] - m) * c[1] + y) for c, m, y in zip(carries, m_new, pv))


def _flash_init(tq):
    return jnp.full((KV_GROUP, tq, 1), NEG_INF, F32), jnp.zeros((KV_GROUP, tq, LANE), F32)


def _flash_out(carry):
    acc = carry[1]
    return acc[:, :, HEAD_DIM:] / jnp.maximum(acc[:, :, :1], 1e-30)


def _nsa_prompt_body(q_ref, kvs_ref, kvw_ref, sel_ref, oc_ref, ng_ref, o_ref):
    tq = q_ref.shape[1]
    nbp = sel_ref.shape[2]
    q0 = pl.program_id(1) * tq
    q = q_ref[0] * (HEAD_DIM ** -0.5)
    t = q0 + lax.broadcasted_iota(jnp.int32, (tq, 1), 0)
    blk = lax.broadcasted_iota(jnp.int32, (nbp, 1), 0)
    zpad = jnp.zeros((tq, LANE - HEAD_DIM), F32)
    qgs = []
    sels = []
    for g in range(KV_HEADS):
        heads = [jnp.concatenate([q[:, (g * KV_GROUP + m) * HEAD_DIM:(g * KV_GROUP + m + 1) * HEAD_DIM], zpad], axis=1)
                 for m in range(KV_GROUP)]
        qgs.append(jnp.concatenate(heads, axis=0).astype(BF16))
        sels.append(sel_ref[0, g].astype(BF16))

    def sel_step(kt, carry):
        k0 = pl.multiple_of(kt * SEL_TK, SEL_TK)
        kpos = k0 + lax.broadcasted_iota(jnp.int32, (1, SEL_TK), 1)
        expand = (kpos // SEL_BLOCK == blk).astype(BF16)
        causal = kpos <= t
        biases = [jnp.where((_dot_tn(sels[g], expand) > 0.5) & causal, 0.0, NEG_INF) for g in range(KV_HEADS)]
        return _flash_step(carry, qgs, [kvs_ref[g, 0, pl.ds(k0, SEL_TK), :] for g in range(KV_HEADS)], biases, tq)

    res_s = lax.fori_loop(0, (q0 + tq + SEL_TK - 1) // SEL_TK, sel_step, (_flash_init(tq),) * KV_HEADS)

    span = WINDOW + tq
    w0 = pl.multiple_of(jnp.maximum(q0 - WINDOW, 0), tq)
    wpos = w0 + lax.broadcasted_iota(jnp.int32, (1, span), 1)
    wbias = jnp.where((wpos <= t) & (wpos > t - WINDOW), 0.0, NEG_INF)
    res_w = _flash_step((_flash_init(tq),) * KV_HEADS, qgs,
                        [kvw_ref[g, 0, pl.ds(w0, span), :] for g in range(KV_HEADS)], [wbias] * KV_HEADS, tq)

    ng = ng_ref[0]
    oc = oc_ref[0]
    for g in range(KV_HEADS):
        o_s = _flash_out(res_s[g])
        o_w = _flash_out(res_w[g])
        for m in range(KV_GROUP):
            h = g * KV_GROUP + m
            sl = slice(h * HEAD_DIM, (h + 1) * HEAD_DIM)
            o_ref[0, :, sl] = (ng[:, h:h + 1] * oc[:, sl] + ng[:, N_HEADS + h:N_HEADS + h + 1] * o_s[m]
                               + ng[:, 2 * N_HEADS + h:2 * N_HEADS + h + 1] * o_w[m])


def _nsa_prompt(q, kvs_g, kvw_g, sel, o_c, ng, tq):
    B, T, _ = q.shape
    tile = lambda w: pl.BlockSpec((1, tq, w), lambda b, i: (b, i, 0))
    whole = pl.BlockSpec((KV_HEADS, 1, T, LANE), lambda b, i: (0, b, 0, 0))
    return pl.pallas_call(
        _nsa_prompt_body,
        grid=(B, T // tq),
        in_specs=[tile(C_MIX), whole, whole, pl.BlockSpec((1, KV_HEADS, sel.shape[2], tq), lambda b, i: (b, 0, 0, i)),
                  tile(C_MIX), tile(LANE)],
        out_specs=tile(C_MIX),
        out_shape=jax.ShapeDtypeStruct((B, T, C_MIX), F32),
        compiler_params=_params("parallel", "arbitrary"),
        name="nsa_prompt",
    )(q, kvs_g, kvw_g, sel, o_c, ng)


DEC_ROWS = 8


def _nsa_decode_body(mode, pages, pos_q0, pos_k0, pt_ref, q_ref, pool_ref, new_ref, *rest):
    if mode == "sel":
        selst_ref, selnew_ref, o_ref, buf, sem, m_scr, l_scr, acc_scr = rest
    else:
        oc_ref, os_ref, ng_ref, o_ref, buf, sem, m_scr, l_scr, acc_scr = rest
    st = pl.program_id(1)
    tq = DEC_ROWS
    tk = pages * pool_ref.shape[2]
    bps = tk // SEL_BLOCK

    slot = _paged_fetch(pt_ref, pool_ref, buf, sem, pages)

    @pl.when(st == 0)
    def _():
        m_scr[...] = jnp.full(m_scr.shape, NEG_INF, F32)
        l_scr[...] = jnp.zeros(l_scr.shape, F32)
        acc_scr[...] = jnp.zeros(acc_scr.shape, F32)

    q = q_ref[0] * (HEAD_DIM ** -0.5)
    t = pos_q0 + lax.broadcasted_iota(jnp.int32, (tq, 1), 0)

    def update(kv, feature_major, kpos, picked):
        for g in range(KV_HEADS):
            mask = (kpos <= t) & (picked(g) if mode == "sel" else (kpos > t - WINDOW))
            mask = jnp.concatenate([mask] * KV_GROUP, axis=0)
            qg = _stack_heads(q, g).astype(BF16)
            carry = (m_scr[g], l_scr[g], acc_scr[g])
            ks = slice(g * HEAD_DIM, (g + 1) * HEAD_DIM)
            vs = slice(LANE + g * HEAD_DIM, LANE + (g + 1) * HEAD_DIM)
            if feature_major:
                s, v = _dot(qg, kv[ks].astype(BF16)), kv[vs].astype(BF16)
            else:
                s, v = _dot_nt(qg, kv[:, ks].astype(BF16)), kv[:, vs].astype(BF16)
            m_scr[g], l_scr[g], acc_scr[g] = _softmax_step(carry, s, mask, v, feature_major)

    kidx = lax.broadcasted_iota(jnp.int32, (1, tk), 1)
    expand = (kidx // SEL_BLOCK == lax.broadcasted_iota(jnp.int32, (bps, 1), 0)).astype(BF16)
    update(buf[slot], True, pos_k0 + st * tk + kidx,
           lambda g: _dot(selst_ref[0, 0, :, g * bps:(g + 1) * bps].astype(BF16), expand) > 0.5)

    @pl.when(st == pl.num_programs(1) - 1)
    def _():
        update(new_ref[0], False, pos_q0 + lax.broadcasted_iota(jnp.int32, (1, tq), 1),
               lambda g: selnew_ref[0, :, g:g + 1] > 0.5)
        for g in range(KV_HEADS):
            o = acc_scr[g] / jnp.maximum(l_scr[g], 1e-30)
            if mode == "sel":
                for m in range(KV_GROUP):
                    h = g * KV_GROUP + m
                    o_ref[0, :, h * HEAD_DIM:(h + 1) * HEAD_DIM] = o[m * tq:(m + 1) * tq]
            else:
                _gate_mix(ng_ref[0], oc_ref[0], os_ref[0], o, g, tq, o_ref, os_stacked=False)


def _nsa_decode(mode, q, pool, page_table, kv_new, extras, pages, pos_q0, pos_k0):
    B, n_pages = page_table.shape
    rowblk = lambda a: pl.BlockSpec((1,) + a.shape[1:], lambda b, s, pt: (b,) + (0,) * (a.ndim - 1))
    stepblk = lambda a: pl.BlockSpec((1, 1) + a.shape[2:], lambda b, s, pt: (b, s, 0, 0))
    rows = KV_GROUP * DEC_ROWS
    return pl.pallas_call(
        functools.partial(_nsa_decode_body, mode, pages, pos_q0, pos_k0),
        grid_spec=pltpu.PrefetchScalarGridSpec(
            num_scalar_prefetch=1,
            grid=(B, n_pages // pages),
            in_specs=[rowblk(q), pl.BlockSpec(memory_space=pl.ANY), rowblk(kv_new)]
                     + [stepblk(e) if e.ndim == 4 else rowblk(e) for e in extras],
            out_specs=pl.BlockSpec((1, DEC_ROWS, C_MIX), lambda b, s, pt: (b, 0, 0)),
            scratch_shapes=[pltpu.VMEM((2, KV_W, pages * pool.shape[2]), F32), pltpu.SemaphoreType.DMA((2,)),
                            pltpu.VMEM((KV_HEADS, rows, 1), F32), pltpu.VMEM((KV_HEADS, rows, 1), F32),
                            pltpu.VMEM((KV_HEADS, rows, HEAD_DIM), F32)],
        ),
        out_shape=jax.ShapeDtypeStruct((B, DEC_ROWS, C_MIX), F32),
        compiler_params=_params("arbitrary", "arbitrary"),
        name="nsa_decode_" + mode,
    )(page_table, q, pool, kv_new, *extras)


ROUTE_W = LANE


def _merge_body(x_ref, oa_ref, ob_ref, mg_ref, wa_ref, wb_ref, wo_ref, n2_ref, wrh_ref, wrl_ref, br_ref,
                x1_ref, h_ref, route_ref):
    mg = mg_ref[...]
    merged = (mg[:, :D_MODEL] * _dot(oa_ref[...].astype(BF16), wa_ref[...])
              + mg[:, D_MODEL:] * _dot(ob_ref[...].astype(BF16), wb_ref[...]))
    x1 = x_ref[...] + _dot(merged.astype(BF16), wo_ref[...])
    x1_ref[...] = x1
    h = x1 * lax.rsqrt(jnp.mean(x1 * x1, axis=-1, keepdims=True) + RMS_EPS) * n2_ref[...]
    h_ref[...] = h.astype(BF16)
    hh, hl = _split2(h)
    logits = _dot(hh, wrh_ref[...]) + _dot(hl, wrh_ref[...]) + _dot(hh, wrl_ref[...]) + br_ref[...]
    lane = lax.broadcasted_iota(jnp.int32, (1, ROUTE_W), 1)
    first = lambda hit: jnp.min(jnp.where(hit, lane, ROUTE_W), axis=-1, keepdims=True)
    is_g = lane < N_GROUPS
    gl = jnp.where(is_g, logits, NEG_INF)
    gmax = jnp.max(gl, axis=-1, keepdims=True)
    g_sel = first(gl == gmax)
    g_w = 1.0 / jnp.sum(jnp.where(is_g, jnp.exp(gl - gmax), 0.0), axis=-1, keepdims=True)
    in_grp = (lane >= N_GROUPS) & (lane < N_GROUPS + N_EXPERTS) & (((lane - N_GROUPS) >> 3) == g_sel)
    el = jnp.where(in_grp, logits, NEG_INF)
    v1 = jnp.max(el, axis=-1, keepdims=True)
    i1 = first(el == v1)
    el2 = jnp.where(lane == i1, NEG_INF, el)
    v2 = jnp.max(el2, axis=-1, keepdims=True)
    i2 = first(el2 == v2)
    d = jnp.exp(v2 - v1)
    w1 = g_w / (1.0 + d)
    w2 = g_w * d / (1.0 + d)
    route_ref[...] = jnp.where(lane == 0, (i1 - N_GROUPS).astype(F32),
                               jnp.where(lane == 1, (i2 - N_GROUPS).astype(F32),
                                         jnp.where(lane == 2, w1, jnp.where(lane == 3, w2, 0.0))))


def _merge(x2d, oa, ob, mg, wa, wb, wo, norm2, w_rg, b_rg, w_re, b_re, tm):
    n = x2d.shape[0]
    const = lambda shape: pl.BlockSpec(shape, lambda i: (0,) * len(shape))
    row = lambda w: pl.BlockSpec((tm, w), lambda i: (i, 0))
    wr = jnp.zeros((D_MODEL, ROUTE_W), F32).at[:, :N_GROUPS].set(w_rg).at[:, N_GROUPS:N_GROUPS + N_EXPERTS].set(w_re)
    br = jnp.zeros((1, ROUTE_W), F32).at[0, :N_GROUPS].set(b_rg).at[0, N_GROUPS:N_GROUPS + N_EXPERTS].set(b_re)
    wrh, wrl = _split2(wr)
    return pl.pallas_call(
        _merge_body,
        grid=(n // tm,),
        in_specs=[row(D_MODEL), row(C_MIX), row(C_MIX), row(2 * D_MODEL), const((C_MIX, D_MODEL)),
                  const((C_MIX, D_MODEL)), const((D_MODEL, D_MODEL)), const((1, D_MODEL)),
                  const((D_MODEL, ROUTE_W)), const((D_MODEL, ROUTE_W)), const((1, ROUTE_W))],
        out_specs=[row(D_MODEL), row(D_MODEL), row(ROUTE_W)],
        out_shape=[jax.ShapeDtypeStruct((n, D_MODEL), F32), jax.ShapeDtypeStruct((n, D_MODEL), BF16),
                   jax.ShapeDtypeStruct((n, ROUTE_W), F32)],
        compiler_params=_params("parallel"),
        name="merge_route",
    )(x2d, oa, ob, mg, wa.astype(BF16), wb.astype(BF16), wo.astype(BF16), norm2.reshape(1, D_MODEL), wrh, wrl, br)


MOE_TB = 256


def _moe_body(be_ref, nu_ref, x_ref, wg_ref, wu_ref, wd_ref, y_ref, wg_b, wu_b, wd_b):
    i = pl.program_id(0)

    @pl.when((i == 0) | (be_ref[i] != be_ref[jnp.maximum(i - 1, 0)]))
    def _():
        wg_b[...] = wg_ref[0].astype(BF16)
        wu_b[...] = wu_ref[0].astype(BF16)
        wd_b[...] = wd_ref[0].astype(BF16)

    @pl.when(i < nu_ref[0])
    def _():
        x = x_ref[...]
        gate = _dot(x, wg_b[...])
        hid = gate * _sigmoid(gate) * _dot(x, wu_b[...])
        y_ref[...] = _dot(hid.astype(BF16), wd_b[...])

    @pl.when(i >= nu_ref[0])
    def _():
        y_ref[...] = jnp.zeros(y_ref.shape, F32)


def _moe_experts(xbuf, blk_e, n_used, wg, wu, wd):
    n_blk = blk_e.shape[0]
    wspec = lambda shape: pl.BlockSpec((1,) + shape, lambda i, be, nu: (be[i], 0, 0))
    return pl.pallas_call(
        _moe_body,
        grid_spec=pltpu.PrefetchScalarGridSpec(
            num_scalar_prefetch=2,
            grid=(n_blk,),
            in_specs=[pl.BlockSpec((MOE_TB, D_MODEL), lambda i, be, nu: (i, 0)), wspec((D_MODEL, D_EXPERT)),
                      wspec((D_MODEL, D_EXPERT)), wspec((D_EXPERT, D_MODEL))],
            out_specs=pl.BlockSpec((MOE_TB, D_MODEL), lambda i, be, nu: (i, 0)),
            scratch_shapes=[pltpu.VMEM((D_MODEL, D_EXPERT), BF16), pltpu.VMEM((D_MODEL, D_EXPERT), BF16),
                            pltpu.VMEM((D_EXPERT, D_MODEL), BF16)],
        ),
        out_shape=jax.ShapeDtypeStruct((n_blk * MOE_TB, D_MODEL), F32),
        compiler_params=_params("arbitrary"),
        name="moe_experts",
    )(blk_e, n_used, xbuf, wg, wu, wd)


def _moe_dispatch(h, route):
    n = h.shape[0]
    expert = route[:, :2].astype(jnp.int32).reshape(-1)
    wts = route[:, 2:4]
    n_slots = 2 * n
    n_blk = -(-n_slots // MOE_TB) + N_EXPERTS
    onehot = expert[:, None] == jnp.arange(N_EXPERTS, dtype=jnp.int32)[None, :]
    counts = jnp.sum(onehot, axis=0, dtype=jnp.int32)
    c_start = jnp.cumsum(counts) - counts
    padded = (counts + MOE_TB - 1) // MOE_TB * MOE_TB
    p_end = jnp.cumsum(padded)
    p_start = p_end - padded
    order = jnp.argsort(expert, stable=True).astype(jnp.int32)
    rank = jnp.argsort(order).astype(jnp.int32)
    dest = (rank + jnp.sum(jnp.where(onehot, (p_start - c_start)[None, :], 0), axis=1)).reshape(n, 2)
    blk_e = jnp.minimum(jnp.sum(p_end[None, :] <= (jnp.arange(n_blk, dtype=jnp.int32) * MOE_TB)[:, None], axis=1),
                        N_EXPERTS - 1).astype(jnp.int32)
    n_used = (p_end[-1:] // MOE_TB).astype(jnp.int32)
    k_in_e = jnp.arange(n_blk * MOE_TB, dtype=jnp.int32) - jnp.repeat(p_start[blk_e], MOE_TB)
    src = order[jnp.clip(jnp.repeat(c_start[blk_e], MOE_TB) + k_in_e, 0, n_slots - 1)]
    row_tok = jnp.where(k_in_e < jnp.repeat(counts[blk_e], MOE_TB), src // 2, 0)
    return h[row_tok], blk_e, n_used, dest, wts


def _moe_apply(x1, dispatch, wg, wu, wd):
    xbuf, blk_e, n_used, dest, wts = dispatch
    ybuf = _moe_experts(xbuf, blk_e, n_used, wg, wu, wd)
    return x1 + wts[:, 0:1] * ybuf[dest[:, 0]] + wts[:, 1:2] * ybuf[dest[:, 1]]


def kernel(x_prompt, x_sample, cache_cmp_kv, cache_slc_kv, cache_win_kv, state_wkv, state_shift, page_table, norm1, w_in, mu_shift, w0, w_decay_up, a0, w_iclr_up, w_gate_up, k_k, k_a, r_k, ln_x_w, ln_x_b, q_norm, kc_norm, ks_norm, kw_norm, cmp_pe_k, cmp_w1_k, cmp_b1_k, cmp_w2_k, cmp_pe_v, cmp_w1_v, cmp_b1_v, cmp_w2_v, w_branch_a, w_branch_b, w_out, norm2, w_route_group, b_route_group, w_route_expert, b_route_expert, w_exp_gate, w_exp_up, w_exp_down):
    assert norm1.shape[0] == 1, "single-layer trunk"
    Bp, Tp, _ = x_prompt.shape
    Bs, Ts, _ = x_sample.shape
    n_pool = cache_cmp_kv.shape[1]
    past = page_table.shape[1] * PAGE_SIZE
    n_buf = cache_win_kv.shape[2]
    kv5 = lambda a, b, t: a.reshape(1, b, t, 2, KV_HEADS, HEAD_DIM)
    kv5t = lambda a: a.reshape(a.shape[0], 2, KV_HEADS, HEAD_DIM, a.shape[2]).transpose(0, 4, 1, 2, 3)[None]

    w_pad = _pad_w_in(w_in[0])
    rw_p = (mu_shift[0], w0[0], w_decay_up[0], a0[0], w_iclr_up[0], w_gate_up[0], k_k[0], k_a[0],
            r_k[0].reshape(-1), ln_x_w[0], ln_x_b[0])
    wc, pea, peb, b1, w2b = _cmp_weights(cmp_pe_k[0], cmp_w1_k[0], cmp_b1_k[0], cmp_w2_k[0],
                                         cmp_pe_v[0], cmp_w1_v[0], cmp_b1_v[0], cmp_w2_v[0])
    merge_p = (w_branch_a[0], w_branch_b[0], w_out[0], norm2[0], w_route_group[0], b_route_group[0],
               w_route_expert[0], b_route_expert[0])
    moe_w = (w_exp_gate[0], w_exp_up[0], w_exp_down[0])

    xp = x_prompt.reshape(Bp * Tp, D_MODEL)
    zrw, q, kvc, _, _, ng, mg, kvs_g, kvw_g, kvc_t, kvs_t, kvw_t = _in_proj(xp, norm1[0], w_pad, q_norm[0], ks_norm[0],
                                                                            kw_norm[0], 512, Tp)
    zrw3 = zrw.reshape(Bp, Tp, RW_PAD)
    oa, wkv_p = _rwkv(zrw3, jnp.zeros((Bp, 1, RW_PAD), F32), jnp.zeros((Bp, N_HEADS, HEAD_DIM, HEAD_DIM), F32),
                      Tp, *rw_p)
    ab = _cmp_proj(kvc.reshape(Bp * Tp // CMP_STRIDE, CH_W), pea, peb, wc, min(256, Bp * Tp // CMP_STRIDE))
    kc, vc = _cmp_finish(ab.reshape(Bp, Tp // CMP_STRIDE, -1), b1, w2b, kc_norm[0])
    q3 = q.reshape(Bp, Tp, C_MIX)
    o_c, sel = _cmp_attn(q3, kc, vc, 0, (Tp - CMP_LEN) // CMP_STRIDE + 1, Tp // SEL_BLOCK, 128, True)
    ob = _nsa_prompt(q3, kvs_g.reshape(KV_HEADS, Bp, Tp, LANE), kvw_g.reshape(KV_HEADS, Bp, Tp, LANE), sel, o_c,
                     ng.reshape(Bp, Tp, LANE), 128)
    x1, h, route = _merge(xp, oa.reshape(Bp * Tp, C_MIX), ob.reshape(Bp * Tp, C_MIX), mg, *merge_p, 512)
    disp_p = _moe_dispatch(h, route)
    keep_p = min(WINDOW, Tp)

    xs = x_sample.reshape(Bs * Ts, D_MODEL)
    zrw_s, q_s, kvc_s, kvs_s, kvw_s, ng_s, mg_s = _in_proj(xs, norm1[0], w_pad, q_norm[0], ks_norm[0], kw_norm[0],
                                                           Bs * Ts, Bs * Ts)[:7]
    zrw_s3 = zrw_s.reshape(Bs, Ts, RW_PAD)
    oa_s, wkv_s = _rwkv(jnp.pad(zrw_s3, ((0, 0), (0, RW_CHUNK - Ts), (0, 0))),
                        jnp.pad(state_shift[0], ((0, 0), (0, RW_PAD - RW_IN)))[:, None], state_wkv[0], Ts, *rw_p)
    nc_s = (past + Ts - CMP_LEN) // CMP_STRIDE + 1
    assert (nc_s + CMP_LEN // CMP_STRIDE - 1) * CMP_STRIDE <= past, "compression blocks only cover cached rows"
    feature_major = lambda c: c.transpose(0, 2, 3, 4, 1).reshape(c.shape[0], KV_W, c.shape[1])
    ab_s = _cmp_proj_paged(feature_major(cache_cmp_kv[0]), page_table, pea, peb, wc)
    kc_s, vc_s = _cmp_finish(ab_s, b1, w2b, kc_norm[0])
    rows8 = lambda a: jnp.pad(a.reshape(Bs, Ts, -1), ((0, 0), (0, DEC_ROWS - Ts), (0, 0)))
    q8 = rows8(q_s)
    oc_s, sel_s = _cmp_attn(q8, kc_s, vc_s, past, nc_s, -(-(past + Ts) // SEL_BLOCK), DEC_ROWS, False)
    assert past % SEL_BLOCK == 0 and Ts <= SEL_BLOCK, "the new rows share one selection block"
    n_st = page_table.shape[1] // CMP_PAGES
    sel4 = sel_s.reshape(Bs, DEC_ROWS, KV_HEADS, -1)
    sel_steps = sel4[..., :past // SEL_BLOCK].reshape(Bs, DEC_ROWS, KV_HEADS, n_st, -1)
    sel_steps = sel_steps.transpose(0, 3, 1, 2, 4).reshape(Bs, n_st, DEC_ROWS, -1)
    sel_new = jnp.pad(sel4[..., past // SEL_BLOCK], ((0, 0), (0, 0), (0, LANE - KV_HEADS)))
    os_s = _nsa_decode("sel", q8, feature_major(cache_slc_kv[0]), page_table, rows8(kvs_s),
                       (sel_steps, sel_new), CMP_PAGES, past, 0)
    ob_s = _nsa_decode("win", q8, feature_major(cache_win_kv[0]), jnp.arange(Bs, dtype=jnp.int32).reshape(Bs, 1),
                       rows8(kvw_s), (oc_s, os_s, rows8(ng_s)), 1, past, past - n_buf)
    x1_s, h_s, route_s = _merge(xs, oa_s[:, :Ts].reshape(Bs * Ts, C_MIX), ob_s[:, :Ts].reshape(Bs * Ts, C_MIX), mg_s,
                                *merge_p, Bs * Ts)
    disp_p, x1_s = lax.optimization_barrier((disp_p, x1_s))
    y_p = _moe_apply(x1, disp_p, *moe_w).reshape(Bp, Tp, D_MODEL)
    y_s = _moe_apply(x1_s, _moe_dispatch(h_s, route_s), *moe_w).reshape(Bs, Ts, D_MODEL)
    keep_s = min(WINDOW, n_buf + Ts)
    win_s = jnp.concatenate([cache_win_kv[0].reshape(Bs, n_buf, KV_W), kvw_s.reshape(Bs, Ts, KV_W)], axis=1)

    return (y_p, y_s,
            kv5t(kvc_t), kv5t(kvs_t), kv5t(kvw_t[:, :, Tp - keep_p:]),
            wkv_p[None], zrw3[:, -1, :RW_IN][None],
            kv5(kvc_s, Bs, Ts), kv5(kvs_s, Bs, Ts), kv5(win_s[:, n_buf + Ts - keep_s:], Bs, keep_s),
            wkv_s[None], zrw_s3[:, -1, :RW_IN][None])
```

```python
import functools
import math

import jax
import jax.numpy as jnp
from jax import lax
from jax.experimental import pallas as pl
from jax.experimental.pallas import tpu as pltpu

F32 = jnp.float32
BF16 = jnp.bfloat16

D_MODEL = 1024
HEAD_DIM = 64
N_HEADS = 8
C_MIX = N_HEADS * HEAD_DIM
R_DECAY, R_ICLR, R_GATE = 32, 32, 96
RW_IN = 3 * C_MIX + R_DECAY + R_ICLR + R_GATE
KV_HEADS = 2
KV_GROUP = N_HEADS // KV_HEADS
KV_W = 2 * KV_HEADS * HEAD_DIM
CMP_LEN, CMP_STRIDE = 32, 16
SEL_BLOCK = 64
N_SEL = 16
WINDOW = 512
PAGE_SIZE = 128
N_GROUPS, EXPERTS_PER_GROUP = 4, 8
N_EXPERTS = N_GROUPS * EXPERTS_PER_GROUP
D_EXPERT = D_MODEL // 2
RMS_EPS = 1e-6
GN_EPS = 64e-5
NEG_INF = -1e30
FORCE_SCORE = 1e6

LANE = 128
VMEM_LIMIT = 56 * 1024 * 1024

RW_PAD = 1792
OFF_Q = RW_PAD
OFF_KVC = OFF_Q + C_MIX
OFF_KVS = OFF_KVC + KV_W
OFF_KVW = OFF_KVS + KV_W
OFF_NG = OFF_KVW + KV_W
OFF_MG = OFF_NG + LANE
N_IN_PAD = OFF_MG + 2 * D_MODEL
RW_TAIL = 3 * C_MIX


def _params(*sem):
    return pltpu.CompilerParams(dimension_semantics=sem, vmem_limit_bytes=VMEM_LIMIT)


def _dot(a, b):
    return jnp.dot(a, b, preferred_element_type=F32)


def _dot_nt(a, b):
    return lax.dot_general(a, b, (((1,), (1,)), ((), ())), preferred_element_type=F32)


def _dot_tn(a, b):
    return lax.dot_general(a, b, (((0,), (0,)), ((), ())), preferred_element_type=F32)


def _split2(x):
    hi = x.astype(BF16)
    lo = (x - hi.astype(F32)).astype(BF16)
    return hi, lo


def _split3(x):
    hi = x.astype(BF16)
    r1 = x - hi.astype(F32)
    mid = r1.astype(BF16)
    lo = (r1 - mid.astype(F32)).astype(BF16)
    return hi, mid, lo


def _gsum(y, g):
    hi, lo = _split2(y)
    return _dot(hi, g) + _dot(lo, g)


def _block_ones(n, blk):
    i = jnp.arange(n) // blk
    return (i[:, None] == i[None, :]).astype(BF16)


def _sigmoid(x):
    return 1.0 / (1.0 + jnp.exp(-x))


def _inproj_body(x_ref, n1_ref, w_ref, g512_ref, g128_ref, qn_ref, ksn_ref, kwn_ref,
                 zrw_ref, q_ref, kvc_ref, kvs_ref, kvw_ref, ng_ref, mg_ref, kvsg_ref, kvwg_ref,
                 kvct_ref, kvst_ref, kvwt_ref):
    x = x_ref[...]
    ms = jnp.mean(x * x, axis=-1, keepdims=True)
    xn = (x * lax.rsqrt(ms + RMS_EPS) * n1_ref[...]).astype(BF16)

    def proj(a, b):
        return _dot(xn, w_ref[:, a:b])

    zrw_ref[...] = proj(0, RW_PAD)
    q = proj(OFF_Q, OFF_KVC)
    q_ref[...] = q * lax.rsqrt(_gsum(q * q, g512_ref[...]) * (1.0 / HEAD_DIM) + RMS_EPS) * qn_ref[...]
    kvc = proj(OFF_KVC, OFF_KVS)
    kvc_ref[...] = kvc
    kvct_ref[0] = kvc.T
    for off, nref, oref, gref, tref in ((OFF_KVS, ksn_ref, kvs_ref, kvsg_ref, kvst_ref),
                                        (OFF_KVW, kwn_ref, kvw_ref, kvwg_ref, kvwt_ref)):
        kv = proj(off, off + KV_W)
        k = kv[:, :LANE]
        kn = k * lax.rsqrt(_gsum(k * k, g128_ref[...]) * (1.0 / HEAD_DIM) + RMS_EPS) * nref[...]
        v = kv[:, LANE:]
        oref[:, :LANE] = kn
        oref[:, LANE:] = v
        tref[0, :LANE] = kn.T
        tref[0, LANE:] = v.T
        for g in range(KV_HEADS):
            hs = slice(g * HEAD_DIM, (g + 1) * HEAD_DIM)
            gref[g] = jnp.concatenate([kn[:, hs], v[:, hs]], axis=1).astype(BF16)
    ng_ref[...] = _sigmoid(proj(OFF_NG, OFF_MG))
    mg_ref[...] = _sigmoid(proj(OFF_MG, N_IN_PAD))


def _in_proj(x2d, norm1, w_pad, q_norm, ks_norm, kw_norm, tm, seq_len):
    n = x2d.shape[0]
    seq_tiles = seq_len // tm
    const = lambda shape: pl.BlockSpec(shape, lambda i: (0,) * len(shape))
    row = lambda w: pl.BlockSpec((tm, w), lambda i: (i, 0))
    widths = (RW_PAD, C_MIX, KV_W, KV_W, KV_W, LANE, 2 * D_MODEL)
    return pl.pallas_call(
        _inproj_body,
        grid=(n // tm,),
        in_specs=[row(D_MODEL), const((1, D_MODEL)), const((D_MODEL, N_IN_PAD)), const((C_MIX, C_MIX)),
                  const((LANE, LANE)), const((1, C_MIX)), const((1, LANE)), const((1, LANE))],
        out_specs=[row(w) for w in widths] + [pl.BlockSpec((KV_HEADS, tm, LANE), lambda i: (0, i, 0))] * 2
                  + [pl.BlockSpec((1, KV_W, tm), lambda i: (i // seq_tiles, 0, i % seq_tiles))] * 3,
        out_shape=[jax.ShapeDtypeStruct((n, w), F32) for w in widths]
                  + [jax.ShapeDtypeStruct((KV_HEADS, n, LANE), BF16)] * 2
                  + [jax.ShapeDtypeStruct((n // seq_len, KV_W, seq_len), F32)] * 3,
        compiler_params=_params("parallel"),
        name="in_proj",
    )(x2d, norm1.reshape(1, D_MODEL), w_pad, _block_ones(C_MIX, HEAD_DIM), _block_ones(LANE, HEAD_DIM),
      jnp.tile(q_norm, N_HEADS).reshape(1, C_MIX), jnp.tile(ks_norm, KV_HEADS).reshape(1, LANE),
      jnp.tile(kw_norm, KV_HEADS).reshape(1, LANE))


def _pad_w_in(w_in):
    d = w_in.shape[0]
    z = lambda n: jnp.zeros((d, n), w_in.dtype)
    o_q = RW_IN
    o_ng = o_q + C_MIX + 3 * KV_W
    o_mg = o_ng + 3 * N_HEADS
    return jnp.concatenate([w_in[:, :RW_IN], z(RW_PAD - RW_IN), w_in[:, o_q:o_ng], w_in[:, o_ng:o_mg],
                            z(LANE - 3 * N_HEADS), w_in[:, o_mg:]], axis=1).astype(BF16)


RW_CHUNK = 64
RW_HSTACK = 4
RW_ROWS = 4


def _rwkv_body(t_valid, z_ref, sp_ref, s0_ref, mu_ref, w0_ref, a0_ref, kk_ref, ka_ref, rk_ref, lnw_ref, lnb_ref,
               wd_ref, wi_ref, wg_ref, g512_ref, o_ref, s_ref, prev_scr):
    @pl.when(pl.program_id(1) == 0)
    def _():
        s_ref[...] = s0_ref[...]
        prev_scr[...] = sp_ref[...]

    prep = [_rwkv_prep(t_valid, bi, z_ref, mu_ref, w0_ref, a0_ref, kk_ref, ka_ref, rk_ref, wd_ref, wi_ref, wg_ref,
                       g512_ref, prev_scr) for bi in range(z_ref.shape[0])]
    ys = _rwkv_chains(prep, s_ref)
    g512 = g512_ref[...]
    for bi, (p, y) in enumerate(zip(prep, ys)):
        yc = y - _gsum(y, g512) * (1.0 / HEAD_DIM)
        var = _gsum(yc * yc, g512) * (1.0 / HEAD_DIM)
        yn = yc * lax.rsqrt(var + GN_EPS) * lnw_ref[...] + lnb_ref[...]
        o_ref[bi] = (yn + _gsum(p["rkk"], g512) * p["v"]) * p["g"]


def _rwkv_prep(t_valid, bi, z_ref, mu_ref, w0_ref, a0_ref, kk_ref, ka_ref, rk_ref, wd_ref, wi_ref, wg_ref,
               g512_ref, prev_scr):
    C = RW_CHUNK
    c = pl.program_id(1)
    z = z_ref[bi]
    row = lax.broadcasted_iota(jnp.int32, (C, 1), 0)
    z_prev = jnp.where(row == 0, prev_scr[bi], pltpu.roll(z, 1, axis=0))
    prev_scr[bi] = z[C - 1:C]
    zm = z + (z_prev - z) * mu_ref[...]
    r = zm[:, 0:C_MIX]
    k = zm[:, C_MIX:2 * C_MIX]
    v = zm[:, 2 * C_MIX:3 * C_MIX]
    tail = zm[:, RW_TAIL:RW_PAD]
    w_lora = _dot(jnp.tanh(tail).astype(BF16), wd_ref[...])
    a_lora = _dot(tail.astype(BF16), wi_ref[...])
    g = _dot(_sigmoid(tail).astype(BF16), wg_ref[...])
    u = -(w0_ref[...] + w_lora)
    softplus = jnp.maximum(u, 0.0) + jnp.log(1.0 + jnp.exp(-jnp.abs(u)))
    w_log = -softplus - 0.5
    valid = (c * C + row) < t_valid
    ld = jnp.where(valid, -jnp.exp(w_log), 0.0)
    a = _sigmoid(a0_ref[...] + a_lora)
    kk = k * kk_ref[...]
    kk = kk / jnp.maximum(jnp.sqrt(_gsum(kk * kk, g512_ref[...])), 1e-12)
    k2 = k * (1.0 + (a - 1.0) * ka_ref[...])

    ci = lax.broadcasted_iota(jnp.int32, (C, C), 0)
    cj = lax.broadcasted_iota(jnp.int32, (C, C), 1)
    tri = (ci >= cj).astype(BF16)
    h1, h2, h3 = _split3(ld)
    cl = _dot(tri, h1) + _dot(tri, h2) + _dot(tri, h3)
    p_in = jnp.exp(cl)
    p_inv = jnp.exp(-cl)
    r_t = r * p_in
    a_t = -kk * jnp.exp(cl - ld)
    b_t = jnp.where(valid, kk * a * p_inv, 0.0)
    k_t = jnp.where(valid, k2 * p_inv, 0.0)
    p_end = p_in[C - 1:C]

    return dict(a_t=a_t, r_t=r_t, b_t=b_t, k_t=k_t, v=v, p_end=p_end, rkk=r * k2 * rk_ref[...], g=g)


def _rwkv_chains(prep, s_ref):
    C = RW_CHUNK
    HS = RW_HSTACK
    R = HS * C
    ri = lax.broadcasted_iota(jnp.int32, (R, R), 0)
    rj = lax.broadcasted_iota(jnp.int32, (R, R), 1)
    same = (ri // C) == (rj // C)
    lower = same & (ri > rj)
    lower_eq = same & (ri >= rj)
    eye = (ri == rj).astype(F32)
    chains = [(bi, hg) for bi in range(len(prep)) for hg in range(N_HEADS // HS)]
    heads = lambda hg: [hg * HS + m for m in range(HS)]
    stack = lambda x, hg: jnp.concatenate([x[:, h * HEAD_DIM:(h + 1) * HEAD_DIM] for h in heads(hg)],
                                          axis=0).astype(BF16)
    A, Rr, Bm, Km, V = ([stack(prep[bi][name], hg) for bi, hg in chains]
                        for name in ("a_t", "r_t", "b_t", "k_t", "v"))
    S4 = [_dot_nt(jnp.concatenate([a, r], axis=0), jnp.concatenate([b, k], axis=0))
          for a, r, b, k in zip(A, Rr, Bm, Km)]
    L = [jnp.where(lower, s[:R, :R], 0.0) for s in S4]
    Lak = [jnp.where(lower, s[:R, R:], 0.0).astype(BF16) for s in S4]
    Mr = [jnp.concatenate([jnp.where(lower_eq, s[R:, :R], 0.0), jnp.where(lower_eq, s[R:, R:], 0.0)],
                          axis=1).astype(BF16) for s in S4]
    s0 = [[s_ref[bi, h] for h in heads(hg)] for bi, hg in chains]
    on_state = [[_dot_nt(jnp.concatenate([a[m * C:(m + 1) * C], r[m * C:(m + 1) * C]], axis=0),
                         s0c[m].astype(BF16)) for m in range(HS)]
                for a, r, s0c in zip(A, Rr, s0)]
    rhs = [jnp.concatenate([o[:C] for o in os], axis=0) + _dot(lak, v) for os, lak, v in zip(on_state, Lak, V)]
    X = [eye + l for l in L]
    Lp = L
    for _ in range(int(math.log2(C)) - 1):
        Lpb = [lp.astype(BF16) for lp in Lp]
        Lp = [_dot(lp, lp) for lp in Lpb]
        X = [x + _dot(x.astype(BF16), lp.astype(BF16)) for x, lp in zip(X, Lp)]
    U = [_dot(x.astype(BF16), r.astype(BF16)).astype(BF16) for x, r in zip(X, rhs)]
    Y = [jnp.concatenate([o[C:] for o in os], axis=0) + _dot(mr, jnp.concatenate([u, v], axis=0))
         for os, mr, u, v in zip(on_state, Mr, U, V)]
    for ci, (bi, hg) in enumerate(chains):
        for m, h in enumerate(heads(hg)):
            rs = slice(m * C, (m + 1) * C)
            upd = _dot_tn(jnp.concatenate([U[ci][rs], V[ci][rs]], axis=0),
                          jnp.concatenate([Bm[ci][rs], Km[ci][rs]], axis=0))
            s_ref[bi, h] = (s0[ci][m] + upd) * prep[bi]["p_end"][:, h * HEAD_DIM:(h + 1) * HEAD_DIM]
    return [jnp.concatenate([Y[ci][m * C:(m + 1) * C] for ci, (cb, _) in enumerate(chains) if cb == bi
                             for m in range(HS)], axis=1) for bi in range(len(prep))]


def _rwkv(z_rw, shift_prev, s0, t_valid, mu, w0, wd, a0, wi, wg, k_k, k_a, r_k, ln_w, ln_b):
    B, T, _ = z_rw.shape
    C = RW_CHUNK
    const = lambda shape: pl.BlockSpec(shape, lambda b, c: (0,) * len(shape))
    vec = lambda p: p.reshape(1, C_MIX)
    pad_rows = lambda w, off: jnp.zeros((RW_PAD - RW_TAIL, C_MIX), F32).at[off:off + w.shape[0]].set(w).astype(BF16)
    nb = math.gcd(B, RW_ROWS)
    state_spec = pl.BlockSpec((nb, N_HEADS, HEAD_DIM, HEAD_DIM), lambda b, c: (b, 0, 0, 0))
    return pl.pallas_call(
        functools.partial(_rwkv_body, t_valid),
        grid=(B // nb, T // C),
        in_specs=[pl.BlockSpec((nb, C, RW_PAD), lambda b, c: (b, c, 0)),
                  pl.BlockSpec((nb, 1, RW_PAD), lambda b, c: (b, 0, 0)),
                  state_spec, const((1, RW_PAD))] + [const((1, C_MIX))] * 7
                 + [const((RW_PAD - RW_TAIL, C_MIX))] * 3 + [const((C_MIX, C_MIX))],
        out_specs=[pl.BlockSpec((nb, C, C_MIX), lambda b, c: (b, c, 0)), state_spec],
        out_shape=[jax.ShapeDtypeStruct((B, T, C_MIX), F32),
                   jax.ShapeDtypeStruct((B, N_HEADS, HEAD_DIM, HEAD_DIM), F32)],
        scratch_shapes=[pltpu.VMEM((nb, 1, RW_PAD), F32)],
        compiler_params=_params("parallel", "arbitrary"),
        name="rwkv7",
    )(z_rw, shift_prev, s0, jnp.pad(mu, (0, RW_PAD - RW_IN)).reshape(1, RW_PAD), vec(w0), vec(a0), vec(k_k),
      vec(k_a), vec(r_k), vec(ln_w), vec(ln_b), pad_rows(wd, 0), pad_rows(wi, R_DECAY),
      pad_rows(wg, R_DECAY + R_ICLR), _block_ones(C_MIX, HEAD_DIM))


CH_W = CMP_STRIDE * KV_W
N_SLOT = 2 * KV_HEADS


def _cmp_weights(pe_k, w1_k, b1_k, w2_k, pe_v, w1_v, b1_v, w2_v):
    eye = jnp.eye(N_SLOT, dtype=F32)
    w1 = jnp.stack([w1_k, w1_k, w1_v, w1_v])
    pe = jnp.stack([pe_k, pe_k, pe_v, pe_v])
    halves = []
    pes = []
    for r in range(CMP_LEN // CMP_STRIDE):
        ls = slice(r * CMP_STRIDE, (r + 1) * CMP_STRIDE)
        halves.append(jnp.einsum('sldf,st->lsdtf', w1[:, ls], eye).reshape(CH_W, N_SLOT * HEAD_DIM))
        pes.append(jnp.transpose(pe[:, ls], (1, 0, 2)).reshape(1, CH_W))
    wc = jnp.concatenate(halves, axis=1).astype(BF16)
    w2 = jnp.stack([w2_k, w2_k, w2_v, w2_v])
    w2b = jnp.einsum('sfd,st->sftd', w2, eye).reshape(N_SLOT * HEAD_DIM, N_SLOT * HEAD_DIM).astype(BF16)
    b1 = jnp.concatenate([b1_k, b1_k, b1_v, b1_v]).reshape(1, N_SLOT * HEAD_DIM)
    return wc, pes[0], pes[1], b1, w2b


def _cmp_proj_body(ch_ref, pea_ref, peb_ref, wc_ref, ab_ref):
    ch = ch_ref[...]
    w = N_SLOT * HEAD_DIM
    ab_ref[:, :w] = _dot((ch + pea_ref[...]).astype(BF16), wc_ref[:, :w])
    ab_ref[:, w:] = _dot((ch + peb_ref[...]).astype(BF16), wc_ref[:, w:])


def _cmp_proj(chunks, pea, peb, wc, tr):
    n = chunks.shape[0]
    const = lambda shape: pl.BlockSpec(shape, lambda i: (0,) * len(shape))
    return pl.pallas_call(
        _cmp_proj_body,
        grid=(n // tr,),
        in_specs=[pl.BlockSpec((tr, CH_W), lambda i: (i, 0)), const((1, CH_W)), const((1, CH_W)),
                  const((CH_W, 2 * N_SLOT * HEAD_DIM))],
        out_specs=pl.BlockSpec((tr, 2 * N_SLOT * HEAD_DIM), lambda i: (i, 0)),
        out_shape=jax.ShapeDtypeStruct((n, 2 * N_SLOT * HEAD_DIM), F32),
        compiler_params=_params("parallel"),
        name="cmp_proj",
    )(chunks, pea, peb, wc)


CMP_PAGES = 16


def _paged_fetch(pt_ref, pool_ref, buf, sem, pages):
    b = pl.program_id(0)
    st = pl.program_id(1)
    n_st = pl.num_programs(1)
    step = b * n_st + st
    slot = step % 2
    ptok = pool_ref.shape[2]

    def copies(bb, stt, sl):
        dst = (lambda i: buf.at[sl, i]) if len(buf.shape) == 4 else (lambda i: buf.at[sl, :, pl.ds(i * ptok, ptok)])
        return [pltpu.make_async_copy(pool_ref.at[pt_ref[bb, stt * pages + i]], dst(i), sem.at[sl])
                for i in range(pages)]

    @pl.when(step == 0)
    def _():
        for c in copies(0, 0, 0):
            c.start()

    @pl.when(step + 1 < pl.num_programs(0) * n_st)
    def _():
        wrap = st + 1 == n_st
        for c in copies(jnp.where(wrap, b + 1, b), jnp.where(wrap, 0, st + 1), 1 - slot):
            c.start()

    for c in copies(b, st, slot):
        c.wait()
    return slot


def _cmp_proj_paged_body(pt_ref, cache_ref, pe_ref, wl_ref, ab_ref, buf, sem, rows_scr):
    slot = _paged_fetch(pt_ref, cache_ref, buf, sem, CMP_PAGES)
    n_chunks = CMP_PAGES * PAGE_SIZE // CMP_STRIDE
    w = N_SLOT * HEAD_DIM
    for i in range(CMP_PAGES):
        for j in range(rows_scr.shape[0]):
            rows_scr[j, i * PAGE_SIZE:(i + 1) * PAGE_SIZE, :] = buf[slot, i, j * LANE:(j + 1) * LANE, :].T
    acc = [jnp.zeros((n_chunks, w), F32) for _ in range(CMP_LEN // CMP_STRIDE)]
    for l in range(CMP_STRIDE):
        x = jnp.concatenate([rows_scr[j, pl.ds(l, n_chunks, stride=CMP_STRIDE), :] for j in range(rows_scr.shape[0])],
                            axis=1)
        for r in range(CMP_LEN // CMP_STRIDE):
            acc[r] = acc[r] + _dot((x + pe_ref[r, l]).astype(BF16), wl_ref[l, :, r * w:(r + 1) * w])
    for r in range(CMP_LEN // CMP_STRIDE):
        ab_ref[0, :, r * w:(r + 1) * w] = acc[r]


def _cmp_proj_paged(cache, page_table, pea, peb, wc):
    B, n_pages = page_table.shape
    rows = PAGE_SIZE // CMP_STRIDE
    w = N_SLOT * HEAD_DIM
    const = lambda shape: pl.BlockSpec(shape, lambda b, g, pt: (0,) * len(shape))
    pe = jnp.stack([pea, peb]).reshape(CMP_LEN // CMP_STRIDE, CMP_STRIDE, 1, w)
    wl = wc.reshape(CMP_STRIDE, w, 2 * w)
    return pl.pallas_call(
        _cmp_proj_paged_body,
        grid_spec=pltpu.PrefetchScalarGridSpec(
            num_scalar_prefetch=1,
            grid=(B, n_pages // CMP_PAGES),
            in_specs=[pl.BlockSpec(memory_space=pl.ANY), const(pe.shape), const(wl.shape)],
            out_specs=pl.BlockSpec((1, CMP_PAGES * rows, 2 * w), lambda b, g, pt: (b, g, 0)),
            scratch_shapes=[pltpu.VMEM((2, CMP_PAGES, KV_W, PAGE_SIZE), F32), pltpu.SemaphoreType.DMA((2,)),
                            pltpu.VMEM((KV_W // LANE, CMP_PAGES * PAGE_SIZE, LANE), F32)],
        ),
        out_shape=jax.ShapeDtypeStruct((B, n_pages * rows, 2 * w), F32),
        compiler_params=_params("arbitrary", "arbitrary"),
        name="cmp_proj_paged",
    )(page_table, cache, pe, wl)


def _cmp_finish_body(ab_ref, b1_ref, w2_ref, kcn_ref, g128_ref, kc_ref, vc_ref):
    ab = ab_ref[0]
    n = ab.shape[0]
    w = N_SLOT * HEAD_DIM
    pre = ab[:, :w] + pltpu.roll(ab[:, w:], n - 1, axis=0) + b1_ref[...]
    hid = pre * _sigmoid(pre)
    out = _dot(hid.astype(BF16), w2_ref[...])
    k = out[:, :LANE]
    kc_ref[0] = k * lax.rsqrt(_gsum(k * k, g128_ref[...]) * (1.0 / HEAD_DIM) + RMS_EPS) * kcn_ref[...]
    vc_ref[0] = out[:, LANE:]


def _cmp_finish(ab, b1, w2b, kc_norm):
    B, n, _ = ab.shape
    const = lambda shape: pl.BlockSpec(shape, lambda b: (0,) * len(shape))
    w = N_SLOT * HEAD_DIM
    return pl.pallas_call(
        _cmp_finish_body,
        grid=(B,),
        in_specs=[pl.BlockSpec((1, n, 2 * w), lambda b: (b, 0, 0)), const((1, w)), const((w, w)), const((1, LANE)),
                  const((LANE, LANE))],
        out_specs=[pl.BlockSpec((1, n, LANE), lambda b: (b, 0, 0))] * 2,
        out_shape=[jax.ShapeDtypeStruct((B, n, LANE), F32)] * 2,
        compiler_params=_params("parallel"),
        name="cmp_finish",
    )(ab, b1, w2b, jnp.tile(kc_norm, KV_HEADS).reshape(1, LANE), _block_ones(LANE, HEAD_DIM))


def _cmp_attn_body(pos0, nc, nb, nbp, blocks_on_rows, q_ref, kc_ref, vc_ref, ovl_ref, o_ref, sel_ref):
    tq = q_ref.shape[1]
    ncp = kc_ref.shape[1]
    q = q_ref[0] * (HEAD_DIM ** -0.5)
    t0 = pos0 + pl.program_id(1) * tq
    t = t0 + lax.broadcasted_iota(jnp.int32, (tq, 1), 0)
    cidx = lax.broadcasted_iota(jnp.int32, (1, ncp), 1)
    cmask = (cidx * CMP_STRIDE + (CMP_LEN - 1) <= t) & (cidx < nc)
    if blocks_on_rows:
        tt = t0 + lax.broadcasted_iota(jnp.int32, (1, tq), 1)
        j = lax.broadcasted_iota(jnp.int32, (nbp, 1), 0)
    else:
        tt = t
        j = lax.broadcasted_iota(jnp.int32, (1, nbp), 1)
    cur = tt // SEL_BLOCK
    valid = (j * SEL_BLOCK <= tt) & (j < nb)
    forced = (j == 0) | (j == cur) | (j == cur - 1)
    for g in range(KV_HEADS):
        kcg = kc_ref[0, :, g * HEAD_DIM:(g + 1) * HEAD_DIM].astype(BF16)
        vcg = vc_ref[0, :, g * HEAD_DIM:(g + 1) * HEAD_DIM].astype(BF16)
        s = _dot_nt(_stack_heads(q, g).astype(BF16), kcg).reshape(KV_GROUP, tq, ncp)
        s = jnp.where(cmask[None], s, NEG_INF)
        e = jnp.where(cmask[None], jnp.exp(s - jnp.max(s, axis=-1, keepdims=True)), 0.0)
        p = e / jnp.maximum(jnp.sum(e, axis=-1, keepdims=True), 1e-30)
        o = _dot(p.reshape(KV_GROUP * tq, ncp).astype(BF16), vcg)
        for m in range(KV_GROUP):
            h = g * KV_GROUP + m
            o_ref[0, :, h * HEAD_DIM:(h + 1) * HEAD_DIM] = o[m * tq:(m + 1) * tq]
        hi, lo = _split2(jnp.sum(p, axis=0))
        if blocks_on_rows:
            imp = _dot_nt(ovl_ref[...], hi) + _dot_nt(ovl_ref[...], lo)
        else:
            imp = _dot(hi, ovl_ref[...]) + _dot(lo, ovl_ref[...])
        score = jnp.where(valid, jnp.where(forced, FORCE_SCORE, imp), NEG_INF)
        cnt = jnp.zeros(score.shape, jnp.int32)
        for jp in range(nb):
            cj = score[jp:jp + 1, :] if blocks_on_rows else score[:, jp:jp + 1]
            cnt = cnt + jnp.where(j > jp, jnp.where(cj >= score, 1, 0), jnp.where(cj > score, 1, 0))
        picked = (cnt < N_SEL).astype(F32)
        if blocks_on_rows:
            sel_ref[0, g] = picked
        else:
            sel_ref[0, :, g * nbp:(g + 1) * nbp] = picked


def _cmp_attn(q, kc, vc, pos0, nc, nb, tq, blocks_on_rows):
    B, Tq, _ = q.shape
    ncp = kc.shape[1]
    nbp = -(-nb // SEL_BLOCK) * SEL_BLOCK
    c0 = jnp.arange(ncp)[:, None] * CMP_STRIDE
    jj = jnp.arange(nbp)[None, :]
    ovl = ((c0 < (jj + 1) * SEL_BLOCK) & (c0 + CMP_LEN > jj * SEL_BLOCK) & (jnp.arange(ncp)[:, None] < nc)
           & (jj < nb)).astype(BF16)
    if blocks_on_rows:
        ovl = ovl.T
        sel_spec = pl.BlockSpec((1, KV_HEADS, nbp, tq), lambda b, i: (b, 0, 0, i))
        sel_shape = (B, KV_HEADS, nbp, Tq)
    else:
        sel_spec = pl.BlockSpec((1, tq, KV_HEADS * nbp), lambda b, i: (b, i, 0))
        sel_shape = (B, Tq, KV_HEADS * nbp)
    return pl.pallas_call(
        functools.partial(_cmp_attn_body, pos0, nc, nb, nbp, blocks_on_rows),
        grid=(B, Tq // tq),
        in_specs=[pl.BlockSpec((1, tq, C_MIX), lambda b, i: (b, i, 0)),
                  pl.BlockSpec((1, ncp, LANE), lambda b, i: (b, 0, 0)),
                  pl.BlockSpec((1, ncp, LANE), lambda b, i: (b, 0, 0)),
                  pl.BlockSpec(ovl.shape, lambda b, i: (0, 0))],
        out_specs=[pl.BlockSpec((1, tq, C_MIX), lambda b, i: (b, i, 0)), sel_spec],
        out_shape=[jax.ShapeDtypeStruct((B, Tq, C_MIX), F32), jax.ShapeDtypeStruct(sel_shape, F32)],
        compiler_params=_params("parallel", "parallel"),
        name="cmp_attn",
    )(q, kc, vc, ovl)


def _softmax_step(carry, s, mask, vb, v_feature_major=False):
    m_, l_, acc = carry
    s = jnp.where(mask, s, NEG_INF)
    m_new = jnp.maximum(m_, jnp.max(s, axis=-1, keepdims=True))
    alpha = jnp.exp(m_ - m_new)
    p = jnp.where(mask, jnp.exp(s - m_new), 0.0)
    pv = _dot_nt(p.astype(BF16), vb) if v_feature_major else _dot(p.astype(BF16), vb)
    return m_new, alpha * l_ + jnp.sum(p, axis=-1, keepdims=True), alpha * acc + pv


def _softmax_init(rows):
    return jnp.full((rows, 1), NEG_INF, F32), jnp.zeros((rows, 1), F32), jnp.zeros((rows, HEAD_DIM), F32)


def _stack_heads(x, g):
    return jnp.concatenate([x[:, (g * KV_GROUP + m) * HEAD_DIM:(g * KV_GROUP + m + 1) * HEAD_DIM]
                            for m in range(KV_GROUP)], axis=0)


def _gate_mix(ng, o_c, o_s, o_w, g, tq, o_ref, os_stacked=True):
    for m in range(KV_GROUP):
        h = g * KV_GROUP + m
        sl = slice(h * HEAD_DIM, (h + 1) * HEAD_DIM)
        rs = slice(m * tq, (m + 1) * tq)
        o_ref[0, :, sl] = (ng[:, h:h + 1] * o_c[:, sl]
                           + ng[:, N_HEADS + h:N_HEADS + h + 1] * (o_s[rs] if os_stacked else o_s[:, sl])
                           + ng[:, 2 * N_HEADS + h:2 * N_HEADS + h + 1] * o_w[rs])


SEL_TK = 1024


def _flash_step(carries, qgs, kvs, biases, tq):
    tk = kvs[0].shape[0]
    left = lax.broadcasted_iota(jnp.int32, (1, LANE), 1) < HEAD_DIM
    s = [_dot_nt(qg, kv).reshape(KV_GROUP, tq, tk) + bias[None] for qg, kv, bias in zip(qgs, kvs, biases)]
    m_new = [jnp.maximum(c[0], jnp.max(x, axis=-1, keepdims=True)) for c, x in zip(carries, s)]
    p = [jnp.exp(x - m).astype(BF16).reshape(KV_GROUP * tq, tk) for x, m in zip(s, m_new)]
    pv = [_dot(x, jnp.where(left, jnp.ones((), BF16), kv)).reshape(KV_GROUP, tq, LANE) for x, kv in zip(p, kvs)]
    return tuple((m, jnp.exp(c[0] - m) * c[1] + y) for c, m, y in zip(carries, m_new, pv))


def _flash_init(tq):
    return jnp.full((KV_GROUP, tq, 1), NEG_INF, F32), jnp.zeros((KV_GROUP, tq, LANE), F32)


def _flash_out(carry):
    acc = carry[1]
    return acc[:, :, HEAD_DIM:] / jnp.maximum(acc[:, :, :1], 1e-30)


def _nsa_prompt_body(q_ref, kvs_ref, kvw_ref, sel_ref, oc_ref, ng_ref, o_ref):
    tq = q_ref.shape[1]
    nbp = sel_ref.shape[2]
    q0 = pl.program_id(1) * tq
    q = q_ref[0] * (HEAD_DIM ** -0.5)
    t = q0 + lax.broadcasted_iota(jnp.int32, (tq, 1), 0)
    blk = lax.broadcasted_iota(jnp.int32, (nbp, 1), 0)
    zpad = jnp.zeros((tq, LANE - HEAD_DIM), F32)
    qgs = []
    sels = []
    for g in range(KV_HEADS):
        heads = [jnp.concatenate([q[:, (g * KV_GROUP + m) * HEAD_DIM:(g * KV_GROUP + m + 1) * HEAD_DIM], zpad], axis=1)
                 for m in range(KV_GROUP)]
        qgs.append(jnp.concatenate(heads, axis=0).astype(BF16))
        sels.append(sel_ref[0, g].astype(BF16))

    def sel_step(kt, carry):
        k0 = pl.multiple_of(kt * SEL_TK, SEL_TK)
        kpos = k0 + lax.broadcasted_iota(jnp.int32, (1, SEL_TK), 1)
        expand = (kpos // SEL_BLOCK == blk).astype(BF16)
        causal = kpos <= t
        biases = [jnp.where((_dot_tn(sels[g], expand) > 0.5) & causal, 0.0, NEG_INF) for g in range(KV_HEADS)]
        return _flash_step(carry, qgs, [kvs_ref[g, 0, pl.ds(k0, SEL_TK), :] for g in range(KV_HEADS)], biases, tq)

    res_s = lax.fori_loop(0, (q0 + tq + SEL_TK - 1) // SEL_TK, sel_step, (_flash_init(tq),) * KV_HEADS)

    span = WINDOW + tq
    w0 = pl.multiple_of(jnp.maximum(q0 - WINDOW, 0), tq)
    wpos = w0 + lax.broadcasted_iota(jnp.int32, (1, span), 1)
    wbias = jnp.where((wpos <= t) & (wpos > t - WINDOW), 0.0, NEG_INF)
    res_w = _flash_step((_flash_init(tq),) * KV_HEADS, qgs,
                        [kvw_ref[g, 0, pl.ds(w0, span), :] for g in range(KV_HEADS)], [wbias] * KV_HEADS, tq)

    ng = ng_ref[0]
    oc = oc_ref[0]
    for g in range(KV_HEADS):
        o_s = _flash_out(res_s[g])
        o_w = _flash_out(res_w[g])
        for m in range(KV_GROUP):
            h = g * KV_GROUP + m
            sl = slice(h * HEAD_DIM, (h + 1) * HEAD_DIM)
            o_ref[0, :, sl] = (ng[:, h:h + 1] * oc[:, sl] + ng[:, N_HEADS + h:N_HEADS + h + 1] * o_s[m]
                               + ng[:, 2 * N_HEADS + h:2 * N_HEADS + h + 1] * o_w[m])


def _nsa_prompt(q, kvs_g, kvw_g, sel, o_c, ng, tq):
    B, T, _ = q.shape
    tile = lambda w: pl.BlockSpec((1, tq, w), lambda b, i: (b, i, 0))
    whole = pl.BlockSpec((KV_HEADS, 1, T, LANE), lambda b, i: (0, b, 0, 0))
    return pl.pallas_call(
        _nsa_prompt_body,
        grid=(B, T // tq),
        in_specs=[tile(C_MIX), whole, whole, pl.BlockSpec((1, KV_HEADS, sel.shape[2], tq), lambda b, i: (b, 0, 0, i)),
                  tile(C_MIX), tile(LANE)],
        out_specs=tile(C_MIX),
        out_shape=jax.ShapeDtypeStruct((B, T, C_MIX), F32),
        compiler_params=_params("parallel", "arbitrary"),
        name="nsa_prompt",
    )(q, kvs_g, kvw_g, sel, o_c, ng)


DEC_ROWS = 8


def _nsa_decode_body(mode, pages, pos_q0, pos_k0, pt_ref, q_ref, pool_ref, new_ref, *rest):
    if mode == "sel":
        selst_ref, selnew_ref, o_ref, buf, sem, m_scr, l_scr, acc_scr = rest
    else:
        oc_ref, os_ref, ng_ref, o_ref, buf, sem, m_scr, l_scr, acc_scr = rest
    st = pl.program_id(1)
    tq = DEC_ROWS
    tk = pages * pool_ref.shape[2]
    bps = tk // SEL_BLOCK

    slot = _paged_fetch(pt_ref, pool_ref, buf, sem, pages)

    @pl.when(st == 0)
    def _():
        m_scr[...] = jnp.full(m_scr.shape, NEG_INF, F32)
        l_scr[...] = jnp.zeros(l_scr.shape, F32)
        acc_scr[...] = jnp.zeros(acc_scr.shape, F32)

    q = q_ref[0] * (HEAD_DIM ** -0.5)
    t = pos_q0 + lax.broadcasted_iota(jnp.int32, (tq, 1), 0)

    def update(kv, feature_major, kpos, picked):
        for g in range(KV_HEADS):
            mask = (kpos <= t) & (picked(g) if mode == "sel" else (kpos > t - WINDOW))
            mask = jnp.concatenate([mask] * KV_GROUP, axis=0)
            qg = _stack_heads(q, g).astype(BF16)
            carry = (m_scr[g], l_scr[g], acc_scr[g])
            ks = slice(g * HEAD_DIM, (g + 1) * HEAD_DIM)
            vs = slice(LANE + g * HEAD_DIM, LANE + (g + 1) * HEAD_DIM)
            if feature_major:
                s, v = _dot(qg, kv[ks].astype(BF16)), kv[vs].astype(BF16)
            else:
                s, v = _dot_nt(qg, kv[:, ks].astype(BF16)), kv[:, vs].astype(BF16)
            m_scr[g], l_scr[g], acc_scr[g] = _softmax_step(carry, s, mask, v, feature_major)

    kidx = lax.broadcasted_iota(jnp.int32, (1, tk), 1)
    expand = (kidx // SEL_BLOCK == lax.broadcasted_iota(jnp.int32, (bps, 1), 0)).astype(BF16)
    update(jnp.concatenate([buf[slot, i] for i in range(pages)], axis=1), True, pos_k0 + st * tk + kidx,
           lambda g: _dot(selst_ref[0, 0, :, g * bps:(g + 1) * bps].astype(BF16), expand) > 0.5)

    @pl.when(st == pl.num_programs(1) - 1)
    def _():
        update(new_ref[0], False, pos_q0 + lax.broadcasted_iota(jnp.int32, (1, tq), 1),
               lambda g: selnew_ref[0, :, g:g + 1] > 0.5)
        for g in range(KV_HEADS):
            o = acc_scr[g] / jnp.maximum(l_scr[g], 1e-30)
            if mode == "sel":
                for m in range(KV_GROUP):
                    h = g * KV_GROUP + m
                    o_ref[0, :, h * HEAD_DIM:(h + 1) * HEAD_DIM] = o[m * tq:(m + 1) * tq]
            else:
                _gate_mix(ng_ref[0], oc_ref[0], os_ref[0], o, g, tq, o_ref, os_stacked=False)


def _nsa_decode(mode, q, pool, page_table, kv_new, extras, pages, pos_q0, pos_k0):
    B, n_pages = page_table.shape
    rowblk = lambda a: pl.BlockSpec((1,) + a.shape[1:], lambda b, s, pt: (b,) + (0,) * (a.ndim - 1))
    stepblk = lambda a: pl.BlockSpec((1, 1) + a.shape[2:], lambda b, s, pt: (b, s, 0, 0))
    rows = KV_GROUP * DEC_ROWS
    return pl.pallas_call(
        functools.partial(_nsa_decode_body, mode, pages, pos_q0, pos_k0),
        grid_spec=pltpu.PrefetchScalarGridSpec(
            num_scalar_prefetch=1,
            grid=(B, n_pages // pages),
            in_specs=[rowblk(q), pl.BlockSpec(memory_space=pl.ANY), rowblk(kv_new)]
                     + [stepblk(e) if e.ndim == 4 else rowblk(e) for e in extras],
            out_specs=pl.BlockSpec((1, DEC_ROWS, C_MIX), lambda b, s, pt: (b, 0, 0)),
            scratch_shapes=[pltpu.VMEM((2, pages, KV_W, pool.shape[2]), F32), pltpu.SemaphoreType.DMA((2,)),
                            pltpu.VMEM((KV_HEADS, rows, 1), F32), pltpu.VMEM((KV_HEADS, rows, 1), F32),
                            pltpu.VMEM((KV_HEADS, rows, HEAD_DIM), F32)],
        ),
        out_shape=jax.ShapeDtypeStruct((B, DEC_ROWS, C_MIX), F32),
        compiler_params=_params("arbitrary", "arbitrary"),
        name="nsa_decode_" + mode,
    )(page_table, q, pool, kv_new, *extras)


ROUTE_W = LANE


def _merge_body(x_ref, oa_ref, ob_ref, mg_ref, wa_ref, wb_ref, wo_ref, n2_ref, wrh_ref, wrl_ref, br_ref,
                x1_ref, h_ref, route_ref):
    mg = mg_ref[...]
    merged = (mg[:, :D_MODEL] * _dot(oa_ref[...].astype(BF16), wa_ref[...])
              + mg[:, D_MODEL:] * _dot(ob_ref[...].astype(BF16), wb_ref[...]))
    x1 = x_ref[...] + _dot(merged.astype(BF16), wo_ref[...])
    x1_ref[...] = x1
    h = x1 * lax.rsqrt(jnp.mean(x1 * x1, axis=-1, keepdims=True) + RMS_EPS) * n2_ref[...]
    h_ref[...] = h.astype(BF16)
    hh, hl = _split2(h)
    logits = _dot(hh, wrh_ref[...]) + _dot(hl, wrh_ref[...]) + _dot(hh, wrl_ref[...]) + br_ref[...]
    lane = lax.broadcasted_iota(jnp.int32, (1, ROUTE_W), 1)
    first = lambda hit: jnp.min(jnp.where(hit, lane, ROUTE_W), axis=-1, keepdims=True)
    is_g = lane < N_GROUPS
    gl = jnp.where(is_g, logits, NEG_INF)
    gmax = jnp.max(gl, axis=-1, keepdims=True)
    g_sel = first(gl == gmax)
    g_w = 1.0 / jnp.sum(jnp.where(is_g, jnp.exp(gl - gmax), 0.0), axis=-1, keepdims=True)
    in_grp = (lane >= N_GROUPS) & (lane < N_GROUPS + N_EXPERTS) & (((lane - N_GROUPS) >> 3) == g_sel)
    el = jnp.where(in_grp, logits, NEG_INF)
    v1 = jnp.max(el, axis=-1, keepdims=True)
    i1 = first(el == v1)
    el2 = jnp.where(lane == i1, NEG_INF, el)
    v2 = jnp.max(el2, axis=-1, keepdims=True)
    i2 = first(el2 == v2)
    d = jnp.exp(v2 - v1)
    w1 = g_w / (1.0 + d)
    w2 = g_w * d / (1.0 + d)
    route_ref[...] = jnp.where(lane == 0, (i1 - N_GROUPS).astype(F32),
                               jnp.where(lane == 1, (i2 - N_GROUPS).astype(F32),
                                         jnp.where(lane == 2, w1, jnp.where(lane == 3, w2, 0.0))))


def _merge(x2d, oa, ob, mg, wa, wb, wo, norm2, w_rg, b_rg, w_re, b_re, tm):
    n = x2d.shape[0]
    const = lambda shape: pl.BlockSpec(shape, lambda i: (0,) * len(shape))
    row = lambda w: pl.BlockSpec((tm, w), lambda i: (i, 0))
    wr = jnp.zeros((D_MODEL, ROUTE_W), F32).at[:, :N_GROUPS].set(w_rg).at[:, N_GROUPS:N_GROUPS + N_EXPERTS].set(w_re)
    br = jnp.zeros((1, ROUTE_W), F32).at[0, :N_GROUPS].set(b_rg).at[0, N_GROUPS:N_GROUPS + N_EXPERTS].set(b_re)
    wrh, wrl = _split2(wr)
    return pl.pallas_call(
        _merge_body,
        grid=(n // tm,),
        in_specs=[row(D_MODEL), row(C_MIX), row(C_MIX), row(2 * D_MODEL), const((C_MIX, D_MODEL)),
                  const((C_MIX, D_MODEL)), const((D_MODEL, D_MODEL)), const((1, D_MODEL)),
                  const((D_MODEL, ROUTE_W)), const((D_MODEL, ROUTE_W)), const((1, ROUTE_W))],
        out_specs=[row(D_MODEL), row(D_MODEL), row(ROUTE_W)],
        out_shape=[jax.ShapeDtypeStruct((n, D_MODEL), F32), jax.ShapeDtypeStruct((n, D_MODEL), BF16),
                   jax.ShapeDtypeStruct((n, ROUTE_W), F32)],
        compiler_params=_params("parallel"),
        name="merge_route",
    )(x2d, oa, ob, mg, wa.astype(BF16), wb.astype(BF16), wo.astype(BF16), norm2.reshape(1, D_MODEL), wrh, wrl, br)


MOE_TB = 256


def _moe_body(be_ref, nu_ref, x_ref, wg_ref, wu_ref, wd_ref, y_ref, wg_b, wu_b, wd_b):
    i = pl.program_id(0)

    @pl.when((i == 0) | (be_ref[i] != be_ref[jnp.maximum(i - 1, 0)]))
    def _():
        wg_b[...] = wg_ref[0].astype(BF16)
        wu_b[...] = wu_ref[0].astype(BF16)
        wd_b[...] = wd_ref[0].astype(BF16)

    @pl.when(i < nu_ref[0])
    def _():
        x = x_ref[...]
        gate = _dot(x, wg_b[...])
        hid = gate * _sigmoid(gate) * _dot(x, wu_b[...])
        y_ref[...] = _dot(hid.astype(BF16), wd_b[...])

    @pl.when(i >= nu_ref[0])
    def _():
        y_ref[...] = jnp.zeros(y_ref.shape, F32)


def _moe_experts(xbuf, blk_e, n_used, wg, wu, wd):
    n_blk = blk_e.shape[0]
    wspec = lambda shape: pl.BlockSpec((1,) + shape, lambda i, be, nu: (be[i], 0, 0))
    return pl.pallas_call(
        _moe_body,
        grid_spec=pltpu.PrefetchScalarGridSpec(
            num_scalar_prefetch=2,
            grid=(n_blk,),
            in_specs=[pl.BlockSpec((MOE_TB, D_MODEL), lambda i, be, nu: (i, 0)), wspec((D_MODEL, D_EXPERT)),
                      wspec((D_MODEL, D_EXPERT)), wspec((D_EXPERT, D_MODEL))],
            out_specs=pl.BlockSpec((MOE_TB, D_MODEL), lambda i, be, nu: (i, 0)),
            scratch_shapes=[pltpu.VMEM((D_MODEL, D_EXPERT), BF16), pltpu.VMEM((D_MODEL, D_EXPERT), BF16),
                            pltpu.VMEM((D_EXPERT, D_MODEL), BF16)],
        ),
        out_shape=jax.ShapeDtypeStruct((n_blk * MOE_TB, D_MODEL), F32),
        compiler_params=_params("arbitrary"),
        name="moe_experts",
    )(blk_e, n_used, xbuf, wg, wu, wd)


def _moe_dispatch(h, route):
    n = h.shape[0]
    expert = route[:, :2].astype(jnp.int32).reshape(-1)
    wts = route[:, 2:4]
    n_slots = 2 * n
    n_blk = -(-n_slots // MOE_TB) + N_EXPERTS
    onehot = expert[:, None] == jnp.arange(N_EXPERTS, dtype=jnp.int32)[None, :]
    counts = jnp.sum(onehot, axis=0, dtype=jnp.int32)
    c_start = jnp.cumsum(counts) - counts
    padded = (counts + MOE_TB - 1) // MOE_TB * MOE_TB
    p_end = jnp.cumsum(padded)
    p_start = p_end - padded
    order = jnp.argsort(expert, stable=True).astype(jnp.int32)
    rank = jnp.argsort(order).astype(jnp.int32)
    dest = (rank + jnp.sum(jnp.where(onehot, (p_start - c_start)[None, :], 0), axis=1)).reshape(n, 2)
    blk_e = jnp.minimum(jnp.sum(p_end[None, :] <= (jnp.arange(n_blk, dtype=jnp.int32) * MOE_TB)[:, None], axis=1),
                        N_EXPERTS - 1).astype(jnp.int32)
    n_used = (p_end[-1:] // MOE_TB).astype(jnp.int32)
    k_in_e = jnp.arange(n_blk * MOE_TB, dtype=jnp.int32) - jnp.repeat(p_start[blk_e], MOE_TB)
    src = order[jnp.clip(jnp.repeat(c_start[blk_e], MOE_TB) + k_in_e, 0, n_slots - 1)]
    row_tok = jnp.where(k_in_e < jnp.repeat(counts[blk_e], MOE_TB), src // 2, 0)
    return h[row_tok], blk_e, n_used, dest, wts


def _moe_apply(x1, dispatch, wg, wu, wd):
    xbuf, blk_e, n_used, dest, wts = dispatch
    ybuf = _moe_experts(xbuf, blk_e, n_used, wg, wu, wd)
    return x1 + wts[:, 0:1] * ybuf[dest[:, 0]] + wts[:, 1:2] * ybuf[dest[:, 1]]


def kernel(x_prompt, x_sample, cache_cmp_kv, cache_slc_kv, cache_win_kv, state_wkv, state_shift, page_table, norm1, w_in, mu_shift, w0, w_decay_up, a0, w_iclr_up, w_gate_up, k_k, k_a, r_k, ln_x_w, ln_x_b, q_norm, kc_norm, ks_norm, kw_norm, cmp_pe_k, cmp_w1_k, cmp_b1_k, cmp_w2_k, cmp_pe_v, cmp_w1_v, cmp_b1_v, cmp_w2_v, w_branch_a, w_branch_b, w_out, norm2, w_route_group, b_route_group, w_route_expert, b_route_expert, w_exp_gate, w_exp_up, w_exp_down):
    assert norm1.shape[0] == 1, "single-layer trunk"
    Bp, Tp, _ = x_prompt.shape
    Bs, Ts, _ = x_sample.shape
    n_pool = cache_cmp_kv.shape[1]
    past = page_table.shape[1] * PAGE_SIZE
    n_buf = cache_win_kv.shape[2]
    kv5 = lambda a, b, t: a.reshape(1, b, t, 2, KV_HEADS, HEAD_DIM)
    kv5t = lambda a: a.reshape(a.shape[0], 2, KV_HEADS, HEAD_DIM, a.shape[2]).transpose(0, 4, 1, 2, 3)[None]

    w_pad = _pad_w_in(w_in[0])
    rw_p = (mu_shift[0], w0[0], w_decay_up[0], a0[0], w_iclr_up[0], w_gate_up[0], k_k[0], k_a[0],
            r_k[0].reshape(-1), ln_x_w[0], ln_x_b[0])
    wc, pea, peb, b1, w2b = _cmp_weights(cmp_pe_k[0], cmp_w1_k[0], cmp_b1_k[0], cmp_w2_k[0],
                                         cmp_pe_v[0], cmp_w1_v[0], cmp_b1_v[0], cmp_w2_v[0])
    merge_p = (w_branch_a[0], w_branch_b[0], w_out[0], norm2[0], w_route_group[0], b_route_group[0],
               w_route_expert[0], b_route_expert[0])
    moe_w = (w_exp_gate[0], w_exp_up[0], w_exp_down[0])

    xp = x_prompt.reshape(Bp * Tp, D_MODEL)
    zrw, q, kvc, _, _, ng, mg, kvs_g, kvw_g, kvc_t, kvs_t, kvw_t = _in_proj(xp, norm1[0], w_pad, q_norm[0], ks_norm[0],
                                                                            kw_norm[0], 512, Tp)
    zrw3 = zrw.reshape(Bp, Tp, RW_PAD)
    oa, wkv_p = _rwkv(zrw3, jnp.zeros((Bp, 1, RW_PAD), F32), jnp.zeros((Bp, N_HEADS, HEAD_DIM, HEAD_DIM), F32),
                      Tp, *rw_p)
    ab = _cmp_proj(kvc.reshape(Bp * Tp // CMP_STRIDE, CH_W), pea, peb, wc, min(256, Bp * Tp // CMP_STRIDE))
    kc, vc = _cmp_finish(ab.reshape(Bp, Tp // CMP_STRIDE, -1), b1, w2b, kc_norm[0])
    q3 = q.reshape(Bp, Tp, C_MIX)
    o_c, sel = _cmp_attn(q3, kc, vc, 0, (Tp - CMP_LEN) // CMP_STRIDE + 1, Tp // SEL_BLOCK, 128, True)
    ob = _nsa_prompt(q3, kvs_g.reshape(KV_HEADS, Bp, Tp, LANE), kvw_g.reshape(KV_HEADS, Bp, Tp, LANE), sel, o_c,
                     ng.reshape(Bp, Tp, LANE), 128)
    x1, h, route = _merge(xp, oa.reshape(Bp * Tp, C_MIX), ob.reshape(Bp * Tp, C_MIX), mg, *merge_p, 512)
    disp_p = _moe_dispatch(h, route)
    keep_p = min(WINDOW, Tp)

    xs = x_sample.reshape(Bs * Ts, D_MODEL)
    zrw_s, q_s, kvc_s, kvs_s, kvw_s, ng_s, mg_s = _in_proj(xs, norm1[0], w_pad, q_norm[0], ks_norm[0], kw_norm[0],
                                                           Bs * Ts, Bs * Ts)[:7]
    zrw_s3 = zrw_s.reshape(Bs, Ts, RW_PAD)
    oa_s, wkv_s = _rwkv(jnp.pad(zrw_s3, ((0, 0), (0, RW_CHUNK - Ts), (0, 0))),
                        jnp.pad(state_shift[0], ((0, 0), (0, RW_PAD - RW_IN)))[:, None], state_wkv[0], Ts, *rw_p)
    nc_s = (past + Ts - CMP_LEN) // CMP_STRIDE + 1
    assert (nc_s + CMP_LEN // CMP_STRIDE - 1) * CMP_STRIDE <= past, "compression blocks only cover cached rows"
    feature_major = lambda c: c.transpose(0, 2, 3, 4, 1).reshape(c.shape[0], KV_W, c.shape[1])
    ab_s = _cmp_proj_paged(feature_major(cache_cmp_kv[0]), page_table, pea, peb, wc)
    kc_s, vc_s = _cmp_finish(ab_s, b1, w2b, kc_norm[0])
    rows8 = lambda a: jnp.pad(a.reshape(Bs, Ts, -1), ((0, 0), (0, DEC_ROWS - Ts), (0, 0)))
    q8 = rows8(q_s)
    oc_s, sel_s = _cmp_attn(q8, kc_s, vc_s, past, nc_s, -(-(past + Ts) // SEL_BLOCK), DEC_ROWS, False)
    assert past % SEL_BLOCK == 0 and Ts <= SEL_BLOCK, "the new rows share one selection block"
    n_st = page_table.shape[1] // CMP_PAGES
    sel4 = sel_s.reshape(Bs, DEC_ROWS, KV_HEADS, -1)
    sel_steps = sel4[..., :past // SEL_BLOCK].reshape(Bs, DEC_ROWS, KV_HEADS, n_st, -1)
    sel_steps = sel_steps.transpose(0, 3, 1, 2, 4).reshape(Bs, n_st, DEC_ROWS, -1)
    sel_new = jnp.pad(sel4[..., past // SEL_BLOCK], ((0, 0), (0, 0), (0, LANE - KV_HEADS)))
    os_s = _nsa_decode("sel", q8, feature_major(cache_slc_kv[0]), page_table, rows8(kvs_s),
                       (sel_steps, sel_new), CMP_PAGES, past, 0)
    ob_s = _nsa_decode("win", q8, feature_major(cache_win_kv[0]), jnp.arange(Bs, dtype=jnp.int32).reshape(Bs, 1),
                       rows8(kvw_s), (oc_s, os_s, rows8(ng_s)), 1, past, past - n_buf)
    x1_s, h_s, route_s = _merge(xs, oa_s[:, :Ts].reshape(Bs * Ts, C_MIX), ob_s[:, :Ts].reshape(Bs * Ts, C_MIX), mg_s,
                                *merge_p, Bs * Ts)
    disp_p, x1_s = lax.optimization_barrier((disp_p, x1_s))
    y_p = _moe_apply(x1, disp_p, *moe_w).reshape(Bp, Tp, D_MODEL)
    y_s = _moe_apply(x1_s, _moe_dispatch(h_s, route_s), *moe_w).reshape(Bs, Ts, D_MODEL)
    keep_s = min(WINDOW, n_buf + Ts)
    win_s = jnp.concatenate([cache_win_kv[0].reshape(Bs, n_buf, KV_W), kvw_s.reshape(Bs, Ts, KV_W)], axis=1)

    return (y_p, y_s,
            kv5t(kvc_t), kv5t(kvs_t), kv5t(kvw_t[:, :, Tp - keep_p:]),
            wkv_p[None], zrw3[:, -1, :RW_IN][None],
            kv5(kvc_s, Bs, Ts), kv5(kvs_s, Bs, Ts), kv5(win_s[:, n_buf + Ts - keep_s:], Bs, keep_s),
            wkv_s[None], zrw_s3[:, -1, :RW_IN][None])
```

```python
import functools
import math

import jax
import jax.numpy as jnp
from jax import lax
from jax.experimental import pallas as pl
from jax.experimental.pallas import tpu as pltpu

F32 = jnp.float32
BF16 = jnp.bfloat16

D_MODEL = 1024
HEAD_DIM = 64
N_HEADS = 8
C_MIX = N_HEADS * HEAD_DIM
R_DECAY, R_ICLR, R_GATE = 32, 32, 96
RW_IN = 3 * C_MIX + R_DECAY + R_ICLR + R_GATE
KV_HEADS = 2
KV_GROUP = N_HEADS // KV_HEADS
KV_W = 2 * KV_HEADS * HEAD_DIM
CMP_LEN, CMP_STRIDE = 32, 16
SEL_BLOCK = 64
N_SEL = 16
WINDOW = 512
PAGE_SIZE = 128
N_GROUPS, EXPERTS_PER_GROUP = 4, 8
N_EXPERTS = N_GROUPS * EXPERTS_PER_GROUP
D_EXPERT = D_MODEL // 2
RMS_EPS = 1e-6
GN_EPS = 64e-5
NEG_INF = -1e30
FORCE_SCORE = 1e6

LANE = 128
VMEM_LIMIT = 56 * 1024 * 1024

RW_PAD = 1792
OFF_Q = RW_PAD
OFF_KVC = OFF_Q + C_MIX
OFF_KVS = OFF_KVC + KV_W
OFF_KVW = OFF_KVS + KV_W
OFF_NG = OFF_KVW + KV_W
OFF_MG = OFF_NG + LANE
N_IN_PAD = OFF_MG + 2 * D_MODEL
RW_TAIL = 3 * C_MIX


def _params(*sem):
    return pltpu.CompilerParams(dimension_semantics=sem, vmem_limit_bytes=VMEM_LIMIT)


def _dot(a, b):
    return jnp.dot(a, b, preferred_element_type=F32)


def _dot_nt(a, b):
    return lax.dot_general(a, b, (((1,), (1,)), ((), ())), preferred_element_type=F32)


def _dot_tn(a, b):
    return lax.dot_general(a, b, (((0,), (0,)), ((), ())), preferred_element_type=F32)


def _split2(x):
    hi = x.astype(BF16)
    lo = (x - hi.astype(F32)).astype(BF16)
    return hi, lo


def _split3(x):
    hi = x.astype(BF16)
    r1 = x - hi.astype(F32)
    mid = r1.astype(BF16)
    lo = (r1 - mid.astype(F32)).astype(BF16)
    return hi, mid, lo


def _gsum(y, g):
    hi, lo = _split2(y)
    return _dot(hi, g) + _dot(lo, g)


def _gsum_cols(y, g):
    n = y.shape[1] // LANE
    rows = y.shape[0]
    stacked = jnp.concatenate([part[:, j * LANE:(j + 1) * LANE] for part in _split2(y) for j in range(n)], axis=0)
    r = _dot(stacked, g)
    r = r[:n * rows] + r[n * rows:]
    return jnp.concatenate([r[j * rows:(j + 1) * rows] for j in range(n)], axis=1)


def _block_ones(n, blk):
    i = jnp.arange(n) // blk
    return (i[:, None] == i[None, :]).astype(BF16)


def _sigmoid(x):
    return 1.0 / (1.0 + jnp.exp(-x))


def _inproj_body(x_ref, n1_ref, w_ref, g512_ref, g128_ref, qn_ref, ksn_ref, kwn_ref,
                 zrw_ref, q_ref, kvc_ref, kvs_ref, kvw_ref, ng_ref, mg_ref, kvsg_ref, kvwg_ref,
                 kvct_ref, kvst_ref, kvwt_ref):
    x = x_ref[...]
    ms = jnp.mean(x * x, axis=-1, keepdims=True)
    xn = (x * lax.rsqrt(ms + RMS_EPS) * n1_ref[...]).astype(BF16)

    def proj(a, b):
        return _dot(xn, w_ref[:, a:b])

    zrw_ref[...] = proj(0, RW_PAD)
    q = proj(OFF_Q, OFF_KVC)
    q_ref[...] = q * lax.rsqrt(_gsum(q * q, g512_ref[...]) * (1.0 / HEAD_DIM) + RMS_EPS) * qn_ref[...]
    kvc = proj(OFF_KVC, OFF_KVS)
    kvc_ref[...] = kvc
    kvct_ref[0] = kvc.T
    for off, nref, oref, gref, tref in ((OFF_KVS, ksn_ref, kvs_ref, kvsg_ref, kvst_ref),
                                        (OFF_KVW, kwn_ref, kvw_ref, kvwg_ref, kvwt_ref)):
        kv = proj(off, off + KV_W)
        k = kv[:, :LANE]
        kn = k * lax.rsqrt(_gsum(k * k, g128_ref[...]) * (1.0 / HEAD_DIM) + RMS_EPS) * nref[...]
        v = kv[:, LANE:]
        oref[:, :LANE] = kn
        oref[:, LANE:] = v
        tref[0, :LANE] = kn.T
        tref[0, LANE:] = v.T
        for g in range(KV_HEADS):
            hs = slice(g * HEAD_DIM, (g + 1) * HEAD_DIM)
            gref[g] = jnp.concatenate([kn[:, hs], v[:, hs]], axis=1).astype(BF16)
    ng_ref[...] = _sigmoid(proj(OFF_NG, OFF_MG))
    mg_ref[...] = _sigmoid(proj(OFF_MG, N_IN_PAD))


def _in_proj(x2d, norm1, w_pad, q_norm, ks_norm, kw_norm, tm, seq_len):
    n = x2d.shape[0]
    seq_tiles = seq_len // tm
    const = lambda shape: pl.BlockSpec(shape, lambda i: (0,) * len(shape))
    row = lambda w: pl.BlockSpec((tm, w), lambda i: (i, 0))
    widths = (RW_PAD, C_MIX, KV_W, KV_W, KV_W, LANE, 2 * D_MODEL)
    return pl.pallas_call(
        _inproj_body,
        grid=(n // tm,),
        in_specs=[row(D_MODEL), const((1, D_MODEL)), const((D_MODEL, N_IN_PAD)), const((C_MIX, C_MIX)),
                  const((LANE, LANE)), const((1, C_MIX)), const((1, LANE)), const((1, LANE))],
        out_specs=[row(w) for w in widths] + [pl.BlockSpec((KV_HEADS, tm, LANE), lambda i: (0, i, 0))] * 2
                  + [pl.BlockSpec((1, KV_W, tm), lambda i: (i // seq_tiles, 0, i % seq_tiles))] * 3,
        out_shape=[jax.ShapeDtypeStruct((n, w), F32) for w in widths]
                  + [jax.ShapeDtypeStruct((KV_HEADS, n, LANE), BF16)] * 2
                  + [jax.ShapeDtypeStruct((n // seq_len, KV_W, seq_len), F32)] * 3,
        compiler_params=_params("parallel"),
        name="in_proj",
    )(x2d, norm1.reshape(1, D_MODEL), w_pad, _block_ones(C_MIX, HEAD_DIM), _block_ones(LANE, HEAD_DIM),
      jnp.tile(q_norm, N_HEADS).reshape(1, C_MIX), jnp.tile(ks_norm, KV_HEADS).reshape(1, LANE),
      jnp.tile(kw_norm, KV_HEADS).reshape(1, LANE))


def _pad_w_in(w_in):
    d = w_in.shape[0]
    z = lambda n: jnp.zeros((d, n), w_in.dtype)
    o_q = RW_IN
    o_ng = o_q + C_MIX + 3 * KV_W
    o_mg = o_ng + 3 * N_HEADS
    return jnp.concatenate([w_in[:, :RW_IN], z(RW_PAD - RW_IN), w_in[:, o_q:o_ng], w_in[:, o_ng:o_mg],
                            z(LANE - 3 * N_HEADS), w_in[:, o_mg:]], axis=1).astype(BF16)


RW_CHUNK = 64
RW_HSTACK = 4
RW_ROWS = 4


def _rwkv_body(t_valid, z_ref, sp_ref, s0_ref, mu_ref, w0_ref, a0_ref, kk_ref, ka_ref, rk_ref, lnw_ref, lnb_ref,
               wd_ref, wi_ref, wg_ref, g128_ref, o_ref, s_ref, prev_scr):
    @pl.when(pl.program_id(1) == 0)
    def _():
        s_ref[...] = s0_ref[...]
        prev_scr[...] = sp_ref[...]

    prep = [_rwkv_prep(t_valid, bi, z_ref, mu_ref, w0_ref, a0_ref, kk_ref, ka_ref, rk_ref, wd_ref, wi_ref, wg_ref,
                       g128_ref, prev_scr) for bi in range(z_ref.shape[0])]
    ys = _rwkv_chains(prep, s_ref)
    g128 = g128_ref[...]
    for bi, (p, y) in enumerate(zip(prep, ys)):
        yc = y - _gsum_cols(y, g128) * (1.0 / HEAD_DIM)
        var = _gsum_cols(yc * yc, g128) * (1.0 / HEAD_DIM)
        yn = yc * lax.rsqrt(var + GN_EPS) * lnw_ref[...] + lnb_ref[...]
        o_ref[bi] = (yn + _gsum_cols(p["rkk"], g128) * p["v"]) * p["g"]


def _rwkv_prep(t_valid, bi, z_ref, mu_ref, w0_ref, a0_ref, kk_ref, ka_ref, rk_ref, wd_ref, wi_ref, wg_ref,
               g128_ref, prev_scr):
    C = RW_CHUNK
    c = pl.program_id(1)
    z = z_ref[bi]
    row = lax.broadcasted_iota(jnp.int32, (C, 1), 0)
    z_prev = jnp.where(row == 0, prev_scr[bi], pltpu.roll(z, 1, axis=0))
    prev_scr[bi] = z[C - 1:C]
    zm = z + (z_prev - z) * mu_ref[...]
    r = zm[:, 0:C_MIX]
    k = zm[:, C_MIX:2 * C_MIX]
    v = zm[:, 2 * C_MIX:3 * C_MIX]
    tail = zm[:, RW_TAIL:RW_PAD]
    w_lora = _dot(jnp.tanh(tail).astype(BF16), wd_ref[...])
    a_lora = _dot(tail.astype(BF16), wi_ref[...])
    g = _dot(_sigmoid(tail).astype(BF16), wg_ref[...])
    u = -(w0_ref[...] + w_lora)
    softplus = jnp.maximum(u, 0.0) + jnp.log(1.0 + jnp.exp(-jnp.abs(u)))
    w_log = -softplus - 0.5
    valid = (c * C + row) < t_valid
    ld = jnp.where(valid, -jnp.exp(w_log), 0.0)
    a = _sigmoid(a0_ref[...] + a_lora)
    kk = k * kk_ref[...]
    kk = kk / jnp.maximum(jnp.sqrt(_gsum_cols(kk * kk, g128_ref[...])), 1e-12)
    k2 = k * (1.0 + (a - 1.0) * ka_ref[...])

    ci = lax.broadcasted_iota(jnp.int32, (C, C), 0)
    cj = lax.broadcasted_iota(jnp.int32, (C, C), 1)
    tri = (ci >= cj).astype(BF16)
    h1, h2, h3 = _split3(ld)
    cl = _dot(tri, h1) + _dot(tri, h2) + _dot(tri, h3)
    p_in = jnp.exp(cl)
    p_inv = jnp.exp(-cl)
    r_t = r * p_in
    a_t = -kk * jnp.exp(cl - ld)
    b_t = jnp.where(valid, kk * a * p_inv, 0.0)
    k_t = jnp.where(valid, k2 * p_inv, 0.0)
    p_end = p_in[C - 1:C]

    return dict(a_t=a_t, r_t=r_t, b_t=b_t, k_t=k_t, v=v, p_end=p_end, rkk=r * k2 * rk_ref[...], g=g)


def _rwkv_chains(prep, s_ref):
    C = RW_CHUNK
    HS = RW_HSTACK
    R = HS * C
    ri = lax.broadcasted_iota(jnp.int32, (R, R), 0)
    rj = lax.broadcasted_iota(jnp.int32, (R, R), 1)
    same = (ri // C) == (rj // C)
    lower = same & (ri > rj)
    lower_eq = same & (ri >= rj)
    eye = (ri == rj).astype(F32)
    chains = [(bi, hg) for bi in range(len(prep)) for hg in range(N_HEADS // HS)]
    heads = lambda hg: [hg * HS + m for m in range(HS)]
    stack = lambda x, hg: jnp.concatenate([x[:, h * HEAD_DIM:(h + 1) * HEAD_DIM] for h in heads(hg)],
                                          axis=0).astype(BF16)
    A, Rr, Bm, Km, V = ([stack(prep[bi][name], hg) for bi, hg in chains]
                        for name in ("a_t", "r_t", "b_t", "k_t", "v"))
    S4 = [_dot_nt(jnp.concatenate([a, r], axis=0), jnp.concatenate([b, k], axis=0))
          for a, r, b, k in zip(A, Rr, Bm, Km)]
    L = [jnp.where(lower, s[:R, :R], 0.0) for s in S4]
    Lak = [jnp.where(lower, s[:R, R:], 0.0).astype(BF16) for s in S4]
    Mr = [jnp.concatenate([jnp.where(lower_eq, s[R:, :R], 0.0), jnp.where(lower_eq, s[R:, R:], 0.0)],
                          axis=1).astype(BF16) for s in S4]
    s0 = [[s_ref[bi, h] for h in heads(hg)] for bi, hg in chains]
    on_state = [[_dot_nt(jnp.concatenate([a[m * C:(m + 1) * C], r[m * C:(m + 1) * C]], axis=0),
                         s0c[m].astype(BF16)) for m in range(HS)]
                for a, r, s0c in zip(A, Rr, s0)]
    rhs = [jnp.concatenate([o[:C] for o in os], axis=0) + _dot(lak, v) for os, lak, v in zip(on_state, Lak, V)]
    X = [eye + l for l in L]
    Lp = L
    for _ in range(int(math.log2(C)) - 1):
        Lpb = [lp.astype(BF16) for lp in Lp]
        Lp = [_dot(lp, lp) for lp in Lpb]
        X = [x + _dot(x.astype(BF16), lp.astype(BF16)) for x, lp in zip(X, Lp)]
    U = [_dot(x.astype(BF16), r.astype(BF16)).astype(BF16) for x, r in zip(X, rhs)]
    Y = [jnp.concatenate([o[C:] for o in os], axis=0) + _dot(mr, jnp.concatenate([u, v], axis=0))
         for os, mr, u, v in zip(on_state, Mr, U, V)]
    for ci, (bi, hg) in enumerate(chains):
        for m, h in enumerate(heads(hg)):
            rs = slice(m * C, (m + 1) * C)
            upd = _dot_tn(jnp.concatenate([U[ci][rs], V[ci][rs]], axis=0),
                          jnp.concatenate([Bm[ci][rs], Km[ci][rs]], axis=0))
            s_ref[bi, h] = (s0[ci][m] + upd) * prep[bi]["p_end"][:, h * HEAD_DIM:(h + 1) * HEAD_DIM]
    return [jnp.concatenate([Y[ci][m * C:(m + 1) * C] for ci, (cb, _) in enumerate(chains) if cb == bi
                             for m in range(HS)], axis=1) for bi in range(len(prep))]


def _rwkv(z_rw, shift_prev, s0, t_valid, mu, w0, wd, a0, wi, wg, k_k, k_a, r_k, ln_w, ln_b):
    B, T, _ = z_rw.shape
    C = RW_CHUNK
    const = lambda shape: pl.BlockSpec(shape, lambda b, c: (0,) * len(shape))
    vec = lambda p: p.reshape(1, C_MIX)
    pad_rows = lambda w, off: jnp.zeros((RW_PAD - RW_TAIL, C_MIX), F32).at[off:off + w.shape[0]].set(w).astype(BF16)
    nb = math.gcd(B, RW_ROWS)
    state_spec = pl.BlockSpec((nb, N_HEADS, HEAD_DIM, HEAD_DIM), lambda b, c: (b, 0, 0, 0))
    return pl.pallas_call(
        functools.partial(_rwkv_body, t_valid),
        grid=(B // nb, T // C),
        in_specs=[pl.BlockSpec((nb, C, RW_PAD), lambda b, c: (b, c, 0)),
                  pl.BlockSpec((nb, 1, RW_PAD), lambda b, c: (b, 0, 0)),
                  state_spec, const((1, RW_PAD))] + [const((1, C_MIX))] * 7
                 + [const((RW_PAD - RW_TAIL, C_MIX))] * 3 + [const((LANE, LANE))],
        out_specs=[pl.BlockSpec((nb, C, C_MIX), lambda b, c: (b, c, 0)), state_spec],
        out_shape=[jax.ShapeDtypeStruct((B, T, C_MIX), F32),
                   jax.ShapeDtypeStruct((B, N_HEADS, HEAD_DIM, HEAD_DIM), F32)],
        scratch_shapes=[pltpu.VMEM((nb, 1, RW_PAD), F32)],
        compiler_params=_params("parallel", "arbitrary"),
        name="rwkv7",
    )(z_rw, shift_prev, s0, jnp.pad(mu, (0, RW_PAD - RW_IN)).reshape(1, RW_PAD), vec(w0), vec(a0), vec(k_k),
      vec(k_a), vec(r_k), vec(ln_w), vec(ln_b), pad_rows(wd, 0), pad_rows(wi, R_DECAY),
      pad_rows(wg, R_DECAY + R_ICLR), _block_ones(LANE, HEAD_DIM))


CH_W = CMP_STRIDE * KV_W
N_SLOT = 2 * KV_HEADS


def _cmp_weights(pe_k, w1_k, b1_k, w2_k, pe_v, w1_v, b1_v, w2_v):
    eye = jnp.eye(N_SLOT, dtype=F32)
    w1 = jnp.stack([w1_k, w1_k, w1_v, w1_v])
    pe = jnp.stack([pe_k, pe_k, pe_v, pe_v])
    halves = []
    pes = []
    for r in range(CMP_LEN // CMP_STRIDE):
        ls = slice(r * CMP_STRIDE, (r + 1) * CMP_STRIDE)
        halves.append(jnp.einsum('sldf,st->lsdtf', w1[:, ls], eye).reshape(CH_W, N_SLOT * HEAD_DIM))
        pes.append(jnp.transpose(pe[:, ls], (1, 0, 2)).reshape(1, CH_W))
    wc = jnp.concatenate(halves, axis=1).astype(BF16)
    w2 = jnp.stack([w2_k, w2_k, w2_v, w2_v])
    w2b = jnp.einsum('sfd,st->sftd', w2, eye).reshape(N_SLOT * HEAD_DIM, N_SLOT * HEAD_DIM).astype(BF16)
    b1 = jnp.concatenate([b1_k, b1_k, b1_v, b1_v]).reshape(1, N_SLOT * HEAD_DIM)
    return wc, pes[0], pes[1], b1, w2b


def _cmp_proj_body(ch_ref, pea_ref, peb_ref, wc_ref, ab_ref):
    ch = ch_ref[...]
    w = N_SLOT * HEAD_DIM
    ab_ref[:, :w] = _dot((ch + pea_ref[...]).astype(BF16), wc_ref[:, :w])
    ab_ref[:, w:] = _dot((ch + peb_ref[...]).astype(BF16), wc_ref[:, w:])


def _cmp_proj(chunks, pea, peb, wc, tr):
    n = chunks.shape[0]
    const = lambda shape: pl.BlockSpec(shape, lambda i: (0,) * len(shape))
    return pl.pallas_call(
        _cmp_proj_body,
        grid=(n // tr,),
        in_specs=[pl.BlockSpec((tr, CH_W), lambda i: (i, 0)), const((1, CH_W)), const((1, CH_W)),
                  const((CH_W, 2 * N_SLOT * HEAD_DIM))],
        out_specs=pl.BlockSpec((tr, 2 * N_SLOT * HEAD_DIM), lambda i: (i, 0)),
        out_shape=jax.ShapeDtypeStruct((n, 2 * N_SLOT * HEAD_DIM), F32),
        compiler_params=_params("parallel"),
        name="cmp_proj",
    )(chunks, pea, peb, wc)


CMP_PAGES = 16


def _paged_fetch(pt_ref, pool_ref, buf, sem, pages):
    b = pl.program_id(0)
    st = pl.program_id(1)
    n_st = pl.num_programs(1)
    step = b * n_st + st
    slot = step % 2
    ptok = pool_ref.shape[2]

    def copies(bb, stt, sl):
        dst = (lambda i: buf.at[sl, i]) if len(buf.shape) == 4 else (lambda i: buf.at[sl, :, pl.ds(i * ptok, ptok)])
        return [pltpu.make_async_copy(pool_ref.at[pt_ref[bb, stt * pages + i]], dst(i), sem.at[sl])
                for i in range(pages)]

    @pl.when(step == 0)
    def _():
        for c in copies(0, 0, 0):
            c.start()

    @pl.when(step + 1 < pl.num_programs(0) * n_st)
    def _():
        wrap = st + 1 == n_st
        for c in copies(jnp.where(wrap, b + 1, b), jnp.where(wrap, 0, st + 1), 1 - slot):
            c.start()

    for c in copies(b, st, slot):
        c.wait()
    return slot


def _cmp_proj_paged_body(pt_ref, cache_ref, pe_ref, wl_ref, ab_ref, buf, sem, rows_scr):
    slot = _paged_fetch(pt_ref, cache_ref, buf, sem, CMP_PAGES)
    n_chunks = CMP_PAGES * PAGE_SIZE // CMP_STRIDE
    w = N_SLOT * HEAD_DIM
    for i in range(CMP_PAGES):
        for j in range(rows_scr.shape[0]):
            rows_scr[j, i * PAGE_SIZE:(i + 1) * PAGE_SIZE, :] = buf[slot, i, j * LANE:(j + 1) * LANE, :].T
    acc = [jnp.zeros((n_chunks, w), F32) for _ in range(CMP_LEN // CMP_STRIDE)]
    for l in range(CMP_STRIDE):
        x = jnp.concatenate([rows_scr[j, pl.ds(l, n_chunks, stride=CMP_STRIDE), :] for j in range(rows_scr.shape[0])],
                            axis=1)
        for r in range(CMP_LEN // CMP_STRIDE):
            acc[r] = acc[r] + _dot((x + pe_ref[r, l]).astype(BF16), wl_ref[l, :, r * w:(r + 1) * w])
    for r in range(CMP_LEN // CMP_STRIDE):
        ab_ref[0, :, r * w:(r + 1) * w] = acc[r]


def _cmp_proj_paged(cache, page_table, pea, peb, wc):
    B, n_pages = page_table.shape
    rows = PAGE_SIZE // CMP_STRIDE
    w = N_SLOT * HEAD_DIM
    const = lambda shape: pl.BlockSpec(shape, lambda b, g, pt: (0,) * len(shape))
    pe = jnp.stack([pea, peb]).reshape(CMP_LEN // CMP_STRIDE, CMP_STRIDE, 1, w)
    wl = wc.reshape(CMP_STRIDE, w, 2 * w)
    return pl.pallas_call(
        _cmp_proj_paged_body,
        grid_spec=pltpu.PrefetchScalarGridSpec(
            num_scalar_prefetch=1,
            grid=(B, n_pages // CMP_PAGES),
            in_specs=[pl.BlockSpec(memory_space=pl.ANY), const(pe.shape), const(wl.shape)],
            out_specs=pl.BlockSpec((1, CMP_PAGES * rows, 2 * w), lambda b, g, pt: (b, g, 0)),
            scratch_shapes=[pltpu.VMEM((2, CMP_PAGES, KV_W, PAGE_SIZE), F32), pltpu.SemaphoreType.DMA((2,)),
                            pltpu.VMEM((KV_W // LANE, CMP_PAGES * PAGE_SIZE, LANE), F32)],
        ),
        out_shape=jax.ShapeDtypeStruct((B, n_pages * rows, 2 * w), F32),
        compiler_params=_params("arbitrary", "arbitrary"),
        name="cmp_proj_paged",
    )(page_table, cache, pe, wl)


def _cmp_finish_body(ab_ref, b1_ref, w2_ref, kcn_ref, g128_ref, kc_ref, vc_ref):
    ab = ab_ref[0]
    n = ab.shape[0]
    w = N_SLOT * HEAD_DIM
    pre = ab[:, :w] + pltpu.roll(ab[:, w:], n - 1, axis=0) + b1_ref[...]
    hid = pre * _sigmoid(pre)
    out = _dot(hid.astype(BF16), w2_ref[...])
    k = out[:, :LANE]
    kc_ref[0] = k * lax.rsqrt(_gsum(k * k, g128_ref[...]) * (1.0 / HEAD_DIM) + RMS_EPS) * kcn_ref[...]
    vc_ref[0] = out[:, LANE:]


def _cmp_finish(ab, b1, w2b, kc_norm):
    B, n, _ = ab.shape
    const = lambda shape: pl.BlockSpec(shape, lambda b: (0,) * len(shape))
    w = N_SLOT * HEAD_DIM
    return pl.pallas_call(
        _cmp_finish_body,
        grid=(B,),
        in_specs=[pl.BlockSpec((1, n, 2 * w), lambda b: (b, 0, 0)), const((1, w)), const((w, w)), const((1, LANE)),
                  const((LANE, LANE))],
        out_specs=[pl.BlockSpec((1, n, LANE), lambda b: (b, 0, 0))] * 2,
        out_shape=[jax.ShapeDtypeStruct((B, n, LANE), F32)] * 2,
        compiler_params=_params("parallel"),
        name="cmp_finish",
    )(ab, b1, w2b, jnp.tile(kc_norm, KV_HEADS).reshape(1, LANE), _block_ones(LANE, HEAD_DIM))


def _cmp_attn_body(pos0, nc, nb, nbp, blocks_on_rows, q_ref, kc_ref, vc_ref, ovl_ref, o_ref, sel_ref):
    tq = q_ref.shape[1]
    ncp = kc_ref.shape[1]
    q = q_ref[0] * (HEAD_DIM ** -0.5)
    t0 = pos0 + pl.program_id(1) * tq
    t = t0 + lax.broadcasted_iota(jnp.int32, (tq, 1), 0)
    cidx = lax.broadcasted_iota(jnp.int32, (1, ncp), 1)
    cmask = (cidx * CMP_STRIDE + (CMP_LEN - 1) <= t) & (cidx < nc)
    if blocks_on_rows:
        tt = t0 + lax.broadcasted_iota(jnp.int32, (1, tq), 1)
        j = lax.broadcasted_iota(jnp.int32, (nbp, 1), 0)
    else:
        tt = t
        j = lax.broadcasted_iota(jnp.int32, (1, nbp), 1)
    cur = tt // SEL_BLOCK
    valid = (j * SEL_BLOCK <= tt) & (j < nb)
    forced = (j == 0) | (j == cur) | (j == cur - 1)
    for g in range(KV_HEADS):
        kcg = kc_ref[0, :, g * HEAD_DIM:(g + 1) * HEAD_DIM].astype(BF16)
        vcg = vc_ref[0, :, g * HEAD_DIM:(g + 1) * HEAD_DIM].astype(BF16)
        s = _dot_nt(_stack_heads(q, g).astype(BF16), kcg).reshape(KV_GROUP, tq, ncp)
        s = jnp.where(cmask[None], s, NEG_INF)
        e = jnp.where(cmask[None], jnp.exp(s - jnp.max(s, axis=-1, keepdims=True)), 0.0)
        p = e / jnp.maximum(jnp.sum(e, axis=-1, keepdims=True), 1e-30)
        o = _dot(p.reshape(KV_GROUP * tq, ncp).astype(BF16), vcg)
        for m in range(KV_GROUP):
            h = g * KV_GROUP + m
            o_ref[0, :, h * HEAD_DIM:(h + 1) * HEAD_DIM] = o[m * tq:(m + 1) * tq]
        hi, lo = _split2(jnp.sum(p, axis=0))
        if blocks_on_rows:
            imp = _dot_nt(ovl_ref[...], hi) + _dot_nt(ovl_ref[...], lo)
        else:
            imp = _dot(hi, ovl_ref[...]) + _dot(lo, ovl_ref[...])
        score = jnp.where(valid, jnp.where(forced, FORCE_SCORE, imp), NEG_INF)
        cnt = jnp.zeros(score.shape, jnp.int32)
        for jp in range(nb):
            cj = score[jp:jp + 1, :] if blocks_on_rows else score[:, jp:jp + 1]
            cnt = cnt + jnp.where(j > jp, jnp.where(cj >= score, 1, 0), jnp.where(cj > score, 1, 0))
        picked = (cnt < N_SEL).astype(F32)
        if blocks_on_rows:
            sel_ref[0, g] = picked
        else:
            sel_ref[0, :, g * nbp:(g + 1) * nbp] = picked


def _cmp_attn(q, kc, vc, pos0, nc, nb, tq, blocks_on_rows):
    B, Tq, _ = q.shape
    ncp = kc.shape[1]
    nbp = -(-nb // SEL_BLOCK) * SEL_BLOCK
    c0 = jnp.arange(ncp)[:, None] * CMP_STRIDE
    jj = jnp.arange(nbp)[None, :]
    ovl = ((c0 < (jj + 1) * SEL_BLOCK) & (c0 + CMP_LEN > jj * SEL_BLOCK) & (jnp.arange(ncp)[:, None] < nc)
           & (jj < nb)).astype(BF16)
    if blocks_on_rows:
        ovl = ovl.T
        sel_spec = pl.BlockSpec((1, KV_HEADS, nbp, tq), lambda b, i: (b, 0, 0, i))
        sel_shape = (B, KV_HEADS, nbp, Tq)
    else:
        sel_spec = pl.BlockSpec((1, tq, KV_HEADS * nbp), lambda b, i: (b, i, 0))
        sel_shape = (B, Tq, KV_HEADS * nbp)
    return pl.pallas_call(
        functools.partial(_cmp_attn_body, pos0, nc, nb, nbp, blocks_on_rows),
        grid=(B, Tq // tq),
        in_specs=[pl.BlockSpec((1, tq, C_MIX), lambda b, i: (b, i, 0)),
                  pl.BlockSpec((1, ncp, LANE), lambda b, i: (b, 0, 0)),
                  pl.BlockSpec((1, ncp, LANE), lambda b, i: (b, 0, 0)),
                  pl.BlockSpec(ovl.shape, lambda b, i: (0, 0))],
        out_specs=[pl.BlockSpec((1, tq, C_MIX), lambda b, i: (b, i, 0)), sel_spec],
        out_shape=[jax.ShapeDtypeStruct((B, Tq, C_MIX), F32), jax.ShapeDtypeStruct(sel_shape, F32)],
        compiler_params=_params("parallel", "parallel"),
        name="cmp_attn",
    )(q, kc, vc, ovl)


def _softmax_step(carry, s, mask, vb, v_feature_major=False):
    m_, l_, acc = carry
    s = jnp.where(mask, s, NEG_INF)
    m_new = jnp.maximum(m_, jnp.max(s, axis=-1, keepdims=True))
    alpha = jnp.exp(m_ - m_new)
    p = jnp.where(mask, jnp.exp(s - m_new), 0.0)
    pv = _dot_nt(p.astype(BF16), vb) if v_feature_major else _dot(p.astype(BF16), vb)
    return m_new, alpha * l_ + jnp.sum(p, axis=-1, keepdims=True), alpha * acc + pv


def _softmax_init(rows):
    return jnp.full((rows, 1), NEG_INF, F32), jnp.zeros((rows, 1), F32), jnp.zeros((rows, HEAD_DIM), F32)


def _stack_heads(x, g):
    return jnp.concatenate([x[:, (g * KV_GROUP + m) * HEAD_DIM:(g * KV_GROUP + m + 1) * HEAD_DIM]
                            for m in range(KV_GROUP)], axis=0)


def _gate_mix(ng, o_c, o_s, o_w, g, tq, o_ref, os_stacked=True):
    for m in range(KV_GROUP):
        h = g * KV_GROUP + m
        sl = slice(h * HEAD_DIM, (h + 1) * HEAD_DIM)
        rs = slice(m * tq, (m + 1) * tq)
        o_ref[0, :, sl] = (ng[:, h:h + 1] * o_c[:, sl]
                           + ng[:, N_HEADS + h:N_HEADS + h + 1] * (o_s[rs] if os_stacked else o_s[:, sl])
                           + ng[:, 2 * N_HEADS + h:2 * N_HEADS + h + 1] * o_w[rs])


SEL_TK = 1024


def _flash_step(carries, qgs, kvs, biases, tq):
    tk = kvs[0].shape[0]
    left = lax.broadcasted_iota(jnp.int32, (1, LANE), 1) < HEAD_DIM
    s = [_dot_nt(qg, kv).reshape(KV_GROUP, tq, tk) + bias[None] for qg, kv, bias in zip(qgs, kvs, biases)]
    m_new = [jnp.maximum(c[0], jnp.max(x, axis=-1, keepdims=True)) for c, x in zip(carries, s)]
    p = [jnp.exp(x - m).astype(BF16).reshape(KV_GROUP * tq, tk) for x, m in zip(s, m_new)]
    pv = [_dot(x, jnp.where(left, jnp.ones((), BF16), kv)).reshape(KV_GROUP, tq, LANE) for x, kv in zip(p, kvs)]
    return tuple((m, jnp.exp(c[0] - m) * c[1] + y) for c, m, y in zip(carries, m_new, pv))


def _flash_init(tq):
    return jnp.full((KV_GROUP, tq, 1), NEG_INF, F32), jnp.zeros((KV_GROUP, tq, LANE), F32)


def _flash_out(carry):
    acc = carry[1]
    return acc[:, :, HEAD_DIM:] / jnp.maximum(acc[:, :, :1], 1e-30)


def _nsa_prompt_body(q_ref, kvs_ref, kvw_ref, sel_ref, oc_ref, ng_ref, o_ref):
    tq = q_ref.shape[1]
    nbp = sel_ref.shape[2]
    q0 = pl.program_id(1) * tq
    q = q_ref[0] * (HEAD_DIM ** -0.5)
    t = q0 + lax.broadcasted_iota(jnp.int32, (tq, 1), 0)
    blk = lax.broadcasted_iota(jnp.int32, (nbp, 1), 0)
    zpad = jnp.zeros((tq, LANE - HEAD_DIM), F32)
    qgs = []
    sels = []
    for g in range(KV_HEADS):
        heads = [jnp.concatenate([q[:, (g * KV_GROUP + m) * HEAD_DIM:(g * KV_GROUP + m + 1) * HEAD_DIM], zpad], axis=1)
                 for m in range(KV_GROUP)]
        qgs.append(jnp.concatenate(heads, axis=0).astype(BF16))
        sels.append(sel_ref[0, g].astype(BF16))

    def sel_step(kt, carry):
        k0 = pl.multiple_of(kt * SEL_TK, SEL_TK)
        kpos = k0 + lax.broadcasted_iota(jnp.int32, (1, SEL_TK), 1)
        expand = (kpos // SEL_BLOCK == blk).astype(BF16)
        causal = kpos <= t
        biases = [jnp.where((_dot_tn(sels[g], expand) > 0.5) & causal, 0.0, NEG_INF) for g in range(KV_HEADS)]
        return _flash_step(carry, qgs, [kvs_ref[g, 0, pl.ds(k0, SEL_TK), :] for g in range(KV_HEADS)], biases, tq)

    res_s = lax.fori_loop(0, (q0 + tq + SEL_TK - 1) // SEL_TK, sel_step, (_flash_init(tq),) * KV_HEADS)

    span = WINDOW + tq
    w0 = pl.multiple_of(jnp.maximum(q0 - WINDOW, 0), tq)
    wpos = w0 + lax.broadcasted_iota(jnp.int32, (1, span), 1)
    wbias = jnp.where((wpos <= t) & (wpos > t - WINDOW), 0.0, NEG_INF)
    res_w = _flash_step((_flash_init(tq),) * KV_HEADS, qgs,
                        [kvw_ref[g, 0, pl.ds(w0, span), :] for g in range(KV_HEADS)], [wbias] * KV_HEADS, tq)

    ng = ng_ref[0]
    oc = oc_ref[0]
    for g in range(KV_HEADS):
        o_s = _flash_out(res_s[g])
        o_w = _flash_out(res_w[g])
        for m in range(KV_GROUP):
            h = g * KV_GROUP + m
            sl = slice(h * HEAD_DIM, (h + 1) * HEAD_DIM)
            o_ref[0, :, sl] = (ng[:, h:h + 1] * oc[:, sl] + ng[:, N_HEADS + h:N_HEADS + h + 1] * o_s[m]
                               + ng[:, 2 * N_HEADS + h:2 * N_HEADS + h + 1] * o_w[m])


def _nsa_prompt(q, kvs_g, kvw_g, sel, o_c, ng, tq):
    B, T, _ = q.shape
    tile = lambda w: pl.BlockSpec((1, tq, w), lambda b, i: (b, i, 0))
    whole = pl.BlockSpec((KV_HEADS, 1, T, LANE), lambda b, i: (0, b, 0, 0))
    return pl.pallas_call(
        _nsa_prompt_body,
        grid=(B, T // tq),
        in_specs=[tile(C_MIX), whole, whole, pl.BlockSpec((1, KV_HEADS, sel.shape[2], tq), lambda b, i: (b, 0, 0, i)),
                  tile(C_MIX), tile(LANE)],
        out_specs=tile(C_MIX),
        out_shape=jax.ShapeDtypeStruct((B, T, C_MIX), F32),
        compiler_params=_params("parallel", "arbitrary"),
        name="nsa_prompt",
    )(q, kvs_g, kvw_g, sel, o_c, ng)


DEC_ROWS = 8


def _nsa_decode_body(mode, pages, pos_q0, pos_k0, pt_ref, q_ref, pool_ref, new_ref, *rest):
    if mode == "sel":
        selst_ref, selnew_ref, o_ref, buf, sem, m_scr, l_scr, acc_scr = rest
    else:
        oc_ref, os_ref, ng_ref, o_ref, buf, sem, m_scr, l_scr, acc_scr = rest
    st = pl.program_id(1)
    tq = DEC_ROWS
    tk = pages * pool_ref.shape[2]
    bps = tk // SEL_BLOCK

    slot = _paged_fetch(pt_ref, pool_ref, buf, sem, pages)

    @pl.when(st == 0)
    def _():
        m_scr[...] = jnp.full(m_scr.shape, NEG_INF, F32)
        l_scr[...] = jnp.zeros(l_scr.shape, F32)
        acc_scr[...] = jnp.zeros(acc_scr.shape, F32)

    q = q_ref[0] * (HEAD_DIM ** -0.5)
    t = pos_q0 + lax.broadcasted_iota(jnp.int32, (tq, 1), 0)

    def update(kv, feature_major, kpos, picked):
        for g in range(KV_HEADS):
            mask = (kpos <= t) & (picked(g) if mode == "sel" else (kpos > t - WINDOW))
            mask = jnp.concatenate([mask] * KV_GROUP, axis=0)
            qg = _stack_heads(q, g).astype(BF16)
            carry = (m_scr[g], l_scr[g], acc_scr[g])
            ks = slice(g * HEAD_DIM, (g + 1) * HEAD_DIM)
            vs = slice(LANE + g * HEAD_DIM, LANE + (g + 1) * HEAD_DIM)
            if feature_major:
                s, v = _dot(qg, kv[ks].astype(BF16)), kv[vs].astype(BF16)
            else:
                s, v = _dot_nt(qg, kv[:, ks].astype(BF16)), kv[:, vs].astype(BF16)
            m_scr[g], l_scr[g], acc_scr[g] = _softmax_step(carry, s, mask, v, feature_major)

    kidx = lax.broadcasted_iota(jnp.int32, (1, tk), 1)
    expand = (kidx // SEL_BLOCK == lax.broadcasted_iota(jnp.int32, (bps, 1), 0)).astype(BF16)
    update(jnp.concatenate([buf[slot, i] for i in range(pages)], axis=1), True, pos_k0 + st * tk + kidx,
           lambda g: _dot(selst_ref[0, 0, :, g * bps:(g + 1) * bps].astype(BF16), expand) > 0.5)

    @pl.when(st == pl.num_programs(1) - 1)
    def _():
        update(new_ref[0], False, pos_q0 + lax.broadcasted_iota(jnp.int32, (1, tq), 1),
               lambda g: selnew_ref[0, :, g:g + 1] > 0.5)
        for g in range(KV_HEADS):
            o = acc_scr[g] / jnp.maximum(l_scr[g], 1e-30)
            if mode == "sel":
                for m in range(KV_GROUP):
                    h = g * KV_GROUP + m
                    o_ref[0, :, h * HEAD_DIM:(h + 1) * HEAD_DIM] = o[m * tq:(m + 1) * tq]
            else:
                _gate_mix(ng_ref[0], oc_ref[0], os_ref[0], o, g, tq, o_ref, os_stacked=False)


def _nsa_decode(mode, q, pool, page_table, kv_new, extras, pages, pos_q0, pos_k0):
    B, n_pages = page_table.shape
    rowblk = lambda a: pl.BlockSpec((1,) + a.shape[1:], lambda b, s, pt: (b,) + (0,) * (a.ndim - 1))
    stepblk = lambda a: pl.BlockSpec((1, 1) + a.shape[2:], lambda b, s, pt: (b, s, 0, 0))
    rows = KV_GROUP * DEC_ROWS
    return pl.pallas_call(
        functools.partial(_nsa_decode_body, mode, pages, pos_q0, pos_k0),
        grid_spec=pltpu.PrefetchScalarGridSpec(
            num_scalar_prefetch=1,
            grid=(B, n_pages // pages),
            in_specs=[rowblk(q), pl.BlockSpec(memory_space=pl.ANY), rowblk(kv_new)]
                     + [stepblk(e) if e.ndim == 4 else rowblk(e) for e in extras],
            out_specs=pl.BlockSpec((1, DEC_ROWS, C_MIX), lambda b, s, pt: (b, 0, 0)),
            scratch_shapes=[pltpu.VMEM((2, pages, KV_W, pool.shape[2]), F32), pltpu.SemaphoreType.DMA((2,)),
                            pltpu.VMEM((KV_HEADS, rows, 1), F32), pltpu.VMEM((KV_HEADS, rows, 1), F32),
                            pltpu.VMEM((KV_HEADS, rows, HEAD_DIM), F32)],
        ),
        out_shape=jax.ShapeDtypeStruct((B, DEC_ROWS, C_MIX), F32),
        compiler_params=_params("arbitrary", "arbitrary"),
        name="nsa_decode_" + mode,
    )(page_table, q, pool, kv_new, *extras)


ROUTE_W = LANE


def _merge_body(x_ref, oa_ref, ob_ref, mg_ref, wa_ref, wb_ref, wo_ref, n2_ref, wrh_ref, wrl_ref, br_ref,
                x1_ref, h_ref, route_ref):
    mg = mg_ref[...]
    merged = (mg[:, :D_MODEL] * _dot(oa_ref[...].astype(BF16), wa_ref[...])
              + mg[:, D_MODEL:] * _dot(ob_ref[...].astype(BF16), wb_ref[...]))
    x1 = x_ref[...] + _dot(merged.astype(BF16), wo_ref[...])
    x1_ref[...] = x1
    h = x1 * lax.rsqrt(jnp.mean(x1 * x1, axis=-1, keepdims=True) + RMS_EPS) * n2_ref[...]
    h_ref[...] = h.astype(BF16)
    hh, hl = _split2(h)
    logits = _dot(hh, wrh_ref[...]) + _dot(hl, wrh_ref[...]) + _dot(hh, wrl_ref[...]) + br_ref[...]
    lane = lax.broadcasted_iota(jnp.int32, (1, ROUTE_W), 1)
    first = lambda hit: jnp.min(jnp.where(hit, lane, ROUTE_W), axis=-1, keepdims=True)
    is_g = lane < N_GROUPS
    gl = jnp.where(is_g, logits, NEG_INF)
    gmax = jnp.max(gl, axis=-1, keepdims=True)
    g_sel = first(gl == gmax)
    g_w = 1.0 / jnp.sum(jnp.where(is_g, jnp.exp(gl - gmax), 0.0), axis=-1, keepdims=True)
    in_grp = (lane >= N_GROUPS) & (lane < N_GROUPS + N_EXPERTS) & (((lane - N_GROUPS) >> 3) == g_sel)
    el = jnp.where(in_grp, logits, NEG_INF)
    v1 = jnp.max(el, axis=-1, keepdims=True)
    i1 = first(el == v1)
    el2 = jnp.where(lane == i1, NEG_INF, el)
    v2 = jnp.max(el2, axis=-1, keepdims=True)
    i2 = first(el2 == v2)
    d = jnp.exp(v2 - v1)
    w1 = g_w / (1.0 + d)
    w2 = g_w * d / (1.0 + d)
    route_ref[...] = jnp.where(lane == 0, (i1 - N_GROUPS).astype(F32),
                               jnp.where(lane == 1, (i2 - N_GROUPS).astype(F32),
                                         jnp.where(lane == 2, w1, jnp.where(lane == 3, w2, 0.0))))


def _merge(x2d, oa, ob, mg, wa, wb, wo, norm2, w_rg, b_rg, w_re, b_re, tm):
    n = x2d.shape[0]
    const = lambda shape: pl.BlockSpec(shape, lambda i: (0,) * len(shape))
    row = lambda w: pl.BlockSpec((tm, w), lambda i: (i, 0))
    wr = jnp.zeros((D_MODEL, ROUTE_W), F32).at[:, :N_GROUPS].set(w_rg).at[:, N_GROUPS:N_GROUPS + N_EXPERTS].set(w_re)
    br = jnp.zeros((1, ROUTE_W), F32).at[0, :N_GROUPS].set(b_rg).at[0, N_GROUPS:N_GROUPS + N_EXPERTS].set(b_re)
    wrh, wrl = _split2(wr)
    return pl.pallas_call(
        _merge_body,
        grid=(n // tm,),
        in_specs=[row(D_MODEL), row(C_MIX), row(C_MIX), row(2 * D_MODEL), const((C_MIX, D_MODEL)),
                  const((C_MIX, D_MODEL)), const((D_MODEL, D_MODEL)), const((1, D_MODEL)),
                  const((D_MODEL, ROUTE_W)), const((D_MODEL, ROUTE_W)), const((1, ROUTE_W))],
        out_specs=[row(D_MODEL), row(D_MODEL), row(ROUTE_W)],
        out_shape=[jax.ShapeDtypeStruct((n, D_MODEL), F32), jax.ShapeDtypeStruct((n, D_MODEL), BF16),
                   jax.ShapeDtypeStruct((n, ROUTE_W), F32)],
        compiler_params=_params("parallel"),
        name="merge_route",
    )(x2d, oa, ob, mg, wa.astype(BF16), wb.astype(BF16), wo.astype(BF16), norm2.reshape(1, D_MODEL), wrh, wrl, br)


MOE_TB = 256


def _moe_body(be_ref, nu_ref, x_ref, wg_ref, wu_ref, wd_ref, y_ref, wg_b, wu_b, wd_b):
    i = pl.program_id(0)

    @pl.when((i == 0) | (be_ref[i] != be_ref[jnp.maximum(i - 1, 0)]))
    def _():
        wg_b[...] = wg_ref[0].astype(BF16)
        wu_b[...] = wu_ref[0].astype(BF16)
        wd_b[...] = wd_ref[0].astype(BF16)

    @pl.when(i < nu_ref[0])
    def _():
        x = x_ref[...]
        gate = _dot(x, wg_b[...])
        hid = gate * _sigmoid(gate) * _dot(x, wu_b[...])
        y_ref[...] = _dot(hid.astype(BF16), wd_b[...])

    @pl.when(i >= nu_ref[0])
    def _():
        y_ref[...] = jnp.zeros(y_ref.shape, F32)


def _moe_experts(xbuf, blk_e, n_used, wg, wu, wd):
    n_blk = blk_e.shape[0]
    wspec = lambda shape: pl.BlockSpec((1,) + shape, lambda i, be, nu: (be[i], 0, 0))
    return pl.pallas_call(
        _moe_body,
        grid_spec=pltpu.PrefetchScalarGridSpec(
            num_scalar_prefetch=2,
            grid=(n_blk,),
            in_specs=[pl.BlockSpec((MOE_TB, D_MODEL), lambda i, be, nu: (i, 0)), wspec((D_MODEL, D_EXPERT)),
                      wspec((D_MODEL, D_EXPERT)), wspec((D_EXPERT, D_MODEL))],
            out_specs=pl.BlockSpec((MOE_TB, D_MODEL), lambda i, be, nu: (i, 0)),
            scratch_shapes=[pltpu.VMEM((D_MODEL, D_EXPERT), BF16), pltpu.VMEM((D_MODEL, D_EXPERT), BF16),
                            pltpu.VMEM((D_EXPERT, D_MODEL), BF16)],
        ),
        out_shape=jax.ShapeDtypeStruct((n_blk * MOE_TB, D_MODEL), F32),
        compiler_params=_params("arbitrary"),
        name="moe_experts",
    )(blk_e, n_used, xbuf, wg, wu, wd)


def _moe_dispatch(h, route):
    n = h.shape[0]
    expert = route[:, :2].astype(jnp.int32).reshape(-1)
    wts = route[:, 2:4]
    n_slots = 2 * n
    n_blk = -(-n_slots // MOE_TB) + N_EXPERTS
    onehot = expert[:, None] == jnp.arange(N_EXPERTS, dtype=jnp.int32)[None, :]
    counts = jnp.sum(onehot, axis=0, dtype=jnp.int32)
    c_start = jnp.cumsum(counts) - counts
    padded = (counts + MOE_TB - 1) // MOE_TB * MOE_TB
    p_end = jnp.cumsum(padded)
    p_start = p_end - padded
    order = jnp.argsort(expert, stable=True).astype(jnp.int32)
    rank = jnp.argsort(order).astype(jnp.int32)
    dest = (rank + jnp.sum(jnp.where(onehot, (p_start - c_start)[None, :], 0), axis=1)).reshape(n, 2)
    blk_e = jnp.minimum(jnp.sum(p_end[None, :] <= (jnp.arange(n_blk, dtype=jnp.int32) * MOE_TB)[:, None], axis=1),
                        N_EXPERTS - 1).astype(jnp.int32)
    n_used = (p_end[-1:] // MOE_TB).astype(jnp.int32)
    k_in_e = jnp.arange(n_blk * MOE_TB, dtype=jnp.int32) - jnp.repeat(p_start[blk_e], MOE_TB)
    src = order[jnp.clip(jnp.repeat(c_start[blk_e], MOE_TB) + k_in_e, 0, n_slots - 1)]
    row_tok = jnp.where(k_in_e < jnp.repeat(counts[blk_e], MOE_TB), src // 2, 0)
    return h[row_tok], blk_e, n_used, dest, wts


def _moe_apply(x1, dispatch, wg, wu, wd):
    xbuf, blk_e, n_used, dest, wts = dispatch
    ybuf = _moe_experts(xbuf, blk_e, n_used, wg, wu, wd)
    return x1 + wts[:, 0:1] * ybuf[dest[:, 0]] + wts[:, 1:2] * ybuf[dest[:, 1]]


def kernel(x_prompt, x_sample, cache_cmp_kv, cache_slc_kv, cache_win_kv, state_wkv, state_shift, page_table, norm1, w_in, mu_shift, w0, w_decay_up, a0, w_iclr_up, w_gate_up, k_k, k_a, r_k, ln_x_w, ln_x_b, q_norm, kc_norm, ks_norm, kw_norm, cmp_pe_k, cmp_w1_k, cmp_b1_k, cmp_w2_k, cmp_pe_v, cmp_w1_v, cmp_b1_v, cmp_w2_v, w_branch_a, w_branch_b, w_out, norm2, w_route_group, b_route_group, w_route_expert, b_route_expert, w_exp_gate, w_exp_up, w_exp_down):
    assert norm1.shape[0] == 1, "single-layer trunk"
    Bp, Tp, _ = x_prompt.shape
    Bs, Ts, _ = x_sample.shape
    n_pool = cache_cmp_kv.shape[1]
    past = page_table.shape[1] * PAGE_SIZE
    n_buf = cache_win_kv.shape[2]
    kv5 = lambda a, b, t: a.reshape(1, b, t, 2, KV_HEADS, HEAD_DIM)
    kv5t = lambda a: a.reshape(a.shape[0], 2, KV_HEADS, HEAD_DIM, a.shape[2]).transpose(0, 4, 1, 2, 3)[None]

    w_pad = _pad_w_in(w_in[0])
    rw_p = (mu_shift[0], w0[0], w_decay_up[0], a0[0], w_iclr_up[0], w_gate_up[0], k_k[0], k_a[0],
            r_k[0].reshape(-1), ln_x_w[0], ln_x_b[0])
    wc, pea, peb, b1, w2b = _cmp_weights(cmp_pe_k[0], cmp_w1_k[0], cmp_b1_k[0], cmp_w2_k[0],
                                         cmp_pe_v[0], cmp_w1_v[0], cmp_b1_v[0], cmp_w2_v[0])
    merge_p = (w_branch_a[0], w_branch_b[0], w_out[0], norm2[0], w_route_group[0], b_route_group[0],
               w_route_expert[0], b_route_expert[0])
    moe_w = (w_exp_gate[0], w_exp_up[0], w_exp_down[0])

    xp = x_prompt.reshape(Bp * Tp, D_MODEL)
    zrw, q, kvc, _, _, ng, mg, kvs_g, kvw_g, kvc_t, kvs_t, kvw_t = _in_proj(xp, norm1[0], w_pad, q_norm[0], ks_norm[0],
                                                                            kw_norm[0], 512, Tp)
    zrw3 = zrw.reshape(Bp, Tp, RW_PAD)
    oa, wkv_p = _rwkv(zrw3, jnp.zeros((Bp, 1, RW_PAD), F32), jnp.zeros((Bp, N_HEADS, HEAD_DIM, HEAD_DIM), F32),
                      Tp, *rw_p)
    ab = _cmp_proj(kvc.reshape(Bp * Tp // CMP_STRIDE, CH_W), pea, peb, wc, min(256, Bp * Tp // CMP_STRIDE))
    kc, vc = _cmp_finish(ab.reshape(Bp, Tp // CMP_STRIDE, -1), b1, w2b, kc_norm[0])
    q3 = q.reshape(Bp, Tp, C_MIX)
    o_c, sel = _cmp_attn(q3, kc, vc, 0, (Tp - CMP_LEN) // CMP_STRIDE + 1, Tp // SEL_BLOCK, 128, True)
    ob = _nsa_prompt(q3, kvs_g.reshape(KV_HEADS, Bp, Tp, LANE), kvw_g.reshape(KV_HEADS, Bp, Tp, LANE), sel, o_c,
                     ng.reshape(Bp, Tp, LANE), 128)
    x1, h, route = _merge(xp, oa.reshape(Bp * Tp, C_MIX), ob.reshape(Bp * Tp, C_MIX), mg, *merge_p, 512)
    disp_p = _moe_dispatch(h, route)
    keep_p = min(WINDOW, Tp)

    xs = x_sample.reshape(Bs * Ts, D_MODEL)
    zrw_s, q_s, kvc_s, kvs_s, kvw_s, ng_s, mg_s = _in_proj(xs, norm1[0], w_pad, q_norm[0], ks_norm[0], kw_norm[0],
                                                           Bs * Ts, Bs * Ts)[:7]
    zrw_s3 = zrw_s.reshape(Bs, Ts, RW_PAD)
    oa_s, wkv_s = _rwkv(jnp.pad(zrw_s3, ((0, 0), (0, RW_CHUNK - Ts), (0, 0))),
                        jnp.pad(state_shift[0], ((0, 0), (0, RW_PAD - RW_IN)))[:, None], state_wkv[0], Ts, *rw_p)
    nc_s = (past + Ts - CMP_LEN) // CMP_STRIDE + 1
    assert (nc_s + CMP_LEN // CMP_STRIDE - 1) * CMP_STRIDE <= past, "compression blocks only cover cached rows"
    feature_major = lambda c: c.transpose(0, 2, 3, 4, 1).reshape(c.shape[0], KV_W, c.shape[1])
    ab_s = _cmp_proj_paged(feature_major(cache_cmp_kv[0]), page_table, pea, peb, wc)
    kc_s, vc_s = _cmp_finish(ab_s, b1, w2b, kc_norm[0])
    rows8 = lambda a: jnp.pad(a.reshape(Bs, Ts, -1), ((0, 0), (0, DEC_ROWS - Ts), (0, 0)))
    q8 = rows8(q_s)
    oc_s, sel_s = _cmp_attn(q8, kc_s, vc_s, past, nc_s, -(-(past + Ts) // SEL_BLOCK), DEC_ROWS, False)
    assert past % SEL_BLOCK == 0 and Ts <= SEL_BLOCK, "the new rows share one selection block"
    n_st = page_table.shape[1] // CMP_PAGES
    sel4 = sel_s.reshape(Bs, DEC_ROWS, KV_HEADS, -1)
    sel_steps = sel4[..., :past // SEL_BLOCK].reshape(Bs, DEC_ROWS, KV_HEADS, n_st, -1)
    sel_steps = sel_steps.transpose(0, 3, 1, 2, 4).reshape(Bs, n_st, DEC_ROWS, -1)
    sel_new = jnp.pad(sel4[..., past // SEL_BLOCK], ((0, 0), (0, 0), (0, LANE - KV_HEADS)))
    os_s = _nsa_decode("sel", q8, feature_major(cache_slc_kv[0]), page_table, rows8(kvs_s),
                       (sel_steps, sel_new), CMP_PAGES, past, 0)
    ob_s = _nsa_decode("win", q8, feature_major(cache_win_kv[0]), jnp.arange(Bs, dtype=jnp.int32).reshape(Bs, 1),
                       rows8(kvw_s), (oc_s, os_s, rows8(ng_s)), 1, past, past - n_buf)
    x1_s, h_s, route_s = _merge(xs, oa_s[:, :Ts].reshape(Bs * Ts, C_MIX), ob_s[:, :Ts].reshape(Bs * Ts, C_MIX), mg_s,
                                *merge_p, Bs * Ts)
    disp_p, x1_s = lax.optimization_barrier((disp_p, x1_s))
    y_p = _moe_apply(x1, disp_p, *moe_w).reshape(Bp, Tp, D_MODEL)
    y_s = _moe_apply(x1_s, _moe_dispatch(h_s, route_s), *moe_w).reshape(Bs, Ts, D_MODEL)
    keep_s = min(WINDOW, n_buf + Ts)
    win_s = jnp.concatenate([cache_win_kv[0].reshape(Bs, n_buf, KV_W), kvw_s.reshape(Bs, Ts, KV_W)], axis=1)

    return (y_p, y_s,
            kv5t(kvc_t), kv5t(kvs_t), kv5t(kvw_t[:, :, Tp - keep_p:]),
            wkv_p[None], zrw3[:, -1, :RW_IN][None],
            kv5(kvc_s, Bs, Ts), kv5(kvs_s, Bs, Ts), kv5(win_s[:, n_buf + Ts - keep_s:], Bs, keep_s),
            wkv_s[None], zrw_s3[:, -1, :RW_IN][None])
```
